```python
import math
import jax, jax.numpy as jnp
from jax import lax
import numpy as np

D_MODEL = 1024
BATCH = 4
SEQ = 4096
DEPTH = 2

RET_HEADS = 4
RET_HEAD_DIM = 128
RET_CHUNK = 128
DSA_HEADS = 8
DSA_HEAD_DIM = 64
DSA_PATTERNS = ((128, 1), (512, 4), (2048, 16))
DSA_BLOCK = 128
T5_BUCKETS = 32
T5_MAX_DIST = 2048
MLSTM_HEADS = 4
MLSTM_HEAD_DIM = 256
MLSTM_CHUNK = 128
MLSTM_CONV = 4
D_FF = 2816
FFN_CONV = 3
LN_EPS = 1e-5
DEEPNORM_ALPHA = (2.0 * DEPTH) ** 0.25
DEEPNORM_BETA = (8.0 * DEPTH) ** -0.25

N_EVEN = (DEPTH + 1) // 2
N_ODD = DEPTH // 2
RET_W = RET_HEADS * RET_HEAD_DIM
DSA_W = DSA_HEADS * DSA_HEAD_DIM
MIX_W_EVEN = RET_W + DSA_W
EVEN_IN = 4 * RET_W + 3 * DSA_W
EVEN_SPLITS = (RET_W, 2 * RET_W, 3 * RET_W, 4 * RET_W, 4 * RET_W + DSA_W, 4 * RET_W + 2 * DSA_W)
MLSTM_W = MLSTM_HEADS * MLSTM_HEAD_DIM
ODD_IN = 4 * MLSTM_W + 2 * MLSTM_HEADS
ODD_SPLITS = (2 * MLSTM_W, 3 * MLSTM_W, 4 * MLSTM_W)

kernel_name = "hybrid_retention_dilated_mlstm_deepnorm"


def layer_norm(x, g, b):
    xf = x.astype(jnp.float32)
    mu = jnp.mean(xf, -1, keepdims=True)
    var = jnp.mean(jnp.square(xf - mu), -1, keepdims=True)
    return ((xf - mu) * lax.rsqrt(var + LN_EPS) * g + b).astype(x.dtype)


def head_norm(y):
    mu = jnp.mean(y, -1, keepdims=True)
    var = jnp.mean(jnp.square(y - mu), -1, keepdims=True)
    return (y - mu) * lax.rsqrt(var + LN_EPS)


def split_heads(t, n_heads):
    B, S, W = t.shape
    return t.reshape(B, S, n_heads, W // n_heads).transpose(0, 2, 1, 3)


def merge_heads(t):
    B, H, S, d = t.shape
    return t.transpose(0, 2, 1, 3).reshape(B, S, H * d)


def causal_dwconv(x, w):
    K, C = w.shape
    return lax.conv_general_dilated(x, w[:, None, :].astype(x.dtype), window_strides=(1,), padding=[(K - 1, 0)], dimension_numbers=("NWC", "WIO", "NWC"), feature_group_count=C)


def rotary(x):
    S, d = x.shape[-2], x.shape[-1]
    inv = 1.0 / (10000.0 ** (jnp.arange(0, d, 2, dtype=jnp.float32) / d))
    ang = jnp.arange(S, dtype=jnp.float32)[:, None] * inv[None, :]
    cos, sin = jnp.cos(ang), jnp.sin(ang)
    x1, x2 = x[..., : d // 2], x[..., d // 2:]
    return jnp.concatenate([x1 * cos - x2 * sin, x1 * sin + x2 * cos], -1)


def retention(q, k, v):
    B, H, S, d = q.shape
    C = RET_CHUNK
    N = S // C
    log_gamma = jnp.log1p(-jnp.exp2(-5.0 - jnp.arange(H, dtype=jnp.float32)))
    qf = rotary(q.astype(jnp.float32))
    kf = rotary(k.astype(jnp.float32)) * (d ** -0.5)
    qc = qf.reshape(B, H, N, C, d)
    kc = kf.reshape(B, H, N, C, d)
    vc = v.astype(jnp.float32).reshape(B, H, N, C, d)
    idx = jnp.arange(C, dtype=jnp.float32)
    rel = idx[:, None] - idx[None, :]
    decay = jnp.where(rel >= 0, jnp.exp(log_gamma[:, None, None] * jnp.maximum(rel, 0.0)), 0.0)
    scores = jnp.einsum('bhncd,bhnsd->bhncs', qc, kc) * decay[None, :, None]
    y_intra = jnp.einsum('bhncs,bhnse->bhnce', scores, vc)
    k_w = jnp.exp(log_gamma[:, None] * (C - 1 - idx)[None, :])
    kv = jnp.einsum('bhnsd,bhnse->bhnde', kc * k_w[None, :, None, :, None], vc)
    chunk_decay = jnp.exp(log_gamma * C)[None, :, None, None]

    def step(R, kv_n):
        return chunk_decay * R + kv_n, R

    _, R_prev = lax.scan(step, jnp.zeros((B, H, d, d), jnp.float32), jnp.moveaxis(kv, 2, 0))
    R_prev = jnp.moveaxis(R_prev, 0, 2)
    q_w = jnp.exp(log_gamma[:, None] * (idx + 1.0)[None, :])
    y_inter = jnp.einsum('bhncd,bhnde->bhnce', qc * q_w[None, :, None, :, None], R_prev)
    return (y_intra + y_inter).reshape(B, H, S, d)


def t5_bucket(dist):
    exact = T5_BUCKETS // 2
    n = jnp.maximum(dist, 0)
    large = exact + (jnp.log(jnp.maximum(n, 1).astype(jnp.float32) / exact) / math.log(T5_MAX_DIST / exact) * (T5_BUCKETS - exact)).astype(jnp.int32)
    large = jnp.minimum(large, T5_BUCKETS - 1)
    return jnp.where(n < exact, n, large)


def dilated_branch(q, k, v, rel_bias, window, dilation):
    B, H, S, d = q.shape
    L = S // dilation
    W = window // dilation
    blk = DSA_BLOCK
    nb = -(-L // blk)
    Lp = nb * blk

    def to_blocks(t):
        t = t.astype(jnp.float32).reshape(B, H, L, dilation, d).swapaxes(2, 3)
        t = jnp.pad(t, ((0, 0), (0, 0), (0, 0), (0, Lp - L), (0, 0)))
        return t.reshape(B, H, dilation, nb, blk, d)

    def with_prev(t):
        prev = jnp.pad(t[:, :, :, :-1], ((0, 0), (0, 0), (0, 0), (1, 0), (0, 0), (0, 0)))
        return jnp.concatenate([prev, t], axis=4)

    qb = to_blocks(q)
    kw = with_prev(to_blocks(k))
    vw = with_prev(to_blocks(v))
    qi = jnp.arange(blk)[:, None] + blk
    kj = jnp.arange(2 * blk)[None, :]
    dist = qi - kj
    bias = rel_bias[t5_bucket(dist * dilation)].astype(jnp.float32).transpose(2, 0, 1)
    blk_start = jnp.arange(nb)[:, None, None] * blk
    valid = (dist >= 0) & (dist <= W) & (blk_start + kj - blk >= 0)
    logits = jnp.einsum('bhrnqd,bhrnkd->bhrnqk', qb, kw) * (d ** -0.5) + bias[None, :, None, None]
    logits = jnp.where(valid, logits, -jnp.inf)
    m = jnp.max(logits, -1)
    p = jnp.exp(logits - m[..., None])
    s = jnp.sum(p, -1)
    o = jnp.einsum('bhrnqk,bhrnkd->bhrnqd', p, vw)

    def from_blocks(t):
        tail = t.shape[5:]
        t = t.reshape((B, H, dilation, Lp) + tail)[:, :, :, :L]
        return jnp.swapaxes(t, 2, 3).reshape((B, H, S) + tail)

    return from_blocks(m), from_blocks(s), from_blocks(o)


def dilated_attention(q, k, v, rel_bias):
    outs = [dilated_branch(q, k, v, rel_bias, w, r) for (w, r) in DSA_PATTERNS]
    M = jnp.max(jnp.stack([m for (m, _, _) in outs]), axis=0)
    num = sum(jnp.exp(m - M)[..., None] * o for (m, _, o) in outs)
    den = sum(jnp.exp(m - M) * s for (m, s, _) in outs)
    return num / den[..., None]


def mlstm(q, k, v, log_i, log_f):
    B, H, S, d = q.shape
    C = MLSTM_CHUNK
    N = S // C
    qc = q.astype(jnp.float32).reshape(B, H, N, C, d)
    kc = (k.astype(jnp.float32) * (d ** -0.5)).reshape(B, H, N, C, d)
    vc = v.astype(jnp.float32).reshape(B, H, N, C, d)
    li = log_i.reshape(B, H, N, C)
    a = jnp.cumsum(log_f.reshape(B, H, N, C), axis=-1)
    g = a[..., -1]
    w_state = g[..., None] - a + li
    m_loc = jnp.max(w_state, -1)
    ew = jnp.exp(w_state - m_loc[..., None])[..., None]
    kv = jnp.einsum('bhncd,bhnce->bhnde', kc * ew, vc)
    ksum = jnp.sum(kc * ew, axis=3)

    def step(carry, inp):
        C_s, n_s, m_s = carry
        kv_n, ks_n, g_n, m_n = inp
        m_new = jnp.maximum(g_n + m_s, m_n)
        a_old = jnp.exp(g_n + m_s - m_new)
        a_loc = jnp.exp(m_n - m_new)
        C_new = a_old[..., None, None] * C_s + a_loc[..., None, None] * kv_n
        n_new = a_old[..., None] * n_s + a_loc[..., None] * ks_n
        return (C_new, n_new, m_new), (C_s, n_s, m_s)

    init = (jnp.zeros((B, H, d, d), jnp.float32), jnp.zeros((B, H, d), jnp.float32), jnp.zeros((B, H), jnp.float32))
    xs = (jnp.moveaxis(kv, 2, 0), jnp.moveaxis(ksum, 2, 0), jnp.moveaxis(g, 2, 0), jnp.moveaxis(m_loc, 2, 0))
    _, (C_prev, n_prev, m_prev) = lax.scan(step, init, xs)
    C_prev = jnp.moveaxis(C_prev, 0, 2)
    n_prev = jnp.moveaxis(n_prev, 0, 2)
    m_prev = jnp.moveaxis(m_prev, 0, 2)
    D = a[..., :, None] - a[..., None, :] + li[..., None, :]
    causal = jnp.tril(jnp.ones((C, C), dtype=bool))
    D = jnp.where(causal, D, -jnp.inf)
    inter_log = a + m_prev[..., None]
    m = jnp.maximum(inter_log, jnp.max(D, -1))
    P = jnp.exp(D - m[..., None]) * jnp.einsum('bhncd,bhnsd->bhncs', qc, kc)
    w_inter = jnp.exp(inter_log - m)
    num = jnp.einsum('bhncs,bhnse->bhnce', P, vc) + w_inter[..., None] * jnp.einsum('bhncd,bhnde->bhnce', qc, C_prev)
    den = jnp.sum(P, -1) + w_inter * jnp.einsum('bhncd,bhnd->bhnc', qc, n_prev)
    h = num / jnp.maximum(jnp.abs(den), jnp.exp(-m))[..., None]
    return h.reshape(B, H, S, d)


def even_mixer(x, w_in, w_out, rel_bias):
    h = x @ w_in
    q_r, k_r, v_r, g_r, q_d, k_d, v_d = jnp.split(h, list(EVEN_SPLITS), axis=-1)
    y_r = retention(split_heads(q_r, RET_HEADS), split_heads(k_r, RET_HEADS), split_heads(v_r, RET_HEADS))
    y_r = merge_heads(head_norm(y_r)) * jax.nn.silu(g_r.astype(jnp.float32))
    y_d = merge_heads(dilated_attention(split_heads(q_d, DSA_HEADS), split_heads(k_d, DSA_HEADS), split_heads(v_d, DSA_HEADS), rel_bias))
    y = jnp.concatenate([y_r, y_d], axis=-1).astype(x.dtype)
    return y @ w_out


def odd_mixer(x, w_in, gate_b, conv_w, w_out):
    h = x @ w_in
    qk, v, o, gates = jnp.split(h, list(ODD_SPLITS), axis=-1)
    qk = jax.nn.silu(causal_dwconv(qk, conv_w))
    q, k = jnp.split(qk, 2, axis=-1)
    gates = gates.astype(jnp.float32) + gate_b
    log_i = gates[..., :MLSTM_HEADS].transpose(0, 2, 1)
    log_f = jax.nn.log_sigmoid(gates[..., MLSTM_HEADS:]).transpose(0, 2, 1)
    y = mlstm(split_heads(q, MLSTM_HEADS), split_heads(k, MLSTM_HEADS), split_heads(v, MLSTM_HEADS), log_i, log_f)
    y = merge_heads(y) * jax.nn.sigmoid(o.astype(jnp.float32))
    return y.astype(x.dtype) @ w_out


def conv_ffn(x, w_up, conv_w, conv_b, w_down):
    gate, up = jnp.split(x @ w_up, 2, axis=-1)
    act = jax.nn.silu(causal_dwconv(gate, conv_w) + conv_b)
    return (act * up) @ w_down


def setup_inputs(seed: int = 0) -> dict:
    key = jax.random.key(seed)
    ks = jax.random.split(key, 16)
    f32 = jnp.float32

    def nrm(k, shape, scale):
        return jax.random.normal(k, shape, f32) * scale

    d_in = D_MODEL ** -0.5
    x = nrm(ks[0], (BATCH, SEQ, D_MODEL), 1.0)
    even_scale = np.ones(EVEN_IN, np.float32)
    even_scale[2 * RET_W:3 * RET_W] = DEEPNORM_BETA
    even_scale[4 * RET_W + 2 * DSA_W:] = DEEPNORM_BETA
    even_w_in = nrm(ks[1], (N_EVEN, D_MODEL, EVEN_IN), d_in) * jnp.asarray(even_scale)
    even_w_out = nrm(ks[2], (N_EVEN, MIX_W_EVEN, D_MODEL), MIX_W_EVEN ** -0.5 * DEEPNORM_BETA)
    rel_bias = nrm(ks[3], (T5_BUCKETS, DSA_HEADS), 0.5)
    odd_scale = np.ones(ODD_IN, np.float32)
    odd_scale[2 * MLSTM_W:3 * MLSTM_W] = DEEPNORM_BETA
    odd_w_in = nrm(ks[4], (N_ODD, D_MODEL, ODD_IN), d_in) * jnp.asarray(odd_scale)
    i_bias = nrm(ks[5], (N_ODD, MLSTM_HEADS), 0.1)
    f_bias = jnp.linspace(3.0, 6.0, MLSTM_HEADS, dtype=f32) + nrm(ks[6], (N_ODD, MLSTM_HEADS), 0.1)
    odd_gate_b = jnp.concatenate([i_bias, f_bias], axis=-1)
    odd_conv_w = nrm(ks[7], (N_ODD, MLSTM_CONV, 2 * MLSTM_W), MLSTM_CONV ** -0.5)
    odd_w_out = nrm(ks[8], (N_ODD, MLSTM_W, D_MODEL), MLSTM_W ** -0.5 * DEEPNORM_BETA)
    ffn_w_up = nrm(ks[9], (DEPTH, D_MODEL, 2 * D_FF), d_in * DEEPNORM_BETA)
    ffn_conv_w = nrm(ks[10], (DEPTH, FFN_CONV, D_FF), FFN_CONV ** -0.5)
    ffn_conv_b = nrm(ks[11], (DEPTH, D_FF), 0.02)
    ffn_w_down = nrm(ks[12], (DEPTH, D_FF, D_MODEL), D_FF ** -0.5 * DEEPNORM_BETA)
    ln_g = 1.0 + nrm(ks[13], (DEPTH, 2, D_MODEL), 0.02)
    ln_b = nrm(ks[14], (DEPTH, 2, D_MODEL), 0.02)
    return {"x": x, "even_w_in": even_w_in, "even_w_out": even_w_out, "rel_bias": rel_bias, "odd_w_in": odd_w_in, "odd_gate_b": odd_gate_b, "odd_conv_w": odd_conv_w, "odd_w_out": odd_w_out, "ffn_w_up": ffn_w_up, "ffn_conv_w": ffn_conv_w, "ffn_conv_b": ffn_conv_b, "ffn_w_down": ffn_w_down, "ln_g": ln_g, "ln_b": ln_b}


def reference(x, even_w_in, even_w_out, rel_bias, odd_w_in, odd_gate_b, odd_conv_w, odd_w_out, ffn_w_up, ffn_conv_w, ffn_conv_b, ffn_w_down, ln_g, ln_b):
    for layer in range(DEPTH):
        j = layer // 2
        if layer % 2 == 0:
            mix = even_mixer(x, even_w_in[j], even_w_out[j], rel_bias)
        else:
            mix = odd_mixer(x, odd_w_in[j], odd_gate_b[j], odd_conv_w[j], odd_w_out[j])
        x = layer_norm(DEEPNORM_ALPHA * x + mix, ln_g[layer, 0], ln_b[layer, 0])
        ffn = conv_ffn(x, ffn_w_up[layer], ffn_conv_w[layer], ffn_conv_b[layer], ffn_w_down[layer])
        x = layer_norm(DEEPNORM_ALPHA * x + ffn, ln_g[layer, 1], ln_b[layer, 1])
    return x
```

```python
import functools
import math

import jax
import jax.numpy as jnp
from jax import lax
from jax.experimental import pallas as pl
from jax.experimental.pallas import tpu as pltpu

F32 = jnp.float32
BF16 = jnp.bfloat16

D_MODEL = 1024
DEPTH = 2
RET_HEADS = 4
RET_HEAD_DIM = 128
RET_CHUNK = 128
DSA_HEADS = 8
DSA_HEAD_DIM = 64
DSA_PATTERNS = ((128, 1), (512, 4), (2048, 16))
DSA_BLOCK = 128
T5_BUCKETS = 32
T5_MAX_DIST = 2048
MLSTM_HEADS = 4
MLSTM_HEAD_DIM = 256
MLSTM_CHUNK = 128
MLSTM_CONV = 4
D_FF = 2816
FFN_CONV = 3
LN_EPS = 1e-5
DEEPNORM_ALPHA = (2.0 * DEPTH) ** 0.25

RET_W = RET_HEADS * RET_HEAD_DIM
DSA_W = DSA_HEADS * DSA_HEAD_DIM
MLSTM_W = MLSTM_HEADS * MLSTM_HEAD_DIM

V7X_LANES = 128
V7X_SUBLANES = 8
V7X_VMEM_BYTES = 64 * 1024 * 1024
VMEM_LIMIT_BYTES = V7X_VMEM_BYTES - 8 * 1024 * 1024

ROW_TILE = 512
FFN_CHUNK = 256
SEQ_TILE = 512
DSA_TILE = DSA_BLOCK * DSA_PATTERNS[-1][1]

_NT = (((1,), (1,)), ((), ()))
_TN = (((0,), (0,)), ((), ()))


def _params(semantics):
    return pltpu.CompilerParams(dimension_semantics=semantics, vmem_limit_bytes=VMEM_LIMIT_BYTES)


def _resident(shape):
    nd = len(shape)
    return pl.BlockSpec(shape, lambda *_: (0,) * nd, pipeline_mode=pl.Buffered(1))


def _layer_norm(z, g, b):
    mu = jnp.mean(z, -1, keepdims=True)
    zc = z - mu
    var = jnp.mean(zc * zc, -1, keepdims=True)
    return zc * lax.rsqrt(var + LN_EPS) * g + b


def _silu(z):
    return z * jax.nn.sigmoid(z)


def _proj_kernel(x_ref, w_ref, o_ref, *, tn):
    xb = x_ref[...].astype(BF16)
    for j in range(w_ref.shape[1] // tn):
        sl = slice(j * tn, (j + 1) * tn)
        o_ref[:, sl] = jnp.dot(xb, w_ref[:, sl], preferred_element_type=F32)


def _proj_gates_kernel(x_ref, w_ref, wg_ref, gb_ref, o_ref, og_ref, *, tn):
    xb = x_ref[...].astype(BF16)
    for j in range(w_ref.shape[1] // tn):
        sl = slice(j * tn, (j + 1) * tn)
        o_ref[:, sl] = jnp.dot(xb, w_ref[:, sl], preferred_element_type=F32)
    og_ref[...] = jnp.dot(xb, wg_ref[...], preferred_element_type=F32) + gb_ref[...]


def _in_proj(x2d, w, tn=512):
    T, K = x2d.shape
    N = w.shape[1]
    tm = min(ROW_TILE, T)
    return pl.pallas_call(
        functools.partial(_proj_kernel, tn=tn),
        grid=(T // tm,),
        in_specs=[pl.BlockSpec((tm, K), lambda i: (i, 0)), _resident((K, N))],
        out_specs=pl.BlockSpec((tm, N), lambda i: (i, 0)),
        out_shape=jax.ShapeDtypeStruct((T, N), F32),
        compiler_params=_params(("parallel",)),
        name="in_proj",
    )(x2d, w)


def _in_proj_gates(x2d, w, wg, gb, tn=512):
    T, K = x2d.shape
    N = w.shape[1]
    tm = min(ROW_TILE, T)
    return pl.pallas_call(
        functools.partial(_proj_gates_kernel, tn=tn),
        grid=(T // tm,),
        in_specs=[
            pl.BlockSpec((tm, K), lambda i: (i, 0)),
            _resident((K, N)),
            _resident((K, V7X_LANES)),
            _resident((1, V7X_LANES)),
        ],
        out_specs=[pl.BlockSpec((tm, N), lambda i: (i, 0)), pl.BlockSpec((tm, V7X_LANES), lambda i: (i, 0))],
        out_shape=[jax.ShapeDtypeStruct((T, N), F32), jax.ShapeDtypeStruct((T, V7X_LANES), F32)],
        compiler_params=_params(("parallel",)),
        name="in_proj_gates",
    )(x2d, w, wg, gb)


def _out_proj_ln_kernel(*refs, n_y):
    x_ref = refs[0]
    y_refs = refs[1 : 1 + n_y]
    w_ref, g_ref, b_ref, o_ref = refs[1 + n_y :]
    acc = None
    row = 0
    for y_ref in y_refs:
        kw = y_ref.shape[1]
        part = jnp.dot(y_ref[...], w_ref[row : row + kw, :], preferred_element_type=F32)
        acc = part if acc is None else acc + part
        row += kw
    z = DEEPNORM_ALPHA * x_ref[...] + acc
    o_ref[...] = _layer_norm(z, g_ref[...], b_ref[...])


def _out_proj_ln(x2d, ys, w, g, b):
    T, D = x2d.shape
    tm = min(ROW_TILE, T)
    row = lambda i: (i, 0)
    return pl.pallas_call(
        functools.partial(_out_proj_ln_kernel, n_y=len(ys)),
        grid=(T // tm,),
        in_specs=[pl.BlockSpec((tm, D), row)]
        + [pl.BlockSpec((tm, y.shape[1]), row) for y in ys]
        + [_resident(w.shape), _resident((1, D)), _resident((1, D))],
        out_specs=pl.BlockSpec((tm, D), row),
        out_shape=jax.ShapeDtypeStruct((T, D), F32),
        compiler_params=_params(("parallel",)),
        name="out_proj_ln",
    )(x2d, *ys, w, g.reshape(1, D), b.reshape(1, D))


def _shift_rows(x, prev, s):
    if s == 0:
        return x
    rolled = pltpu.roll(x, s, 0)
    head = pltpu.roll(prev, s, 0)
    rid = lax.broadcasted_iota(jnp.int32, (V7X_SUBLANES, x.shape[1]), 0)
    first = jnp.where(rid < s, head, rolled[:V7X_SUBLANES])
    return jnp.concatenate([first, rolled[V7X_SUBLANES:]], axis=0)


def _causal_dwconv(x, prev, w_ref, taps):
    acc = None
    for k in range(taps):
        term = _shift_rows(x, prev, taps - 1 - k) * w_ref[k : k + 1, :]
        acc = term if acc is None else acc + term
    return acc


def _ffn_kernel(x_ref, wup_ref, cw_ref, cb_ref, wdn_ref, g_ref, b_ref, o_ref, carry_ref, acc_ref, *, tiles_per_seq):
    i = pl.program_id(0)

    @pl.when(i % tiles_per_seq == 0)
    def _():
        carry_ref[...] = jnp.zeros_like(carry_ref)

    x = x_ref[...]
    xb = x.astype(BF16)
    tm = x.shape[0]
    for j in range(D_FF // FFN_CHUNK):
        sl = slice(j * FFN_CHUNK, (j + 1) * FFN_CHUNK)
        su = slice(D_FF + j * FFN_CHUNK, D_FF + (j + 1) * FFN_CHUNK)
        gate = jnp.dot(xb, wup_ref[:, sl], preferred_element_type=F32)
        up = jnp.dot(xb, wup_ref[:, su], preferred_element_type=F32)
        prev = carry_ref[:, sl]
        carry_ref[:, sl] = gate[tm - V7X_SUBLANES :, :]
        conv = _causal_dwconv(gate, prev, cw_ref.at[:, sl], FFN_CONV) + cb_ref[:, sl]
        act = (_silu(conv) * up).astype(BF16)
        part = jnp.dot(act, wdn_ref[sl, :], preferred_element_type=F32)
        if j == 0:
            acc_ref[...] = part
        else:
            acc_ref[...] += part
    z = DEEPNORM_ALPHA * x + acc_ref[...]
    o_ref[...] = _layer_norm(z, g_ref[...], b_ref[...])


def _ffn_ln(x2d, seq_len, w_up, conv_w, conv_b, w_down, g, b):
    T, D = x2d.shape
    tm = min(ROW_TILE, seq_len)
    cw = jnp.zeros((V7X_SUBLANES, D_FF), F32).at[:FFN_CONV].set(conv_w)
    row = lambda i: (i, 0)
    return pl.pallas_call(
        functools.partial(_ffn_kernel, tiles_per_seq=seq_len // tm),
        grid=(T // tm,),
        in_specs=[
            pl.BlockSpec((tm, D), row),
            _resident(w_up.shape),
            _resident(cw.shape),
            _resident((1, D_FF)),
            _resident(w_down.shape),
            _resident((1, D)),
            _resident((1, D)),
        ],
        out_specs=pl.BlockSpec((tm, D), row),
        out_shape=jax.ShapeDtypeStruct((T, D), F32),
        scratch_shapes=[pltpu.VMEM((V7X_SUBLANES, D_FF), F32), pltpu.VMEM((tm, D), F32)],
        compiler_params=_params(("arbitrary",)),
        name="conv_ffn_ln",
    )(x2d, w_up, cw, conv_b.reshape(1, D_FF), w_down, g.reshape(1, D), b.reshape(1, D))


def _retention_tables(seq_len):
    H, C, d = RET_HEADS, RET_CHUNK, RET_HEAD_DIM
    scale = d ** -0.5
    log_gamma = jnp.log1p(-jnp.exp2(-5.0 - jnp.arange(H, dtype=F32)))
    idx = jnp.arange(C, dtype=F32)
    rel = idx[:, None] - idx[None, :]
    decay = jnp.where(rel >= 0, jnp.exp(log_gamma[:, None, None] * jnp.maximum(rel, 0.0)), 0.0) * scale
    k_w = jnp.exp(log_gamma[:, None] * (C - 1 - idx)[None, :]) * scale
    q_w = jnp.exp(log_gamma[:, None] * (idx + 1.0)[None, :])
    kw_t = jnp.broadcast_to(k_w[:, :, None], (H, C, d))
    qw_t = jnp.broadcast_to(q_w[:, :, None], (H, C, d))
    cd_t = jnp.broadcast_to(jnp.exp(log_gamma * C)[:, None, None], (H, 1, d))
    inv = 1.0 / (10000.0 ** (jnp.arange(0, d, 2, dtype=F32) / d))
    ang = jnp.arange(seq_len, dtype=F32)[:, None] * inv[None, :]
    cos, sin = jnp.cos(ang), jnp.sin(ang)
    cos_t = jnp.concatenate([cos, cos], -1)
    sin_t = jnp.concatenate([-sin, sin], -1)
    return cos_t, sin_t, decay, kw_t, qw_t, cd_t


def _retention_kernel(q_ref, k_ref, v_ref, g_ref, cos_ref, sin_ref, dec_ref, kw_ref, qw_ref, cd_ref, o_ref, r_ref):
    @pl.when(pl.program_id(2) == 0)
    def _():
        r_ref[...] = jnp.zeros_like(r_ref)

    C = RET_CHUNK
    half = RET_HEAD_DIM // 2
    decay = dec_ref[0]
    kw = kw_ref[0]
    qw = qw_ref[0]
    cd = cd_ref[0]
    for n in range(q_ref.shape[1] // C):
        sl = slice(n * C, (n + 1) * C)
        q = q_ref[0, sl, :]
        k = k_ref[0, sl, :]
        cos = cos_ref[sl, :]
        sin = sin_ref[sl, :]
        qf = q * cos + pltpu.roll(q, half, 1) * sin
        kf = k * cos + pltpu.roll(k, half, 1) * sin
        vb = v_ref[0, sl, :].astype(BF16)
        scores = lax.dot_general(qf.astype(BF16), kf.astype(BF16), _NT, preferred_element_type=F32) * decay
        r = r_ref[...]
        y = jnp.dot(scores.astype(BF16), vb, preferred_element_type=F32)
        y += jnp.dot((qf * qw).astype(BF16), r.astype(BF16), preferred_element_type=F32)
        kv = lax.dot_general((kf * kw).astype(BF16), vb, _TN, preferred_element_type=F32)
        r_ref[...] = cd * r + kv
        mu = jnp.mean(y, -1, keepdims=True)
        yc = y - mu
        var = jnp.mean(yc * yc, -1, keepdims=True)
        out = yc * lax.rsqrt(var + LN_EPS) * _silu(g_ref[0, sl, :])
        o_ref[0, sl, :] = out.astype(o_ref.dtype)


def _retention(h3, tables):
    B, S, _ = h3.shape
    H, d, C = RET_HEADS, RET_HEAD_DIM, RET_CHUNK
    ts = min(SEQ_TILE, S)
    cos_t, sin_t, decay, kw_t, qw_t, cd_t = tables
    col = lambda off: pl.BlockSpec((1, ts, d), lambda b, h, c: (b, c, off + h))
    per_head = lambda shape: pl.BlockSpec((1,) + shape, lambda b, h, c: (h, 0, 0))
    pos = pl.BlockSpec((ts, d), lambda b, h, c: (c, 0))
    return pl.pallas_call(
        _retention_kernel,
        grid=(B, H, S // ts),
        in_specs=[col(0), col(H), col(2 * H), col(3 * H), pos, pos,
                  per_head((C, C)), per_head((C, d)), per_head((C, d)), per_head((1, d))],
        out_specs=pl.BlockSpec((1, ts, d), lambda b, h, c: (b, c, h)),
        out_shape=jax.ShapeDtypeStruct((B, S, H * d), BF16),
        scratch_shapes=[pltpu.VMEM((d, d), F32)],
        compiler_params=_params(("parallel", "parallel", "arbitrary")),
        name="retention",
    )(h3, h3, h3, h3, cos_t, sin_t, decay, kw_t, qw_t, cd_t)


def _t5_bucket(dist):
    exact = T5_BUCKETS // 2
    n = jnp.maximum(dist, 0)
    large = exact + (jnp.log(jnp.maximum(n, 1).astype(F32) / exact) / math.log(T5_MAX_DIST / exact) * (T5_BUCKETS - exact)).astype(jnp.int32)
    large = jnp.minimum(large, T5_BUCKETS - 1)
    return jnp.where(n < exact, n, large)


def _dsa_bucket_tables():
    blk = DSA_BLOCK
    qi = jnp.arange(blk)[:, None]
    kj = jnp.arange(2 * blk)[None, :]
    tabs = []
    for window, dilation in DSA_PATTERNS:
        assert window // dilation <= blk
        for offset in (blk, 0):
            dist = qi + offset - kj
            valid = (dist >= 0) & (dist <= window // dilation)
            tabs.append(jnp.where(valid, _t5_bucket(dist * dilation), -1))
    return jnp.stack(tabs).reshape(len(DSA_PATTERNS), 2, blk, 2 * blk).astype(jnp.int32)


def _dsa_bias_kernel(rb_ref, bk_ref, o_ref):
    h = pl.program_id(2)
    bk = bk_ref[0, 0]
    acc = jnp.full(bk.shape, -jnp.inf, F32)
    for b in range(T5_BUCKETS):
        acc = jnp.where(bk == b, rb_ref[b, h], acc)
    o_ref[0, 0, 0] = acc


def _dsa_bias(rel_bias):
    buckets = _dsa_bucket_tables()
    P, blk = len(DSA_PATTERNS), DSA_BLOCK
    return pl.pallas_call(
        _dsa_bias_kernel,
        grid=(P, 2, DSA_HEADS),
        in_specs=[pl.BlockSpec(memory_space=pltpu.SMEM), pl.BlockSpec((1, 1, blk, 2 * blk), lambda p, v, h: (p, v, 0, 0))],
        out_specs=pl.BlockSpec((1, 1, 1, blk, 2 * blk), lambda p, v, h: (p, v, h, 0, 0)),
        out_shape=jax.ShapeDtypeStruct((P, 2, DSA_HEADS, blk, 2 * blk), F32),
        compiler_params=_params(("parallel", "parallel", "parallel")),
        name="dsa_bias",
    )(rel_bias, buckets)


def _dsa_kernel(q_ref, k_ref, v_ref, bias_ref, o_ref, acc_o, acc_m, acc_l):
    blk = DSA_BLOCK
    tq = o_ref.shape[1]
    base = pl.multiple_of(pl.program_id(2) * tq, tq)
    lane = lax.broadcasted_iota(jnp.int32, (1, V7X_LANES), 1)
    head0 = lane < DSA_HEAD_DIM
    scale = DSA_HEAD_DIM ** -0.5

    def attend(qs, ks, vs, bias):
        kb = ks.astype(BF16)
        o_full = m_full = l_full = None
        for hh, mask in enumerate((head0, jnp.logical_not(head0))):
            qh = (jnp.where(mask, qs, 0.0) * scale).astype(BF16)
            vh = jnp.where(mask, vs, 0.0).astype(BF16)
            logits = lax.dot_general(qh, kb, _NT, preferred_element_type=F32) + bias[hh]
            m = jnp.max(logits, -1, keepdims=True)
            p = jnp.exp(logits - m)
            l = jnp.sum(p, -1, keepdims=True)
            o = jnp.dot(p.astype(BF16), vh, preferred_element_type=F32)
            if hh == 0:
                o_full, m_full, l_full = o, m, l
            else:
                o_full = o_full + o
                m_full = jnp.where(head0, m_full, m)
                l_full = jnp.where(head0, l_full, l)
        return o_full, m_full, l_full

    for bi, (_, r) in enumerate(DSA_PATTERNS):
        span = blk * r
        for rho in range(r):

            def body(nb, carry, bi=bi, r=r, span=span, rho=rho):
                off = pl.multiple_of(nb * span, span)
                q_start = pl.multiple_of(base + off, span)
                k_start = pl.multiple_of(jnp.maximum(q_start - span, 0), span)
                variant = (q_start == 0).astype(jnp.int32)
                rows_q = pl.ds(rho, blk, stride=r) if r > 1 else pl.ds(0, blk)
                rows_k = pl.ds(rho, 2 * blk, stride=r) if r > 1 else pl.ds(0, 2 * blk)
                qs = q_ref.at[0, pl.ds(q_start, span)][rows_q, :]
                ks = k_ref.at[0, pl.ds(k_start, 2 * span)][rows_k, :]
                vs = v_ref.at[0, pl.ds(k_start, 2 * span)][rows_k, :]
                bias = bias_ref[bi, pl.ds(variant, 1), 0][0]
                o, m, l = attend(qs, ks, vs, bias)
                acc_o.at[bi, pl.ds(off, span)][rows_q, :] = o
                acc_m.at[bi, pl.ds(off, span)][rows_q, :] = jnp.broadcast_to(m, o.shape)
                acc_l.at[bi, pl.ds(off, span)][rows_q, :] = jnp.broadcast_to(l, o.shape)
                return carry

            n_blocks = tq // span
            if n_blocks == 1:
                body(0, 0)
            else:
                lax.fori_loop(0, n_blocks, body, 0)

    for c in range(tq // blk):
        sl = slice(c * blk, (c + 1) * blk)
        ms = [acc_m[bi, sl, :] for bi in range(len(DSA_PATTERNS))]
        m_all = functools.reduce(jnp.maximum, ms)
        num = den = None
        for bi, m in enumerate(ms):
            w = jnp.exp(m - m_all)
            n_i = w * acc_o[bi, sl, :]
            d_i = w * acc_l[bi, sl, :]
            num = n_i if num is None else num + n_i
            den = d_i if den is None else den + d_i
        o_ref[0, sl, :] = (num / den).astype(o_ref.dtype)


def _dilated_attention(h3, bias, col0):
    B, S, _ = h3.shape
    pairs = DSA_W // V7X_LANES
    tq = DSA_TILE
    assert S % tq == 0 and S >= 2 * tq
    P, blk = len(DSA_PATTERNS), DSA_BLOCK
    bias6 = bias.reshape(P, 2, pairs, 2, blk, 2 * blk)
    seq = lambda off: pl.BlockSpec((1, S, V7X_LANES), lambda b, p, t: (b, 0, col0 + off + p))
    return pl.pallas_call(
        _dsa_kernel,
        grid=(B, pairs, S // tq),
        in_specs=[seq(0), seq(pairs), seq(2 * pairs),
                  pl.BlockSpec((P, 2, 1, 2, blk, 2 * blk), lambda b, p, t: (0, 0, p, 0, 0, 0))],
        out_specs=pl.BlockSpec((1, tq, V7X_LANES), lambda b, p, t: (b, t, p)),
        out_shape=jax.ShapeDtypeStruct((B, S, DSA_W), BF16),
        scratch_shapes=[pltpu.VMEM((P, tq, V7X_LANES), F32)] * 3,
        compiler_params=_params(("parallel", "parallel", "arbitrary")),
        name="dilated_attention",
    )(h3, h3, h3, bias6)


def _log_sigmoid(x):
    return jnp.minimum(x, 0.0) - jnp.log1p(jnp.exp(-jnp.abs(x)))


def _mlstm_kernel(q_ref, k_ref, v_ref, og_ref, cwq_ref, cwk_ref, gc_ref, gr_ref, o_ref,
                  c_ref, n_ref, m_ref, pq_ref, pk_ref):
    h = pl.program_id(1)

    @pl.when(pl.program_id(2) == 0)
    def _():
        c_ref[...] = jnp.zeros_like(c_ref)
        n_ref[...] = jnp.zeros_like(n_ref)
        m_ref[...] = jnp.zeros_like(m_ref)
        pq_ref[...] = jnp.zeros_like(pq_ref)
        pk_ref[...] = jnp.zeros_like(pk_ref)

    C = MLSTM_CHUNK
    d = MLSTM_HEAD_DIM
    lane = lax.broadcasted_iota(jnp.int32, (C, V7X_LANES), 1)
    ri = lax.broadcasted_iota(jnp.int32, (C, C), 0)
    ci = lax.broadcasted_iota(jnp.int32, (C, C), 1)
    causal = ci <= ri
    for n in range(q_ref.shape[1] // C):
        sl = slice(n * C, (n + 1) * C)
        q_raw = q_ref[0, sl, :]
        k_raw = k_ref[0, sl, :]
        qc = _silu(_causal_dwconv(q_raw, pq_ref[...], cwq_ref, MLSTM_CONV))
        kc = _silu(_causal_dwconv(k_raw, pk_ref[...], cwk_ref, MLSTM_CONV)) * (d ** -0.5)
        pq_ref[...] = q_raw[C - V7X_SUBLANES :, :]
        pk_ref[...] = k_raw[C - V7X_SUBLANES :, :]
        vb = v_ref[0, sl, :].astype(BF16)
        qb = qc.astype(BF16)

        gates = gc_ref[0, sl, :]
        li_col = jnp.sum(jnp.where(lane == h, gates, 0.0), -1, keepdims=True)
        lf_col = _log_sigmoid(jnp.sum(jnp.where(lane == MLSTM_HEADS + h, gates, 0.0), -1, keepdims=True))
        li_row = gr_ref[0, 0, 0:1, sl]
        lf_row = _log_sigmoid(gr_ref[0, 0, 1:2, sl])
        a_col = jnp.sum(jnp.where(causal, lf_row, 0.0), -1, keepdims=True)
        a_row = jnp.sum(jnp.where(causal, 0.0, lf_col) + jnp.where(ri == ci, lf_col, 0.0), 0, keepdims=True)
        g_tot = jnp.sum(lf_col, 0, keepdims=True)

        w_state = g_tot - a_col + li_col
        m_loc = jnp.max(w_state, 0, keepdims=True)
        k_ew = kc * jnp.exp(w_state - m_loc)
        kv = lax.dot_general(k_ew.astype(BF16), vb, _TN, preferred_element_type=F32)
        ksum = jnp.sum(k_ew, 0, keepdims=True)

        c_s = c_ref[...]
        n_s = n_ref[...]
        m_s = m_ref[...][:, 0:1]

        dmat = jnp.where(causal, a_col - a_row + li_row, -jnp.inf)
        inter_log = a_col + m_s
        m_row = jnp.maximum(inter_log, jnp.max(dmat, -1, keepdims=True))
        qk = lax.dot_general(qb, kc.astype(BF16), _NT, preferred_element_type=F32)
        p = jnp.exp(dmat - m_row) * qk
        w_inter = jnp.exp(inter_log - m_row)
        num = jnp.dot(p.astype(BF16), vb, preferred_element_type=F32)
        num += w_inter * jnp.dot(qb, c_s.astype(BF16), preferred_element_type=F32)
        den = jnp.sum(p, -1, keepdims=True) + w_inter * jnp.sum(qc * n_s, -1, keepdims=True)
        hid = num / jnp.maximum(jnp.abs(den), jnp.exp(-m_row))
        o_ref[0, sl, :] = (hid * jax.nn.sigmoid(og_ref[0, sl, :])).astype(o_ref.dtype)

        m_new = jnp.maximum(g_tot + m_s, m_loc)
        a_old = jnp.exp(g_tot + m_s - m_new)
        a_new = jnp.exp(m_loc - m_new)
        c_ref[...] = a_old * c_s + a_new * kv
        n_ref[...] = a_old * n_s + a_new * ksum
        m_ref[...] = jnp.broadcast_to(m_new, m_ref.shape)


def _mlstm(h3, gates_col, gates_row, conv_w):
    B, S, _ = h3.shape
    H, d = MLSTM_HEADS, MLSTM_HEAD_DIM
    ts = min(SEQ_TILE, S)
    cw = jnp.zeros((V7X_SUBLANES, 2 * MLSTM_W), F32).at[:MLSTM_CONV].set(conv_w)
    col = lambda off: pl.BlockSpec((1, ts, d), lambda b, h, c: (b, c, off + h))
    cwspec = lambda off: pl.BlockSpec((V7X_SUBLANES, d), lambda b, h, c: (0, off + h))
    return pl.pallas_call(
        _mlstm_kernel,
        grid=(B, H, S // ts),
        in_specs=[col(0), col(H), col(2 * H), col(3 * H), cwspec(0), cwspec(H),
                  pl.BlockSpec((1, ts, V7X_LANES), lambda b, h, c: (b, c, 0)),
                  pl.BlockSpec((1, 1, 2, ts), lambda b, h, c: (b, h, 0, c))],
        out_specs=pl.BlockSpec((1, ts, d), lambda b, h, c: (b, c, h)),
        out_shape=jax.ShapeDtypeStruct((B, S, H * d), BF16),
        scratch_shapes=[pltpu.VMEM((d, d), F32), pltpu.VMEM((1, d), F32), pltpu.VMEM((1, V7X_LANES), F32),
                        pltpu.VMEM((V7X_SUBLANES, d), F32), pltpu.VMEM((V7X_SUBLANES, d), F32)],
        compiler_params=_params(("parallel", "parallel", "arbitrary")),
        name="mlstm",
    )(h3, h3, h3, h3, cw, cw, gates_col, gates_row)


def _even_layer(x2d, B, S, w_in, w_out, rel_bias, g, b):
    h3 = _in_proj(x2d, w_in.astype(BF16)).reshape(B, S, -1)
    y_r = _retention(h3, _retention_tables(S))
    y_d = _dilated_attention(h3, _dsa_bias(rel_bias), 4 * RET_W // V7X_LANES)
    T = B * S
    return _out_proj_ln(x2d, [y_r.reshape(T, RET_W), y_d.reshape(T, DSA_W)], w_out.astype(BF16), g, b)


def _odd_layer(x2d, B, S, w_in, gate_b, conv_w, w_out, g, b):
    H = MLSTM_HEADS
    wide = 4 * MLSTM_W
    wg = jnp.zeros((D_MODEL, V7X_LANES), F32).at[:, : 2 * H].set(w_in[:, wide:]).astype(BF16)
    gb = jnp.zeros((1, V7X_LANES), F32).at[0, : 2 * H].set(gate_b)
    h2, gates = _in_proj_gates(x2d, w_in[:, :wide].astype(BF16), wg, gb)
    gates_col = gates.reshape(B, S, V7X_LANES)
    gates_row = gates_col[:, :, : 2 * H].reshape(B, S, 2, H).transpose(0, 3, 2, 1)
    y = _mlstm(h2.reshape(B, S, wide), gates_col, gates_row, conv_w)
    return _out_proj_ln(x2d, [y.reshape(B * S, MLSTM_W)], w_out.astype(BF16), g, b)


def kernel(x, even_w_in, even_w_out, rel_bias, odd_w_in, odd_gate_b, odd_conv_w, odd_w_out, ffn_w_up, ffn_conv_w, ffn_conv_b, ffn_w_down, ln_g, ln_b):
    B, S, D = x.shape
    x2d = x.reshape(B * S, D)
    for layer in range(DEPTH):
        j = layer // 2
        if layer % 2 == 0:
            x2d = _even_layer(x2d, B, S, even_w_in[j], even_w_out[j], rel_bias, ln_g[layer, 0], ln_b[layer, 0])
        else:
            x2d = _odd_layer(x2d, B, S, odd_w_in[j], odd_gate_b[j], odd_conv_w[j], odd_w_out[j], ln_g[layer, 0], ln_b[layer, 0])
        x2d = _ffn_ln(x2d, S, ffn_w_up[layer].astype(BF16), ffn_conv_w[layer], ffn_conv_b[layer],
                      ffn_w_down[layer].astype(BF16), ln_g[layer, 1], ln_b[layer, 1])
    return x2d.reshape(B, S, D)
```

```python
import functools
import math

import jax
import jax.numpy as jnp
from jax import lax
from jax.experimental import pallas as pl
from jax.experimental.pallas import tpu as pltpu

F32 = jnp.float32
BF16 = jnp.bfloat16

D_MODEL = 1024
DEPTH = 2
RET_HEADS = 4
RET_HEAD_DIM = 128
RET_CHUNK = 128
DSA_HEADS = 8
DSA_HEAD_DIM = 64
DSA_PATTERNS = ((128, 1), (512, 4), (2048, 16))
DSA_BLOCK = 128
T5_BUCKETS = 32
T5_MAX_DIST = 2048
MLSTM_HEADS = 4
MLSTM_HEAD_DIM = 256
MLSTM_CHUNK = 128
MLSTM_CONV = 4
D_FF = 2816
FFN_CONV = 3
LN_EPS = 1e-5
DEEPNORM_ALPHA = (2.0 * DEPTH) ** 0.25

RET_W = RET_HEADS * RET_HEAD_DIM
DSA_W = DSA_HEADS * DSA_HEAD_DIM
MLSTM_W = MLSTM_HEADS * MLSTM_HEAD_DIM

V7X_LANES = 128
V7X_SUBLANES = 8
V7X_VMEM_BYTES = 64 * 1024 * 1024
VMEM_LIMIT_BYTES = V7X_VMEM_BYTES - 8 * 1024 * 1024

ROW_TILE = 512
FFN_CHUNK = 256
SEQ_TILE = 512
DSA_TILE = DSA_BLOCK * DSA_PATTERNS[-1][1]
DSA_GROUP = 2

_NT = (((1,), (1,)), ((), ()))
_TN = (((0,), (0,)), ((), ()))


def _params(semantics):
    return pltpu.CompilerParams(dimension_semantics=semantics, vmem_limit_bytes=VMEM_LIMIT_BYTES)


def _resident(shape):
    nd = len(shape)
    return pl.BlockSpec(shape, lambda *_: (0,) * nd, pipeline_mode=pl.Buffered(1))


def _layer_norm(z, g, b):
    mu = jnp.mean(z, -1, keepdims=True)
    zc = z - mu
    var = jnp.mean(zc * zc, -1, keepdims=True)
    return zc * lax.rsqrt(var + LN_EPS) * g + b


def _silu(z):
    return z * jax.nn.sigmoid(z)


def _proj_kernel(x_ref, w_ref, o_ref, *, tn):
    xb = x_ref[...].astype(BF16)
    for j in range(w_ref.shape[1] // tn):
        sl = slice(j * tn, (j + 1) * tn)
        o_ref[:, sl] = jnp.dot(xb, w_ref[:, sl], preferred_element_type=F32)


def _proj_gates_kernel(x_ref, w_ref, wg_ref, gb_ref, o_ref, og_ref, *, tn):
    xb = x_ref[...].astype(BF16)
    for j in range(w_ref.shape[1] // tn):
        sl = slice(j * tn, (j + 1) * tn)
        o_ref[:, sl] = jnp.dot(xb, w_ref[:, sl], preferred_element_type=F32)
    og_ref[...] = jnp.dot(xb, wg_ref[...], preferred_element_type=F32) + gb_ref[...]


def _in_proj(x2d, w, tn=512):
    T, K = x2d.shape
    N = w.shape[1]
    tm = min(ROW_TILE, T)
    return pl.pallas_call(
        functools.partial(_proj_kernel, tn=tn),
        grid=(T // tm,),
        in_specs=[pl.BlockSpec((tm, K), lambda i: (i, 0)), _resident((K, N))],
        out_specs=pl.BlockSpec((tm, N), lambda i: (i, 0)),
        out_shape=jax.ShapeDtypeStruct((T, N), F32),
        compiler_params=_params(("parallel",)),
        name="in_proj",
    )(x2d, w)


def _in_proj_gates(x2d, w, wg, gb, tn=512):
    T, K = x2d.shape
    N = w.shape[1]
    tm = min(ROW_TILE, T)
    return pl.pallas_call(
        functools.partial(_proj_gates_kernel, tn=tn),
        grid=(T // tm,),
        in_specs=[
            pl.BlockSpec((tm, K), lambda i: (i, 0)),
            _resident((K, N)),
            _resident((K, V7X_LANES)),
            _resident((1, V7X_LANES)),
        ],
        out_specs=[pl.BlockSpec((tm, N), lambda i: (i, 0)), pl.BlockSpec((tm, V7X_LANES), lambda i: (i, 0))],
        out_shape=[jax.ShapeDtypeStruct((T, N), F32), jax.ShapeDtypeStruct((T, V7X_LANES), F32)],
        compiler_params=_params(("parallel",)),
        name="in_proj_gates",
    )(x2d, w, wg, gb)


def _out_proj_ln_kernel(*refs, n_y):
    x_ref = refs[0]
    y_refs = refs[1 : 1 + n_y]
    w_ref, g_ref, b_ref, o_ref = refs[1 + n_y :]
    tm = x_ref.shape[0]
    for lo in range(0, tm, tm // 2):
        rows = slice(lo, lo + tm // 2)
        acc = None
        row = 0
        for y_ref in y_refs:
            kw = y_ref.shape[1]
            part = jnp.dot(y_ref[rows, :], w_ref[row : row + kw, :], preferred_element_type=F32)
            acc = part if acc is None else acc + part
            row += kw
        z = DEEPNORM_ALPHA * x_ref[rows, :] + acc
        o_ref[rows, :] = _layer_norm(z, g_ref[...], b_ref[...])


def _out_proj_ln(x2d, ys, w, g, b):
    T, D = x2d.shape
    tm = min(ROW_TILE, T)
    row = lambda i: (i, 0)
    return pl.pallas_call(
        functools.partial(_out_proj_ln_kernel, n_y=len(ys)),
        grid=(T // tm,),
        in_specs=[pl.BlockSpec((tm, D), row)]
        + [pl.BlockSpec((tm, y.shape[1]), row) for y in ys]
        + [_resident(w.shape), _resident((1, D)), _resident((1, D))],
        out_specs=pl.BlockSpec((tm, D), row),
        out_shape=jax.ShapeDtypeStruct((T, D), F32),
        compiler_params=_params(("parallel",)),
        name="out_proj_ln",
    )(x2d, *ys, w, g.reshape(1, D), b.reshape(1, D))


def _shift_rows(x, prev, s):
    if s == 0:
        return x
    rolled = pltpu.roll(x, s, 0)
    head = pltpu.roll(prev, s, 0)
    rid = lax.broadcasted_iota(jnp.int32, (V7X_SUBLANES, x.shape[1]), 0)
    first = jnp.where(rid < s, head, rolled[:V7X_SUBLANES])
    return jnp.concatenate([first, rolled[V7X_SUBLANES:]], axis=0)


def _causal_dwconv(x, prev, w_ref, taps):
    acc = None
    for k in range(taps):
        term = _shift_rows(x, prev, taps - 1 - k) * w_ref[k : k + 1, :]
        acc = term if acc is None else acc + term
    return acc


def _ffn_kernel(x_ref, wup_ref, cw_ref, cb_ref, wdn_ref, g_ref, b_ref, o_ref, carry_ref, act_ref, *, tiles_per_seq):
    i = pl.program_id(0)

    @pl.when(i % tiles_per_seq == 0)
    def _():
        carry_ref[...] = jnp.zeros_like(carry_ref)

    x = x_ref[...]
    xb = x.astype(BF16)
    tm = x.shape[0]
    n_chunks = D_FF // FFN_CHUNK

    def up_dots(j):
        lo = j * FFN_CHUNK
        return (jnp.dot(xb, wup_ref[:, lo : lo + FFN_CHUNK], preferred_element_type=F32),
                jnp.dot(xb, wup_ref[:, D_FF + lo : D_FF + lo + FFN_CHUNK], preferred_element_type=F32))

    nxt = up_dots(0)
    for j in range(n_chunks):
        sl = slice(j * FFN_CHUNK, (j + 1) * FFN_CHUNK)
        gate, up = nxt
        if j + 1 < n_chunks:
            nxt = up_dots(j + 1)
        prev = carry_ref[:, sl]
        carry_ref[:, sl] = gate[tm - V7X_SUBLANES :, :]
        conv = _causal_dwconv(gate, prev, cw_ref.at[:, sl], FFN_CONV) + cb_ref[:, sl]
        act_ref[:, sl] = (_silu(conv) * up).astype(BF16)
    for lo in range(0, tm, tm // 2):
        rows = slice(lo, lo + tm // 2)
        ffn = jnp.dot(act_ref[rows, :], wdn_ref[...], preferred_element_type=F32)
        z = DEEPNORM_ALPHA * x_ref[rows, :] + ffn
        o_ref[rows, :] = _layer_norm(z, g_ref[...], b_ref[...])


def _ffn_ln(x2d, seq_len, w_up, conv_w, conv_b, w_down, g, b):
    T, D = x2d.shape
    tm = min(ROW_TILE, seq_len)
    cw = jnp.zeros((V7X_SUBLANES, D_FF), F32).at[:FFN_CONV].set(conv_w)
    row = lambda i: (i, 0)
    return pl.pallas_call(
        functools.partial(_ffn_kernel, tiles_per_seq=seq_len // tm),
        grid=(T // tm,),
        in_specs=[
            pl.BlockSpec((tm, D), row),
            _resident(w_up.shape),
            _resident(cw.shape),
            _resident((1, D_FF)),
            _resident(w_down.shape),
            _resident((1, D)),
            _resident((1, D)),
        ],
        out_specs=pl.BlockSpec((tm, D), row),
        out_shape=jax.ShapeDtypeStruct((T, D), F32),
        scratch_shapes=[pltpu.VMEM((V7X_SUBLANES, D_FF), F32), pltpu.VMEM((tm, D_FF), BF16)],
        compiler_params=_params(("arbitrary",)),
        name="conv_ffn_ln",
    )(x2d, w_up, cw, conv_b.reshape(1, D_FF), w_down, g.reshape(1, D), b.reshape(1, D))


def _retention_tables(seq_len):
    H, C, d = RET_HEADS, RET_CHUNK, RET_HEAD_DIM
    scale = d ** -0.5
    log_gamma = jnp.log1p(-jnp.exp2(-5.0 - jnp.arange(H, dtype=F32)))
    idx = jnp.arange(C, dtype=F32)
    rel = idx[:, None] - idx[None, :]
    decay = jnp.where(rel >= 0, jnp.exp(log_gamma[:, None, None] * jnp.maximum(rel, 0.0)), 0.0) * scale
    k_w = jnp.exp(log_gamma[:, None] * (C - 1 - idx)[None, :]) * scale
    q_w = jnp.exp(log_gamma[:, None] * (idx + 1.0)[None, :])
    kw_t = jnp.broadcast_to(k_w[:, :, None], (H, C, d))
    qw_t = jnp.broadcast_to(q_w[:, :, None], (H, C, d))
    cd_t = jnp.broadcast_to(jnp.exp(log_gamma * C)[:, None, None], (H, 1, d))
    inv = 1.0 / (10000.0 ** (jnp.arange(0, d, 2, dtype=F32) / d))
    ang = jnp.arange(seq_len, dtype=F32)[:, None] * inv[None, :]
    cos, sin = jnp.cos(ang), jnp.sin(ang)
    cos_t = jnp.concatenate([cos, cos], -1)
    sin_t = jnp.concatenate([-sin, sin], -1)
    return cos_t, sin_t, decay, kw_t, qw_t, cd_t


def _retention_kernel(q_ref, k_ref, v_ref, g_ref, cos_ref, sin_ref, dec_ref, kw_ref, qw_ref, cd_ref, o_ref, r_ref):
    @pl.when(pl.program_id(2) == 0)
    def _():
        r_ref[...] = jnp.zeros_like(r_ref)

    C = RET_CHUNK
    half = RET_HEAD_DIM // 2
    decay = dec_ref[0]
    kw = kw_ref[0]
    qw = qw_ref[0]
    cd = cd_ref[0]
    for n in range(q_ref.shape[1] // C):
        sl = slice(n * C, (n + 1) * C)
        q = q_ref[0, sl, :]
        k = k_ref[0, sl, :]
        cos = cos_ref[sl, :]
        sin = sin_ref[sl, :]
        qf = q * cos + pltpu.roll(q, half, 1) * sin
        kf = k * cos + pltpu.roll(k, half, 1) * sin
        vb = v_ref[0, sl, :].astype(BF16)
        scores = lax.dot_general(qf.astype(BF16), kf.astype(BF16), _NT, preferred_element_type=F32) * decay
        r = r_ref[...]
        y = jnp.dot(scores.astype(BF16), vb, preferred_element_type=F32)
        y += jnp.dot((qf * qw).astype(BF16), r.astype(BF16), preferred_element_type=F32)
        kv = lax.dot_general((kf * kw).astype(BF16), vb, _TN, preferred_element_type=F32)
        r_ref[...] = cd * r + kv
        mu = jnp.mean(y, -1, keepdims=True)
        yc = y - mu
        var = jnp.mean(yc * yc, -1, keepdims=True)
        out = yc * lax.rsqrt(var + LN_EPS) * _silu(g_ref[0, sl, :])
        o_ref[0, sl, :] = out.astype(o_ref.dtype)


def _retention(h3, tables):
    B, S, _ = h3.shape
    H, d, C = RET_HEADS, RET_HEAD_DIM, RET_CHUNK
    ts = min(SEQ_TILE, S)
    cos_t, sin_t, decay, kw_t, qw_t, cd_t = tables
    col = lambda off: pl.BlockSpec((1, ts, d), lambda b, h, c: (b, c, off + h))
    per_head = lambda shape: pl.BlockSpec((1,) + shape, lambda b, h, c: (h, 0, 0))
    pos = pl.BlockSpec((ts, d), lambda b, h, c: (c, 0))
    return pl.pallas_call(
        _retention_kernel,
        grid=(B, H, S // ts),
        in_specs=[col(0), col(H), col(2 * H), col(3 * H), pos, pos,
                  per_head((C, C)), per_head((C, d)), per_head((C, d)), per_head((1, d))],
        out_specs=pl.BlockSpec((1, ts, d), lambda b, h, c: (b, c, h)),
        out_shape=jax.ShapeDtypeStruct((B, S, H * d), BF16),
        scratch_shapes=[pltpu.VMEM((d, d), F32)],
        compiler_params=_params(("parallel", "parallel", "arbitrary")),
        name="retention",
    )(h3, h3, h3, h3, cos_t, sin_t, decay, kw_t, qw_t, cd_t)


def _t5_bucket(dist):
    exact = T5_BUCKETS // 2
    n = jnp.maximum(dist, 0)
    large = exact + (jnp.log(jnp.maximum(n, 1).astype(F32) / exact) / math.log(T5_MAX_DIST / exact) * (T5_BUCKETS - exact)).astype(jnp.int32)
    large = jnp.minimum(large, T5_BUCKETS - 1)
    return jnp.where(n < exact, n, large)


def _dsa_bucket_tables():
    blk = DSA_BLOCK
    qi = jnp.arange(blk)[:, None]
    kj = jnp.arange(2 * blk)[None, :]
    tabs = []
    for window, dilation in DSA_PATTERNS:
        assert window // dilation <= blk
        for offset in (blk, 0):
            dist = qi + offset - kj
            valid = (dist >= 0) & (dist <= window // dilation)
            tabs.append(jnp.where(valid, _t5_bucket(dist * dilation), -1))
    return jnp.stack(tabs).reshape(len(DSA_PATTERNS), 2, blk, 2 * blk).astype(jnp.int32)


def _dsa_bias_kernel(rb_ref, bk_ref, o_ref):
    h = pl.program_id(2)
    bk = bk_ref[0, 0]
    acc = jnp.full(bk.shape, -jnp.inf, F32)
    for b in range(T5_BUCKETS):
        acc = jnp.where(bk == b, rb_ref[b, h], acc)
    o_ref[0, 0, 0] = acc


def _dsa_bias(rel_bias):
    buckets = _dsa_bucket_tables()
    P, blk = len(DSA_PATTERNS), DSA_BLOCK
    return pl.pallas_call(
        _dsa_bias_kernel,
        grid=(P, 2, DSA_HEADS),
        in_specs=[pl.BlockSpec(memory_space=pltpu.SMEM), pl.BlockSpec((1, 1, blk, 2 * blk), lambda p, v, h: (p, v, 0, 0))],
        out_specs=pl.BlockSpec((1, 1, 1, blk, 2 * blk), lambda p, v, h: (p, v, h, 0, 0)),
        out_shape=jax.ShapeDtypeStruct((P, 2, DSA_HEADS, blk, 2 * blk), F32),
        compiler_params=_params(("parallel", "parallel", "parallel")),
        name="dsa_bias",
    )(rel_bias, buckets)


def _dsa_kernel(q_ref, k_ref, v_ref, bias_ref, o_ref, qd, kd, vd, acc_o, acc_m, acc_l):
    blk = DSA_BLOCK
    S = q_ref.shape[1]
    tq = o_ref.shape[1]
    t = pl.program_id(2)
    lane = lax.broadcasted_iota(jnp.int32, (1, V7X_LANES), 1)
    head0 = lane < DSA_HEAD_DIM
    scale = DSA_HEAD_DIM ** -0.5

    @pl.when(t == 0)
    def _():
        for p, (_, r) in enumerate(DSA_PATTERNS):
            chunks_per_residue = S // (blk * r)

            def prep(idx, carry, p=p, r=r, chunks_per_residue=chunks_per_residue):
                rho = idx // chunks_per_residue
                start = rho + (idx % chunks_per_residue) * (blk * r)
                rows = pl.ds(start, blk, stride=r) if r > 1 else pl.ds(pl.multiple_of(start, blk), blk)
                dst = pl.ds(pl.multiple_of(idx * blk, blk), blk)
                qd[p, dst, :] = (q_ref.at[0][rows, :] * scale).astype(BF16)
                kd[p, dst, :] = k_ref.at[0][rows, :].astype(BF16)
                vd[p, dst, :] = v_ref.at[0][rows, :].astype(BF16)
                return carry

            lax.fori_loop(0, S // blk, prep, 0, unroll=2)

    def block_logits(p, r, idx):
        blocks_per_residue = tq // (blk * r)
        rho = idx // blocks_per_residue
        nb = idx % blocks_per_residue
        l0 = t * (tq // r) + nb * blk
        first = l0 == 0
        q_row = pl.multiple_of(rho * (S // r) + l0, blk)
        k_row = pl.multiple_of(jnp.where(first, q_row, q_row - blk), blk)
        qb = qd[p, pl.ds(q_row, blk), :]
        zero = jnp.zeros_like(qb)
        q2 = jnp.concatenate([jnp.where(head0, qb, zero), jnp.where(head0, zero, qb)], axis=0)
        bias = bias_ref[p, pl.ds(first.astype(jnp.int32), 1), 0][0]
        logits = lax.dot_general(q2, kd[p, pl.ds(k_row, 2 * blk), :], _NT, preferred_element_type=F32) + bias
        start = rho + nb * (blk * r)
        rows = pl.ds(start, blk, stride=r) if r > 1 else pl.ds(pl.multiple_of(start, blk), blk)
        return logits, k_row, rows

    def block_output(p, logits, k_row, rows):
        m = jnp.max(logits, -1, keepdims=True)
        e = jnp.exp(logits - m).astype(BF16)
        vb = vd[p, pl.ds(k_row, 2 * blk), :]
        o2 = jnp.dot(e, jnp.concatenate([vb, jnp.ones_like(vb)], axis=1), preferred_element_type=F32)
        acc_o.at[p][rows, :] = jnp.where(head0, o2[:blk, :V7X_LANES], o2[blk:, :V7X_LANES])
        acc_m.at[p][rows, :] = jnp.where(head0, m[:blk], m[blk:])
        acc_l.at[p][rows, :] = jnp.where(head0, o2[:blk, V7X_LANES:], o2[blk:, V7X_LANES:])

    blocks = [(p, r, idx) for p, (_, r) in enumerate(DSA_PATTERNS) for idx in range(tq // blk)]
    staged = [block_logits(*blocks[i]) for i in range(DSA_GROUP)]
    for i, (p, _, _) in enumerate(blocks):
        if i + DSA_GROUP < len(blocks):
            staged.append(block_logits(*blocks[i + DSA_GROUP]))
        block_output(p, *staged[i])

    for c in range(tq // blk):
        sl = slice(c * blk, (c + 1) * blk)
        ms = [acc_m[bi, sl, :] for bi in range(len(DSA_PATTERNS))]
        m_all = functools.reduce(jnp.maximum, ms)
        num = den = None
        for bi, m in enumerate(ms):
            w = jnp.exp(m - m_all)
            n_i = w * acc_o[bi, sl, :]
            d_i = w * acc_l[bi, sl, :]
            num = n_i if num is None else num + n_i
            den = d_i if den is None else den + d_i
        o_ref[0, sl, :] = (num / den).astype(o_ref.dtype)


def _dilated_attention(h3, bias, col0):
    B, S, _ = h3.shape
    pairs = DSA_W // V7X_LANES
    tq = DSA_TILE
    assert S % tq == 0 and S >= 2 * tq
    P, blk = len(DSA_PATTERNS), DSA_BLOCK
    bias5 = bias.reshape(P, 2, pairs, 2 * blk, 2 * blk)
    seq = lambda off: pl.BlockSpec((1, S, V7X_LANES), lambda b, p, t: (b, 0, col0 + off + p))
    return pl.pallas_call(
        _dsa_kernel,
        grid=(B, pairs, S // tq),
        in_specs=[seq(0), seq(pairs), seq(2 * pairs),
                  pl.BlockSpec((P, 2, 1, 2 * blk, 2 * blk), lambda b, p, t: (0, 0, p, 0, 0))],
        out_specs=pl.BlockSpec((1, tq, V7X_LANES), lambda b, p, t: (b, t, p)),
        out_shape=jax.ShapeDtypeStruct((B, S, DSA_W), BF16),
        scratch_shapes=[pltpu.VMEM((P, S, V7X_LANES), BF16)] * 3 + [pltpu.VMEM((P, tq, V7X_LANES), F32)] * 3,
        compiler_params=_params(("parallel", "parallel", "arbitrary")),
        name="dilated_attention",
    )(h3, h3, h3, bias5)


def _log_sigmoid(x):
    return jnp.minimum(x, 0.0) - jnp.log1p(jnp.exp(-jnp.abs(x)))


def _mlstm_kernel(q_ref, k_ref, v_ref, og_ref, cwq_ref, cwk_ref, gc_ref, gr_ref, o_ref,
                  c_ref, n_ref, m_ref, pq_ref, pk_ref):
    h = pl.program_id(1)

    @pl.when(pl.program_id(2) == 0)
    def _():
        c_ref[...] = jnp.zeros_like(c_ref)
        n_ref[...] = jnp.zeros_like(n_ref)
        m_ref[...] = jnp.zeros_like(m_ref)
        pq_ref[...] = jnp.zeros_like(pq_ref)
        pk_ref[...] = jnp.zeros_like(pk_ref)

    C = MLSTM_CHUNK
    d = MLSTM_HEAD_DIM
    lane = lax.broadcasted_iota(jnp.int32, (C, V7X_LANES), 1)
    ri = lax.broadcasted_iota(jnp.int32, (C, C), 0)
    ci = lax.broadcasted_iota(jnp.int32, (C, C), 1)
    causal = ci <= ri
    for n in range(q_ref.shape[1] // C):
        sl = slice(n * C, (n + 1) * C)
        q_raw = q_ref[0, sl, :]
        k_raw = k_ref[0, sl, :]
        qc = _silu(_causal_dwconv(q_raw, pq_ref[...], cwq_ref, MLSTM_CONV))
        kc = _silu(_causal_dwconv(k_raw, pk_ref[...], cwk_ref, MLSTM_CONV)) * (d ** -0.5)
        pq_ref[...] = q_raw[C - V7X_SUBLANES :, :]
        pk_ref[...] = k_raw[C - V7X_SUBLANES :, :]
        vb = v_ref[0, sl, :].astype(BF16)
        qb = qc.astype(BF16)

        gates = gc_ref[0, sl, :]
        li_col = jnp.sum(jnp.where(lane == h, gates, 0.0), -1, keepdims=True)
        lf_col = _log_sigmoid(jnp.sum(jnp.where(lane == MLSTM_HEADS + h, gates, 0.0), -1, keepdims=True))
        li_row = gr_ref[0, 0, 0:1, sl]
        lf_row = _log_sigmoid(gr_ref[0, 0, 1:2, sl])
        a_col = jnp.sum(jnp.where(causal, lf_row, 0.0), -1, keepdims=True)
        a_row = jnp.sum(jnp.where(causal, 0.0, lf_col) + jnp.where(ri == ci, lf_col, 0.0), 0, keepdims=True)
        g_tot = jnp.sum(lf_col, 0, keepdims=True)

        w_state = g_tot - a_col + li_col
        m_loc = jnp.max(w_state, 0, keepdims=True)
        k_ew = kc * jnp.exp(w_state - m_loc)
        kv = lax.dot_general(k_ew.astype(BF16), vb, _TN, preferred_element_type=F32)
        ksum = jnp.sum(k_ew, 0, keepdims=True)

        c_s = c_ref[...]
        n_s = n_ref[...]
        m_s = m_ref[...][:, 0:1]

        dmat = jnp.where(causal, a_col - a_row + li_row, -jnp.inf)
        inter_log = a_col + m_s
        m_row = jnp.maximum(inter_log, jnp.max(dmat, -1, keepdims=True))
        qk = lax.dot_general(qb, kc.astype(BF16), _NT, preferred_element_type=F32)
        p = jnp.exp(dmat - m_row) * qk
        w_inter = jnp.exp(inter_log - m_row)
        num = jnp.dot(p.astype(BF16), vb, preferred_element_type=F32)
        num += w_inter * jnp.dot(qb, c_s.astype(BF16), preferred_element_type=F32)
        den = jnp.sum(p, -1, keepdims=True) + w_inter * jnp.sum(qc * n_s, -1, keepdims=True)
        hid = num / jnp.maximum(jnp.abs(den), jnp.exp(-m_row))
        o_ref[0, sl, :] = (hid * jax.nn.sigmoid(og_ref[0, sl, :])).astype(o_ref.dtype)

        m_new = jnp.maximum(g_tot + m_s, m_loc)
        a_old = jnp.exp(g_tot + m_s - m_new)
        a_new = jnp.exp(m_loc - m_new)
        c_ref[...] = a_old * c_s + a_new * kv
        n_ref[...] = a_old * n_s + a_new * ksum
        m_ref[...] = jnp.broadcast_to(m_new, m_ref.shape)


def _mlstm(h3, gates_col, gates_row, conv_w):
    B, S, _ = h3.shape
    H, d = MLSTM_HEADS, MLSTM_HEAD_DIM
    ts = min(SEQ_TILE, S)
    cw = jnp.zeros((V7X_SUBLANES, 2 * MLSTM_W), F32).at[:MLSTM_CONV].set(conv_w)
    col = lambda off: pl.BlockSpec((1, ts, d), lambda b, h, c: (b, c, off + h))
    cwspec = lambda off: pl.BlockSpec((V7X_SUBLANES, d), lambda b, h, c: (0, off + h))
    return pl.pallas_call(
        _mlstm_kernel,
        grid=(B, H, S // ts),
        in_specs=[col(0), col(H), col(2 * H), col(3 * H), cwspec(0), cwspec(H),
                  pl.BlockSpec((1, ts, V7X_LANES), lambda b, h, c: (b, c, 0)),
                  pl.BlockSpec((1, 1, 2, ts), lambda b, h, c: (b, h, 0, c))],
        out_specs=pl.BlockSpec((1, ts, d), lambda b, h, c: (b, c, h)),
        out_shape=jax.ShapeDtypeStruct((B, S, H * d), BF16),
        scratch_shapes=[pltpu.VMEM((d, d), F32), pltpu.VMEM((1, d), F32), pltpu.VMEM((1, V7X_LANES), F32),
                        pltpu.VMEM((V7X_SUBLANES, d), F32), pltpu.VMEM((V7X_SUBLANES, d), F32)],
        compiler_params=_params(("parallel", "parallel", "arbitrary")),
        name="mlstm",
    )(h3, h3, h3, h3, cw, cw, gates_col, gates_row)


def _even_layer(x2d, B, S, w_in, w_out, rel_bias, g, b):
    h3 = _in_proj(x2d, w_in.astype(BF16)).reshape(B, S, -1)
    y_r = _retention(h3, _retention_tables(S))
    y_d = _dilated_attention(h3, _dsa_bias(rel_bias), 4 * RET_W // V7X_LANES)
    T = B * S
    return _out_proj_ln(x2d, [y_r.reshape(T, RET_W), y_d.reshape(T, DSA_W)], w_out.astype(BF16), g, b)


def _odd_layer(x2d, B, S, w_in, gate_b, conv_w, w_out, g, b):
    H = MLSTM_HEADS
    wide = 4 * MLSTM_W
    wg = jnp.zeros((D_MODEL, V7X_LANES), F32).at[:, : 2 * H].set(w_in[:, wide:]).astype(BF16)
    gb = jnp.zeros((1, V7X_LANES), F32).at[0, : 2 * H].set(gate_b)
    h2, gates = _in_proj_gates(x2d, w_in[:, :wide].astype(BF16), wg, gb)
    gates_col = gates.reshape(B, S, V7X_LANES)
    gates_row = gates_col[:, :, : 2 * H].reshape(B, S, 2, H).transpose(0, 3, 2, 1)
    y = _mlstm(h2.reshape(B, S, wide), gates_col, gates_row, conv_w)
    return _out_proj_ln(x2d, [y.reshape(B * S, MLSTM_W)], w_out.astype(BF16), g, b)


def kernel(x, even_w_in, even_w_out, rel_bias, odd_w_in, odd_gate_b, odd_conv_w, odd_w_out, ffn_w_up, ffn_conv_w, ffn_conv_b, ffn_w_down, ln_g, ln_b):
    B, S, D = x.shape
    x2d = x.reshape(B * S, D)
    for layer in range(DEPTH):
        j = layer // 2
        if layer % 2 == 0:
            x2d = _even_layer(x2d, B, S, even_w_in[j], even_w_out[j], rel_bias, ln_g[layer, 0], ln_b[layer, 0])
        else:
            x2d = _odd_layer(x2d, B, S, odd_w_in[j], odd_gate_b[j], odd_conv_w[j], odd_w_out[j], ln_g[layer, 0], ln_b[layer, 0])
        x2d = _ffn_ln(x2d, S, ffn_w_up[layer].astype(BF16), ffn_conv_w[layer], ffn_conv_b[layer],
                      ffn_w_down[layer].astype(BF16), ln_g[layer, 1], ln_b[layer, 1])
    return x2d.reshape(B, S, D)
```

```python
import functools
import math

import jax
import jax.numpy as jnp
from jax import lax
from jax.experimental import pallas as pl
from jax.experimental.pallas import tpu as pltpu

F32 = jnp.float32
BF16 = jnp.bfloat16

D_MODEL = 1024
DEPTH = 2
RET_HEADS = 4
RET_HEAD_DIM = 128
RET_CHUNK = 128
DSA_HEADS = 8
DSA_HEAD_DIM = 64
DSA_PATTERNS = ((128, 1), (512, 4), (2048, 16))
DSA_BLOCK = 128
T5_BUCKETS = 32
T5_MAX_DIST = 2048
MLSTM_HEADS = 4
MLSTM_HEAD_DIM = 256
MLSTM_CHUNK = 128
MLSTM_CONV = 4
D_FF = 2816
FFN_CONV = 3
LN_EPS = 1e-5
DEEPNORM_ALPHA = (2.0 * DEPTH) ** 0.25

RET_W = RET_HEADS * RET_HEAD_DIM
DSA_W = DSA_HEADS * DSA_HEAD_DIM
MLSTM_W = MLSTM_HEADS * MLSTM_HEAD_DIM

V7X_LANES = 128
V7X_SUBLANES = 8
V7X_VMEM_BYTES = 64 * 1024 * 1024
VMEM_LIMIT_BYTES = V7X_VMEM_BYTES - 8 * 1024 * 1024

ROW_TILE = 512
PROJ_CHUNK = 512
FFN_CHUNK = 256
SEQ_TILE = 1024
DSA_TILE = DSA_BLOCK * DSA_PATTERNS[-1][1]
DSA_GROUP = 2

_NT = (((1,), (1,)), ((), ()))
_TN = (((0,), (0,)), ((), ()))


def _params(semantics):
    return pltpu.CompilerParams(dimension_semantics=semantics, vmem_limit_bytes=VMEM_LIMIT_BYTES)


def _resident(shape):
    nd = len(shape)
    return pl.BlockSpec(shape, lambda *_: (0,) * nd, pipeline_mode=pl.Buffered(1))


def _layer_norm(z, g, b):
    mu = jnp.mean(z, -1, keepdims=True)
    zc = z - mu
    var = jnp.mean(zc * zc, -1, keepdims=True)
    return zc * lax.rsqrt(var + LN_EPS) * g + b


def _silu(z):
    return z * jax.nn.sigmoid(z)


def _log_sigmoid(x):
    return jnp.minimum(x, 0.0) - jnp.log1p(jnp.exp(-jnp.abs(x)))


def _lookahead(n, produce, consume, depth=1):
    staged = [produce(j) for j in range(min(depth, n))]
    for j in range(n):
        if j + depth < n:
            staged.append(produce(j + depth))
        consume(j, staged[j])
        staged[j] = None


def _shift_rows(x, prev, s):
    if s == 0:
        return x
    rolled = pltpu.roll(x, s, 0)
    head = pltpu.roll(prev, s, 0)
    rid = lax.broadcasted_iota(jnp.int32, (V7X_SUBLANES, x.shape[1]), 0)
    first = jnp.where(rid < s, head, rolled[:V7X_SUBLANES])
    return jnp.concatenate([first, rolled[V7X_SUBLANES:]], axis=0)


def _causal_dwconv(x, prev, w_ref, taps):
    acc = None
    for k in range(taps):
        term = _shift_rows(x, prev, taps - 1 - k) * w_ref[k : k + 1, :]
        acc = term if acc is None else acc + term
    return acc


def _even_in_proj_kernel(x_ref, w_ref, cos_ref, sin_ref, q_ref, kt_ref, v_ref, g_ref, d_ref):
    xb = x_ref[...].astype(BF16)
    cos = cos_ref[...]
    sin = sin_ref[...]
    tn, d = RET_W, RET_HEAD_DIM

    def consume(j, h):
        if j < 2:
            for hh in range(RET_HEADS):
                c = h[:, hh * d : (hh + 1) * d]
                rot = c * cos + pltpu.roll(c, d // 2, 1) * sin
                if j == 0:
                    q_ref[:, hh * d : (hh + 1) * d] = rot.astype(BF16)
                else:
                    kt_ref[hh * d : (hh + 1) * d, :] = rot.T.astype(BF16)
        elif j == 2:
            v_ref[...] = h.astype(BF16)
        elif j == 3:
            g_ref[...] = _silu(h)
        else:
            d_ref[:, (j - 4) * tn : (j - 3) * tn] = h

    _lookahead(w_ref.shape[1] // tn, lambda j: jnp.dot(xb, w_ref[:, j * tn : (j + 1) * tn], preferred_element_type=F32), consume)


def _even_in_proj(x2d, seq_len, w, cos_t, sin_t):
    T, K = x2d.shape
    tm = min(ROW_TILE, seq_len)
    tps = seq_len // tm
    row = lambda i: (i, 0)
    pos = pl.BlockSpec((tm, RET_HEAD_DIM), lambda i: (i % tps, 0))
    shapes = ((T, RET_W), (RET_W, T), (T, RET_W), (T, RET_W), (T, 3 * DSA_W))
    dtypes = (BF16, BF16, BF16, F32, F32)
    return pl.pallas_call(
        _even_in_proj_kernel,
        grid=(T // tm,),
        in_specs=[pl.BlockSpec((tm, K), row), _resident(w.shape), pos, pos],
        out_specs=[pl.BlockSpec((tm, RET_W), row), pl.BlockSpec((RET_W, tm), lambda i: (0, i)),
                   pl.BlockSpec((tm, RET_W), row), pl.BlockSpec((tm, RET_W), row), pl.BlockSpec((tm, 3 * DSA_W), row)],
        out_shape=[jax.ShapeDtypeStruct(s, dt) for s, dt in zip(shapes, dtypes)],
        compiler_params=_params(("parallel",)),
        name="even_in_proj",
    )(x2d, w, cos_t, sin_t)


def _odd_in_proj_kernel(x_ref, w_ref, wg_ref, gb_ref, cw_ref, q_ref, kt_ref, v_ref, o_ref, gt_ref, carry_ref, *, tiles_per_seq):
    @pl.when(pl.program_id(0) % tiles_per_seq == 0)
    def _():
        carry_ref[...] = jnp.zeros_like(carry_ref)

    xb = x_ref[...].astype(BF16)
    tm = xb.shape[0]
    tn = PROJ_CHUNK
    per = MLSTM_W // tn

    order = [c for pair in zip(range(2 * per), range(2 * per, 4 * per)) for c in pair]

    def produce(i):
        j = order[i]
        return jnp.dot(xb, w_ref[:, j * tn : (j + 1) * tn], preferred_element_type=F32)

    def consume(i, h):
        j = order[i]
        sl = slice(j * tn, (j + 1) * tn)
        if j < 2 * per:
            prev = carry_ref[:, sl]
            carry_ref[:, sl] = h[tm - V7X_SUBLANES :, :]
            act = _silu(_causal_dwconv(h, prev, cw_ref.at[:, sl], MLSTM_CONV))
            if j < per:
                q_ref[:, sl] = act.astype(BF16)
            else:
                kt_ref[(j - per) * tn : (j - per + 1) * tn, :] = (act * (MLSTM_HEAD_DIM ** -0.5)).T.astype(BF16)
        elif j < 3 * per:
            v_ref[:, (j - 2 * per) * tn : (j - 2 * per + 1) * tn] = h.astype(BF16)
        else:
            o_ref[:, (j - 3 * per) * tn : (j - 3 * per + 1) * tn] = jax.nn.sigmoid(h)

    _lookahead(4 * per, produce, consume)
    gates = jnp.dot(xb, wg_ref[...], preferred_element_type=F32) + gb_ref[...]
    lane = lax.broadcasted_iota(jnp.int32, (1, V7X_LANES), 1)
    gt_ref[...] = jnp.where(lane < MLSTM_HEADS, gates, _log_sigmoid(gates))


def _odd_in_proj(x2d, seq_len, w, wg, gb, conv_w):
    T, K = x2d.shape
    tm = min(ROW_TILE, seq_len)
    W = MLSTM_W
    cw = jnp.zeros((V7X_SUBLANES, 2 * W), F32).at[:MLSTM_CONV].set(conv_w)
    row = lambda i: (i, 0)
    shapes = ((T, W), (W, T), (T, W), (T, W), (T, V7X_LANES))
    dtypes = (BF16, BF16, BF16, F32, F32)
    return pl.pallas_call(
        functools.partial(_odd_in_proj_kernel, tiles_per_seq=seq_len // tm),
        grid=(T // tm,),
        in_specs=[pl.BlockSpec((tm, K), row), _resident(w.shape), _resident(wg.shape), _resident(gb.shape),
                  _resident(cw.shape)],
        out_specs=[pl.BlockSpec((tm, W), row), pl.BlockSpec((W, tm), lambda i: (0, i)), pl.BlockSpec((tm, W), row),
                   pl.BlockSpec((tm, W), row), pl.BlockSpec((tm, V7X_LANES), row)],
        out_shape=[jax.ShapeDtypeStruct(s, dt) for s, dt in zip(shapes, dtypes)],
        scratch_shapes=[pltpu.VMEM((V7X_SUBLANES, 2 * MLSTM_W), F32)],
        compiler_params=_params(("arbitrary",)),
        name="odd_in_proj",
    )(x2d, w, wg, gb, cw)


def _out_proj_ln_kernel(*refs, n_y):
    x_ref = refs[0]
    y_refs = refs[1 : 1 + n_y]
    w_ref, g_ref, b_ref, o_ref = refs[1 + n_y :]
    tm = x_ref.shape[0]
    for lo in range(0, tm, tm // 2):
        rows = slice(lo, lo + tm // 2)
        acc = None
        row = 0
        for y_ref in y_refs:
            kw = y_ref.shape[1]
            part = jnp.dot(y_ref[rows, :], w_ref[row : row + kw, :], preferred_element_type=F32)
            acc = part if acc is None else acc + part
            row += kw
        z = DEEPNORM_ALPHA * x_ref[rows, :] + acc
        o_ref[rows, :] = _layer_norm(z, g_ref[...], b_ref[...])


def _out_proj_ln(x2d, ys, w, g, b):
    T, D = x2d.shape
    tm = min(ROW_TILE, T)
    row = lambda i: (i, 0)
    return pl.pallas_call(
        functools.partial(_out_proj_ln_kernel, n_y=len(ys)),
        grid=(T // tm,),
        in_specs=[pl.BlockSpec((tm, D), row)]
        + [pl.BlockSpec((tm, y.shape[1]), row) for y in ys]
        + [_resident(w.shape), _resident((1, D)), _resident((1, D))],
        out_specs=pl.BlockSpec((tm, D), row),
        out_shape=jax.ShapeDtypeStruct((T, D), F32),
        compiler_params=_params(("parallel",)),
        name="out_proj_ln",
    )(x2d, *ys, w, g.reshape(1, D), b.reshape(1, D))


def _ffn_kernel(x_ref, wup_ref, cw_ref, cb_ref, wdn_ref, g_ref, b_ref, o_ref, carry_ref, act_ref, *, tiles_per_seq):
    @pl.when(pl.program_id(0) % tiles_per_seq == 0)
    def _():
        carry_ref[...] = jnp.zeros_like(carry_ref)

    xb = x_ref[...].astype(BF16)
    tm = xb.shape[0]

    def up_dots(j):
        lo = j * FFN_CHUNK
        return (jnp.dot(xb, wup_ref[:, lo : lo + FFN_CHUNK], preferred_element_type=F32),
                jnp.dot(xb, wup_ref[:, D_FF + lo : D_FF + lo + FFN_CHUNK], preferred_element_type=F32))

    def activate(j, gate_up):
        gate, up = gate_up
        sl = slice(j * FFN_CHUNK, (j + 1) * FFN_CHUNK)
        prev = carry_ref[:, sl]
        carry_ref[:, sl] = gate[tm - V7X_SUBLANES :, :]
        conv = _causal_dwconv(gate, prev, cw_ref.at[:, sl], FFN_CONV) + cb_ref[:, sl]
        act_ref[:, sl] = (_silu(conv) * up).astype(BF16)

    _lookahead(D_FF // FFN_CHUNK, up_dots, activate)
    for lo in range(0, tm, tm // 2):
        rows = slice(lo, lo + tm // 2)
        ffn = jnp.dot(act_ref[rows, :], wdn_ref[...], preferred_element_type=F32)
        z = DEEPNORM_ALPHA * x_ref[rows, :] + ffn
        o_ref[rows, :] = _layer_norm(z, g_ref[...], b_ref[...])


def _ffn_ln(x2d, seq_len, w_up, conv_w, conv_b, w_down, g, b):
    T, D = x2d.shape
    tm = min(ROW_TILE, seq_len)
    cw = jnp.zeros((V7X_SUBLANES, D_FF), F32).at[:FFN_CONV].set(conv_w)
    row = lambda i: (i, 0)
    return pl.pallas_call(
        functools.partial(_ffn_kernel, tiles_per_seq=seq_len // tm),
        grid=(T // tm,),
        in_specs=[
            pl.BlockSpec((tm, D), row),
            _resident(w_up.shape),
            _resident(cw.shape),
            _resident((1, D_FF)),
            _resident(w_down.shape),
            _resident((1, D)),
            _resident((1, D)),
        ],
        out_specs=pl.BlockSpec((tm, D), row),
        out_shape=jax.ShapeDtypeStruct((T, D), F32),
        scratch_shapes=[pltpu.VMEM((V7X_SUBLANES, D_FF), F32), pltpu.VMEM((tm, D_FF), BF16)],
        compiler_params=_params(("arbitrary",)),
        name="conv_ffn_ln",
    )(x2d, w_up, cw, conv_b.reshape(1, D_FF), w_down, g.reshape(1, D), b.reshape(1, D))


def _retention_tables(seq_len):
    H, C, d = RET_HEADS, RET_CHUNK, RET_HEAD_DIM
    scale = d ** -0.5
    log_gamma = jnp.log1p(-jnp.exp2(-5.0 - jnp.arange(H, dtype=F32)))
    idx = jnp.arange(C, dtype=F32)
    rel = idx[:, None] - idx[None, :]
    decay = jnp.where(rel >= 0, jnp.exp(log_gamma[:, None, None] * jnp.maximum(rel, 0.0)), 0.0) * scale
    k_w = jnp.exp(log_gamma[:, None] * (C - 1 - idx)[None, :]) * scale
    q_w = jnp.exp(log_gamma[:, None] * (idx + 1.0)[None, :])
    kw_t = jnp.broadcast_to(k_w[:, :, None], (H, C, d))
    qw_t = jnp.broadcast_to(q_w[:, :, None], (H, C, d))
    cd_t = jnp.broadcast_to(jnp.exp(log_gamma * C)[:, None, None], (H, 1, d))
    inv = 1.0 / (10000.0 ** (jnp.arange(0, d, 2, dtype=F32) / d))
    ang = jnp.arange(seq_len, dtype=F32)[:, None] * inv[None, :]
    cos, sin = jnp.cos(ang), jnp.sin(ang)
    cos_t = jnp.concatenate([cos, cos], -1)
    sin_t = jnp.concatenate([-sin, sin], -1)
    return cos_t, sin_t, decay, kw_t, qw_t, cd_t


def _retention_kernel(q_ref, kt_ref, v_ref, g_ref, dec_ref, kw_ref, qw_ref, cd_ref, o_ref, r_ref):
    @pl.when(pl.program_id(2) == 0)
    def _():
        r_ref[...] = jnp.zeros_like(r_ref)

    C = RET_CHUNK
    decay = dec_ref[0]
    kw = kw_ref[0]
    qw = qw_ref[0]
    cd = cd_ref[0]

    def local(n):
        sl = slice(n * C, (n + 1) * C)
        qb = q_ref[0, sl, :]
        kt = kt_ref[:, sl]
        vb = v_ref[0, sl, :]
        scores = jnp.dot(qb, kt, preferred_element_type=F32) * decay
        y_intra = jnp.dot(scores.astype(BF16), vb, preferred_element_type=F32)
        kv = jnp.dot(kt, (vb.astype(F32) * kw).astype(BF16), preferred_element_type=F32)
        q_dec = (qb.astype(F32) * qw).astype(BF16)
        return y_intra, kv, q_dec

    def recurrent(n, staged):
        y_intra, kv, q_dec = staged
        sl = slice(n * C, (n + 1) * C)
        r = r_ref[...]
        y = y_intra + jnp.dot(q_dec, r.astype(BF16), preferred_element_type=F32)
        r_ref[...] = cd * r + kv
        mu = jnp.mean(y, -1, keepdims=True)
        yc = y - mu
        var = jnp.mean(yc * yc, -1, keepdims=True)
        o_ref[0, sl, :] = (yc * lax.rsqrt(var + LN_EPS) * g_ref[0, sl, :]).astype(o_ref.dtype)

    _lookahead(q_ref.shape[1] // C, local, recurrent)


def _retention(q3, kt, v3, g3, tables):
    B, S, _ = v3.shape
    H, d, C = RET_HEADS, RET_HEAD_DIM, RET_CHUNK
    ts = min(SEQ_TILE, S)
    tiles = S // ts
    _, _, decay, kw_t, qw_t, cd_t = tables
    col = pl.BlockSpec((1, ts, d), lambda b, h, c: (b, c, h))
    per_head = lambda shape: pl.BlockSpec((1,) + shape, lambda b, h, c: (h, 0, 0))
    return pl.pallas_call(
        _retention_kernel,
        grid=(B, H, tiles),
        in_specs=[col, pl.BlockSpec((d, ts), lambda b, h, c: (h, b * tiles + c)), col, col,
                  per_head((C, C)), per_head((C, d)), per_head((C, d)), per_head((1, d))],
        out_specs=pl.BlockSpec((1, ts, d), lambda b, h, c: (b, c, h)),
        out_shape=jax.ShapeDtypeStruct((B, S, H * d), BF16),
        scratch_shapes=[pltpu.VMEM((d, d), F32)],
        compiler_params=_params(("parallel", "parallel", "arbitrary")),
        name="retention",
    )(q3, kt, v3, g3, decay, kw_t, qw_t, cd_t)


def _t5_bucket(dist):
    exact = T5_BUCKETS // 2
    n = jnp.maximum(dist, 0)
    large = exact + (jnp.log(jnp.maximum(n, 1).astype(F32) / exact) / math.log(T5_MAX_DIST / exact) * (T5_BUCKETS - exact)).astype(jnp.int32)
    large = jnp.minimum(large, T5_BUCKETS - 1)
    return jnp.where(n < exact, n, large)


def _dsa_bucket_tables():
    blk = DSA_BLOCK
    qi = jnp.arange(blk)[:, None]
    kj = jnp.arange(2 * blk)[None, :]
    tabs = []
    for window, dilation in DSA_PATTERNS:
        assert window // dilation <= blk
        for offset in (blk, 0):
            dist = qi + offset - kj
            valid = (dist >= 0) & (dist <= window // dilation)
            tabs.append(jnp.where(valid, _t5_bucket(dist * dilation), -1))
    return jnp.stack(tabs).reshape(len(DSA_PATTERNS), 2, blk, 2 * blk).astype(jnp.int32)


def _dsa_bias_kernel(rb_ref, bk_ref, o_ref):
    for v in range(2):
        bk = bk_ref[0, v]
        for h in range(DSA_HEADS):
            acc = jnp.full(bk.shape, -jnp.inf, F32)
            for b in range(T5_BUCKETS):
                acc = jnp.where(bk == b, rb_ref[b, h], acc)
            o_ref[0, v, h] = acc


def _dsa_bias(rel_bias):
    buckets = _dsa_bucket_tables()
    P, blk = len(DSA_PATTERNS), DSA_BLOCK
    return pl.pallas_call(
        _dsa_bias_kernel,
        grid=(P,),
        in_specs=[pl.BlockSpec(memory_space=pltpu.SMEM), pl.BlockSpec((1, 2, blk, 2 * blk), lambda p: (p, 0, 0, 0))],
        out_specs=pl.BlockSpec((1, 2, DSA_HEADS, blk, 2 * blk), lambda p: (p, 0, 0, 0, 0)),
        out_shape=jax.ShapeDtypeStruct((P, 2, DSA_HEADS, blk, 2 * blk), F32),
        compiler_params=_params(("parallel",)),
        name="dsa_bias",
    )(rel_bias, buckets)


def _dsa_kernel(q_ref, k_ref, v_ref, bias_ref, o_ref, qd, kd, vd, acc_o, acc_m, acc_l):
    blk = DSA_BLOCK
    S = q_ref.shape[1]
    tq = o_ref.shape[1]
    t = pl.program_id(2)
    lane = lax.broadcasted_iota(jnp.int32, (1, V7X_LANES), 1)
    head0 = lane < DSA_HEAD_DIM
    scale = DSA_HEAD_DIM ** -0.5

    @pl.when(t == 0)
    def _():
        for p, (_, r) in enumerate(DSA_PATTERNS):
            chunks_per_residue = S // (blk * r)

            def prep(idx, carry, p=p, r=r, chunks_per_residue=chunks_per_residue):
                rho = idx // chunks_per_residue
                start = rho + (idx % chunks_per_residue) * (blk * r)
                rows = pl.ds(start, blk, stride=r) if r > 1 else pl.ds(pl.multiple_of(start, blk), blk)
                dst = pl.ds(pl.multiple_of(idx * blk, blk), blk)
                qd[p, dst, :] = (q_ref.at[0][rows, :] * scale).astype(BF16)
                kd[p, dst, :] = k_ref.at[0][rows, :].astype(BF16)
                vd[p, dst, :] = v_ref.at[0][rows, :].astype(BF16)
                return carry

            lax.fori_loop(0, S // blk, prep, 0, unroll=2)

    def block_logits(p, r, idx):
        blocks_per_residue = tq // (blk * r)
        rho = idx // blocks_per_residue
        nb = idx % blocks_per_residue
        l0 = t * (tq // r) + nb * blk
        first = l0 == 0
        q_row = pl.multiple_of(rho * (S // r) + l0, blk)
        k_row = pl.multiple_of(jnp.where(first, q_row, q_row - blk), blk)
        qb = qd[p, pl.ds(q_row, blk), :]
        zero = jnp.zeros_like(qb)
        q2 = jnp.concatenate([jnp.where(head0, qb, zero), jnp.where(head0, zero, qb)], axis=0)
        bias = bias_ref[p, pl.ds(first.astype(jnp.int32), 1), 0][0]
        logits = lax.dot_general(q2, kd[p, pl.ds(k_row, 2 * blk), :], _NT, preferred_element_type=F32) + bias
        start = rho + nb * (blk * r)
        rows = pl.ds(start, blk, stride=r) if r > 1 else pl.ds(pl.multiple_of(start, blk), blk)
        return logits, k_row, rows

    def block_output(p, logits, k_row, rows):
        m = jnp.max(logits, -1, keepdims=True)
        e = jnp.exp(logits - m).astype(BF16)
        vb = vd[p, pl.ds(k_row, 2 * blk), :]
        o2 = jnp.dot(e, jnp.concatenate([vb, jnp.ones_like(vb)], axis=1), preferred_element_type=F32)
        acc_o.at[p][rows, :] = jnp.where(head0, o2[:blk, :V7X_LANES], o2[blk:, :V7X_LANES])
        acc_m.at[p][rows, :] = jnp.where(head0, m[:blk], m[blk:])
        acc_l.at[p][rows, :] = jnp.where(head0, o2[:blk, V7X_LANES:], o2[blk:, V7X_LANES:])

    blocks = [(p, r, idx) for p, (_, r) in enumerate(DSA_PATTERNS) for idx in range(tq // blk)]
    staged = [block_logits(*blocks[i]) for i in range(DSA_GROUP)]
    for i, (p, _, _) in enumerate(blocks):
        if i + DSA_GROUP < len(blocks):
            staged.append(block_logits(*blocks[i + DSA_GROUP]))
        block_output(p, *staged[i])

    for c in range(tq // blk):
        sl = slice(c * blk, (c + 1) * blk)
        ms = [acc_m[bi, sl, :] for bi in range(len(DSA_PATTERNS))]
        m_all = functools.reduce(jnp.maximum, ms)
        num = den = None
        for bi, m in enumerate(ms):
            w = jnp.exp(m - m_all)
            n_i = w * acc_o[bi, sl, :]
            d_i = w * acc_l[bi, sl, :]
            num = n_i if num is None else num + n_i
            den = d_i if den is None else den + d_i
        o_ref[0, sl, :] = (num / den).astype(o_ref.dtype)


def _dilated_attention(d3, bias):
    B, S, _ = d3.shape
    pairs = DSA_W // V7X_LANES
    tq = DSA_TILE
    assert S % tq == 0 and S >= 2 * tq
    P, blk = len(DSA_PATTERNS), DSA_BLOCK
    bias5 = bias.reshape(P, 2, pairs, 2 * blk, 2 * blk)
    seq = lambda off: pl.BlockSpec((1, S, V7X_LANES), lambda b, p, t: (b, 0, off + p))
    return pl.pallas_call(
        _dsa_kernel,
        grid=(B, pairs, S // tq),
        in_specs=[seq(0), seq(pairs), seq(2 * pairs),
                  pl.BlockSpec((P, 2, 1, 2 * blk, 2 * blk), lambda b, p, t: (0, 0, p, 0, 0))],
        out_specs=pl.BlockSpec((1, tq, V7X_LANES), lambda b, p, t: (b, t, p)),
        out_shape=jax.ShapeDtypeStruct((B, S, DSA_W), BF16),
        scratch_shapes=[pltpu.VMEM((P, S, V7X_LANES), BF16)] * 3 + [pltpu.VMEM((P, tq, V7X_LANES), F32)] * 3,
        compiler_params=_params(("parallel", "parallel", "arbitrary")),
        name="dilated_attention",
    )(d3, d3, d3, bias5)


def _mlstm_kernel(q_ref, kt_ref, v_ref, og_ref, gr_ref, o_ref, c_ref, m_ref):
    @pl.when(pl.program_id(2) == 0)
    def _():
        c_ref[...] = jnp.zeros_like(c_ref)
        m_ref[...] = jnp.zeros_like(m_ref)

    C = MLSTM_CHUNK
    dv = MLSTM_HEAD_DIM
    L = V7X_LANES
    n_chunks = q_ref.shape[1] // C
    ri = lax.broadcasted_iota(jnp.int32, (C, C), 0)
    ci = lax.broadcasted_iota(jnp.int32, (C, C), 1)
    causal = ci <= ri
    eye = ci == ri
    rep = lambda col: jnp.broadcast_to(col, (C, L))

    def gate_vectors(n):
        sl = slice(n * C, (n + 1) * C)
        li_row = gr_ref[0, 0, 0:1, sl]
        lf_row = gr_ref[0, 0, 1:2, sl]
        li_col = jnp.sum(jnp.where(eye, li_row, 0.0), -1, keepdims=True)
        a_col = jnp.sum(jnp.where(causal, lf_row, 0.0), -1, keepdims=True)
        g_tot = jnp.sum(lf_row, -1, keepdims=True)
        return li_row, li_col, a_col, g_tot

    def decay_matrix(gv):
        li_row, li_col, a_col, g_tot = gv
        a_rep = rep(a_col)
        a_row = jnp.sum(jnp.where(eye, a_rep, 0.0), 0, keepdims=True)
        w_state = g_tot - a_col + li_col
        m_loc = jnp.max(w_state, 0, keepdims=True)
        ew_rep = rep(jnp.exp(w_state - m_loc))
        dmat = jnp.where(causal, a_rep - a_row + li_row, -jnp.inf)
        d_max = jnp.max(dmat, -1, keepdims=True)
        return a_rep, g_tot, m_loc, ew_rep, dmat, d_max

    def local_matmuls(n, dm):
        a_rep, g_tot, m_loc, ew_rep, dmat, d_max = dm
        sl = slice(n * C, (n + 1) * C)
        qb = q_ref[0, sl, :]
        kt = kt_ref[:, sl]
        vb = v_ref[0, sl, :]
        d_rep = rep(d_max)
        p = (jnp.exp(dmat - d_rep) * jnp.dot(qb, kt, preferred_element_type=F32)).astype(BF16)
        pv = jnp.dot(p, jnp.concatenate([vb, jnp.ones((C, L), BF16)], axis=1), preferred_element_type=F32)
        vw = jnp.concatenate([vb.astype(F32) * jnp.concatenate([ew_rep] * (dv // L), axis=1), ew_rep], axis=1)
        kv_ext = jnp.dot(kt, vw.astype(BF16), preferred_element_type=F32)
        return qb, pv, kv_ext, a_rep, d_rep, g_tot, m_loc

    def recurrent(n, staged):
        qb, pv, kv_ext, a_rep, d_rep, g_tot, m_loc = staged
        sl = slice(n * C, (n + 1) * C)
        c_s = c_ref[...]
        m_s = m_ref[...][:, 0:1]
        inter = jnp.dot(qb, c_s.astype(BF16), preferred_element_type=F32)
        inter_log = a_rep + m_s
        m_row = jnp.maximum(inter_log, d_rep)
        f_intra = jnp.exp(d_rep - m_row)
        f_inter = jnp.exp(inter_log - m_row)
        den = f_intra * pv[:, dv:] + f_inter * inter[:, dv:]
        r = 1.0 / jnp.maximum(jnp.abs(den), jnp.exp(-m_row))
        for lo in range(0, dv, L):
            num = f_intra * pv[:, lo : lo + L] + f_inter * inter[:, lo : lo + L]
            o_ref[0, sl, lo : lo + L] = (num * r * og_ref[0, sl, lo : lo + L]).astype(o_ref.dtype)
        m_new = jnp.maximum(g_tot + m_s, m_loc)
        a_old = jnp.exp(g_tot + m_s - m_new)
        a_new = jnp.exp(m_loc - m_new)
        c_ref[...] = a_old * c_s + a_new * kv_ext
        m_ref[...] = jnp.broadcast_to(m_new, m_ref.shape)

    gvs = [gate_vectors(n) for n in range(n_chunks)]
    dms = [decay_matrix(gv) for gv in gvs]
    staged = [local_matmuls(n, dm) for n, dm in enumerate(dms)]
    for n in range(n_chunks):
        recurrent(n, staged[n])


def _mlstm(q3, kt, v3, o3, gates_row):
    B, S, _ = v3.shape
    H, d = MLSTM_HEADS, MLSTM_HEAD_DIM
    ts = min(SEQ_TILE, S)
    tiles = S // ts
    col = pl.BlockSpec((1, ts, d), lambda b, h, c: (b, c, h))
    return pl.pallas_call(
        _mlstm_kernel,
        grid=(B, H, tiles),
        in_specs=[col, pl.BlockSpec((d, ts), lambda b, h, c: (h, b * tiles + c)), col, col,
                  pl.BlockSpec((1, 1, 2, ts), lambda b, h, c: (b, h, 0, c))],
        out_specs=pl.BlockSpec((1, ts, d), lambda b, h, c: (b, c, h)),
        out_shape=jax.ShapeDtypeStruct((B, S, H * d), BF16),
        scratch_shapes=[pltpu.VMEM((d, d + V7X_LANES), F32), pltpu.VMEM((1, V7X_LANES), F32)],
        compiler_params=_params(("parallel", "parallel", "arbitrary")),
        name="mlstm",
    )(q3, kt, v3, o3, gates_row)


def _even_layer(x2d, B, S, w_in, w_out, rel_bias, g, b):
    tables = _retention_tables(S)
    q, kt, v, gate, dsa = _even_in_proj(x2d, S, w_in.astype(BF16), tables[0], tables[1])
    y_r = _retention(q.reshape(B, S, -1), kt, v.reshape(B, S, -1), gate.reshape(B, S, -1), tables)
    y_d = _dilated_attention(dsa.reshape(B, S, -1), _dsa_bias(rel_bias))
    T = B * S
    return _out_proj_ln(x2d, [y_r.reshape(T, RET_W), y_d.reshape(T, DSA_W)], w_out.astype(BF16), g, b)


def _odd_layer(x2d, B, S, w_in, gate_b, conv_w, w_out, g, b):
    H = MLSTM_HEADS
    wide = 4 * MLSTM_W
    wg = jnp.zeros((D_MODEL, V7X_LANES), F32).at[:, : 2 * H].set(w_in[:, wide:]).astype(BF16)
    gb = jnp.zeros((1, V7X_LANES), F32).at[0, : 2 * H].set(gate_b)
    q, kt, v, og, gates = _odd_in_proj(x2d, S, w_in[:, :wide].astype(BF16), wg, gb, conv_w)
    gates_row = gates[:, : 2 * H].reshape(B, S, 2, H).transpose(0, 3, 2, 1)
    y = _mlstm(q.reshape(B, S, -1), kt, v.reshape(B, S, -1), og.reshape(B, S, -1), gates_row)
    return _out_proj_ln(x2d, [y.reshape(B * S, MLSTM_W)], w_out.astype(BF16), g, b)


def kernel(x, even_w_in, even_w_out, rel_bias, odd_w_in, odd_gate_b, odd_conv_w, odd_w_out, ffn_w_up, ffn_conv_w, ffn_conv_b, ffn_w_down, ln_g, ln_b):
    B, S, D = x.shape
    x2d = x.reshape(B * S, D)
    for layer in range(DEPTH):
        j = layer // 2
        if layer % 2 == 0:
            x2d = _even_layer(x2d, B, S, even_w_in[j], even_w_out[j], rel_bias, ln_g[layer, 0], ln_b[layer, 0])
        else:
            x2d = _odd_layer(x2d, B, S, odd_w_in[j], odd_gate_b[j], odd_conv_w[j], odd_w_out[j], ln_g[layer, 0], ln_b[layer, 0])
        x2d = _ffn_ln(x2d, S, ffn_w_up[layer].astype(BF16), ffn_conv_w[layer], ffn_conv_b[layer],
                      ffn_w_down[layer].astype(BF16), ln_g[layer, 1], ln_b[layer, 1])
    return x2d.reshape(B, S, D)
```

```python
import functools
import math

import jax
import jax.numpy as jnp
from jax import lax
from jax.experimental import pallas as pl
from jax.experimental.pallas import tpu as pltpu

F32 = jnp.float32
BF16 = jnp.bfloat16

D_MODEL = 1024
DEPTH = 2
RET_HEADS = 4
RET_HEAD_DIM = 128
RET_CHUNK = 128
DSA_HEADS = 8
DSA_HEAD_DIM = 64
DSA_PATTERNS = ((128, 1), (512, 4), (2048, 16))
DSA_BLOCK = 128
T5_BUCKETS = 32
T5_MAX_DIST = 2048
MLSTM_HEADS = 4
MLSTM_HEAD_DIM = 256
MLSTM_CHUNK = 128
MLSTM_CONV = 4
D_FF = 2816
FFN_CONV = 3
LN_EPS = 1e-5
DEEPNORM_ALPHA = (2.0 * DEPTH) ** 0.25

RET_W = RET_HEADS * RET_HEAD_DIM
DSA_W = DSA_HEADS * DSA_HEAD_DIM
MLSTM_W = MLSTM_HEADS * MLSTM_HEAD_DIM

V7X_LANES = 128
V7X_SUBLANES = 8
V7X_VMEM_BYTES = 64 * 1024 * 1024
VMEM_LIMIT_BYTES = V7X_VMEM_BYTES - 8 * 1024 * 1024

ROW_TILE = 512
PROJ_CHUNK = 512
FFN_CHUNK = 256
SEQ_TILE = 1024
DSA_TILE = DSA_BLOCK * DSA_PATTERNS[-1][1]
DSA_GROUP = 2

_NT = (((1,), (1,)), ((), ()))
_TN = (((0,), (0,)), ((), ()))


def _params(semantics):
    return pltpu.CompilerParams(dimension_semantics=semantics, vmem_limit_bytes=VMEM_LIMIT_BYTES)


def _resident(shape):
    nd = len(shape)
    return pl.BlockSpec(shape, lambda *_: (0,) * nd, pipeline_mode=pl.Buffered(1))


def _resident_layer(stacked_shape, layer):
    return pl.BlockSpec((None,) + tuple(stacked_shape[1:]), lambda *_: (layer, 0, 0), pipeline_mode=pl.Buffered(1))


def _layer_norm(z, g, b):
    mu = jnp.mean(z, -1, keepdims=True)
    zc = z - mu
    var = jnp.mean(zc * zc, -1, keepdims=True)
    return zc * lax.rsqrt(var + LN_EPS) * g + b


def _silu(z):
    return z * jax.nn.sigmoid(z)


def _log_sigmoid(x):
    return jnp.minimum(x, 0.0) - jnp.log1p(jnp.exp(-jnp.abs(x)))


def _lookahead(n, produce, consume, depth=1):
    staged = [produce(j) for j in range(min(depth, n))]
    for j in range(n):
        if j + depth < n:
            staged.append(produce(j + depth))
        consume(j, staged[j])
        staged[j] = None


def _shift_rows(x, prev, s):
    if s == 0:
        return x
    rolled = pltpu.roll(x, s, 0)
    head = pltpu.roll(prev, s, 0)
    rid = lax.broadcasted_iota(jnp.int32, (V7X_SUBLANES, x.shape[1]), 0)
    first = jnp.where(rid < s, head, rolled[:V7X_SUBLANES])
    return jnp.concatenate([first, rolled[V7X_SUBLANES:]], axis=0)


def _causal_dwconv(x, prev, w_ref, taps):
    acc = None
    for k in range(taps):
        term = _shift_rows(x, prev, taps - 1 - k) * w_ref[k : k + 1, :]
        acc = term if acc is None else acc + term
    return acc


def _even_in_proj_kernel(x_ref, w_ref, cos_ref, sin_ref, q_ref, kt_ref, v_ref, g_ref, *rest):
    d_refs, stage_ref = rest[:-1], rest[-1]
    xb = x_ref[...].astype(BF16)
    tm = xb.shape[0]
    cos = cos_ref[...]
    sin = sin_ref[...]
    tn, d = RET_W, RET_HEAD_DIM

    def consume(j, h):
        if j < 2:
            for hh in range(RET_HEADS):
                c = h[:, hh * d : (hh + 1) * d]
                rot = c * cos + pltpu.roll(c, d // 2, 1) * sin
                if j == 0:
                    q_ref[:, hh * d : (hh + 1) * d] = rot.astype(BF16)
                else:
                    kt_ref[hh * d : (hh + 1) * d, :] = rot.T.astype(BF16)
        elif j == 2:
            v_ref[...] = h.astype(BF16)
        elif j == 3:
            g_ref[...] = _silu(h)
        else:
            if j == 4:
                h = h * (DSA_HEAD_DIM ** -0.5)
            lo = (j - 4) * tn
            d_refs[0][:, lo : lo + tn] = h.astype(BF16)
            src, dst = stage_ref.at[0], stage_ref.at[1]
            for s in range(tn // V7X_LANES):
                src[s] = h[:, s * V7X_LANES : (s + 1) * V7X_LANES]
            prev_r = 1
            for d_ref, (_, r) in zip(d_refs[1:], DSA_PATTERNS[1:]):
                step = r // prev_r
                n_rows = tm // r
                for rho_prev in range(prev_r):
                    for a in range(step):
                        rho = rho_prev + prev_r * a
                        for s in range(tn // V7X_LANES):
                            rows = src.at[s][pl.ds(rho_prev * (tm // prev_r) + a, n_rows, stride=step), :]
                            if r != DSA_PATTERNS[-1][1]:
                                dst[s, rho * n_rows : (rho + 1) * n_rows, :] = rows
                            d_ref[0, rho, :, lo + s * V7X_LANES : lo + (s + 1) * V7X_LANES] = rows.astype(BF16)
                src, dst, prev_r = dst, src, r

    _lookahead(w_ref.shape[1] // tn, lambda j: jnp.dot(xb, w_ref[:, j * tn : (j + 1) * tn], preferred_element_type=F32), consume)


def _even_in_proj(x2d, seq_len, w, cos_t, sin_t):
    T, K = x2d.shape
    tm = min(ROW_TILE, seq_len)
    tps = seq_len // tm
    B = T // seq_len
    row = lambda i: (i, 0)
    pos = pl.BlockSpec((tm, RET_HEAD_DIM), lambda i: (i % tps, 0))
    shapes = [(T, RET_W), (RET_W, T), (T, RET_W), (T, RET_W), (T, 3 * DSA_W)]
    dtypes = [BF16, BF16, BF16, F32, BF16]
    specs = [pl.BlockSpec((tm, RET_W), row), pl.BlockSpec((RET_W, tm), lambda i: (0, i)),
             pl.BlockSpec((tm, RET_W), row), pl.BlockSpec((tm, RET_W), row), pl.BlockSpec((tm, 3 * DSA_W), row)]
    for _, r in DSA_PATTERNS[1:]:
        shapes.append((B, r, seq_len // r, 3 * DSA_W))
        dtypes.append(BF16)
        specs.append(pl.BlockSpec((1, r, tm // r, 3 * DSA_W), lambda i: (i // tps, 0, i % tps, 0)))
    return pl.pallas_call(
        _even_in_proj_kernel,
        grid=(T // tm,),
        in_specs=[pl.BlockSpec((tm, K), row), _resident(w.shape), pos, pos],
        out_specs=specs,
        out_shape=[jax.ShapeDtypeStruct(s, dt) for s, dt in zip(shapes, dtypes)],
        scratch_shapes=[pltpu.VMEM((2, RET_W // V7X_LANES, tm, V7X_LANES), F32)],
        compiler_params=_params(("parallel",)),
        name="even_in_proj",
    )(x2d, w, cos_t, sin_t)


def _odd_in_proj_kernel(x_ref, w_ref, wg_ref, gb_ref, cw_ref, q_ref, kt_ref, v_ref, o_ref, gt_ref, carry_ref, *, tiles_per_seq):
    @pl.when(pl.program_id(0) % tiles_per_seq == 0)
    def _():
        carry_ref[...] = jnp.zeros_like(carry_ref)

    xb = x_ref[...].astype(BF16)
    tm = xb.shape[0]
    tn = PROJ_CHUNK
    per = MLSTM_W // tn

    order = [c for pair in zip(range(2 * per), range(2 * per, 4 * per)) for c in pair]

    def produce(i):
        j = order[i]
        return jnp.dot(xb, w_ref[:, j * tn : (j + 1) * tn], preferred_element_type=F32)

    def consume(i, h):
        j = order[i]
        sl = slice(j * tn, (j + 1) * tn)
        if j < 2 * per:
            prev = carry_ref[:, sl]
            carry_ref[:, sl] = h[tm - V7X_SUBLANES :, :]
            act = _silu(_causal_dwconv(h, prev, cw_ref.at[:, sl], MLSTM_CONV))
            if j < per:
                q_ref[:, sl] = act.astype(BF16)
            else:
                kt_ref[(j - per) * tn : (j - per + 1) * tn, :] = (act * (MLSTM_HEAD_DIM ** -0.5)).T.astype(BF16)
        elif j < 3 * per:
            v_ref[:, (j - 2 * per) * tn : (j - 2 * per + 1) * tn] = h.astype(BF16)
        else:
            o_ref[:, (j - 3 * per) * tn : (j - 3 * per + 1) * tn] = jax.nn.sigmoid(h)

    _lookahead(4 * per, produce, consume)
    gates = jnp.dot(xb, wg_ref[...], preferred_element_type=F32) + gb_ref[...]
    lane = lax.broadcasted_iota(jnp.int32, (1, V7X_LANES), 1)
    gt_ref[...] = jnp.where(lane < MLSTM_HEADS, gates, _log_sigmoid(gates))


def _odd_in_proj(x2d, seq_len, w, wg, gb, conv_w):
    T, K = x2d.shape
    tm = min(ROW_TILE, seq_len)
    W = MLSTM_W
    cw = jnp.zeros((V7X_SUBLANES, 2 * W), F32).at[:MLSTM_CONV].set(conv_w)
    row = lambda i: (i, 0)
    shapes = ((T, W), (W, T), (T, W), (T, W), (T, V7X_LANES))
    dtypes = (BF16, BF16, BF16, F32, F32)
    return pl.pallas_call(
        functools.partial(_odd_in_proj_kernel, tiles_per_seq=seq_len // tm),
        grid=(T // tm,),
        in_specs=[pl.BlockSpec((tm, K), row), _resident(w.shape), _resident(wg.shape), _resident(gb.shape),
                  _resident(cw.shape)],
        out_specs=[pl.BlockSpec((tm, W), row), pl.BlockSpec((W, tm), lambda i: (0, i)), pl.BlockSpec((tm, W), row),
                   pl.BlockSpec((tm, W), row), pl.BlockSpec((tm, V7X_LANES), row)],
        out_shape=[jax.ShapeDtypeStruct(s, dt) for s, dt in zip(shapes, dtypes)],
        scratch_shapes=[pltpu.VMEM((V7X_SUBLANES, 2 * MLSTM_W), F32)],
        compiler_params=_params(("arbitrary",)),
        name="odd_in_proj",
    )(x2d, w, wg, gb, cw)


def _tail_kernel(*refs, n_y, tiles_per_seq):
    x_ref = refs[0]
    y_refs = refs[1 : 1 + n_y]
    (wout_ref, g1_ref, b1_ref, wup_ref, cw_ref, cb_ref, wdn_ref, g_ref, b_ref,
     o_ref, carry_ref, act_ref, x1_ref) = refs[1 + n_y :]

    @pl.when(pl.program_id(0) % tiles_per_seq == 0)
    def _():
        carry_ref[...] = jnp.zeros_like(carry_ref)

    tm = x_ref.shape[0]
    for lo in range(0, tm, tm // 2):
        rows = slice(lo, lo + tm // 2)
        acc = None
        row = 0
        for y_ref in y_refs:
            kw = y_ref.shape[1]
            part = jnp.dot(y_ref[rows, :], wout_ref[row : row + kw, :], preferred_element_type=F32)
            acc = part if acc is None else acc + part
            row += kw
        x1_ref[rows, :] = _layer_norm(DEEPNORM_ALPHA * x_ref[rows, :] + acc, g1_ref[...], b1_ref[...])

    xb = x1_ref[...].astype(BF16)

    def up_dots(j):
        lo = j * FFN_CHUNK
        return (jnp.dot(xb, wup_ref[:, lo : lo + FFN_CHUNK], preferred_element_type=F32),
                jnp.dot(xb, wup_ref[:, D_FF + lo : D_FF + lo + FFN_CHUNK], preferred_element_type=F32))

    def activate(j, gate_up):
        gate, up = gate_up
        sl = slice(j * FFN_CHUNK, (j + 1) * FFN_CHUNK)
        prev = carry_ref[:, sl]
        carry_ref[:, sl] = gate[tm - V7X_SUBLANES :, :]
        conv = _causal_dwconv(gate, prev, cw_ref.at[:, sl], FFN_CONV) + cb_ref[:, sl]
        act_ref[:, sl] = (_silu(conv) * up).astype(BF16)

    _lookahead(D_FF // FFN_CHUNK, up_dots, activate)
    for lo in range(0, tm, tm // 2):
        rows = slice(lo, lo + tm // 2)
        ffn = jnp.dot(act_ref[rows, :], wdn_ref[...], preferred_element_type=F32)
        z = DEEPNORM_ALPHA * x1_ref[rows, :] + ffn
        o_ref[rows, :] = _layer_norm(z, g_ref[...], b_ref[...])


def _layer_tail(x2d, ys, seq_len, layer, w_out, ln1, w_up, conv_w, conv_b, w_down, ln2):
    T, D = x2d.shape
    tm = min(ROW_TILE, seq_len)
    cw = jnp.zeros((V7X_SUBLANES, D_FF), F32).at[:FFN_CONV].set(conv_w)
    row = lambda i: (i, 0)
    vec = lambda v: v.reshape(1, -1)
    return pl.pallas_call(
        functools.partial(_tail_kernel, n_y=len(ys), tiles_per_seq=seq_len // tm),
        grid=(T // tm,),
        in_specs=[pl.BlockSpec((tm, D), row)]
        + [pl.BlockSpec((tm, y.shape[1]), row) for y in ys]
        + [_resident(w_out.shape), _resident((1, D)), _resident((1, D)),
           _resident_layer(w_up.shape, layer), _resident(cw.shape), _resident((1, D_FF)),
           _resident_layer(w_down.shape, layer), _resident((1, D)), _resident((1, D))],
        out_specs=pl.BlockSpec((tm, D), row),
        out_shape=jax.ShapeDtypeStruct((T, D), F32),
        scratch_shapes=[pltpu.VMEM((V7X_SUBLANES, D_FF), F32), pltpu.VMEM((tm, D_FF), BF16), pltpu.VMEM((tm, D), F32)],
        compiler_params=_params(("arbitrary",)),
        name="layer_tail",
    )(x2d, *ys, w_out, vec(ln1[0]), vec(ln1[1]), w_up, cw, vec(conv_b), w_down, vec(ln2[0]), vec(ln2[1]))


def _retention_tables(seq_len):
    H, C, d = RET_HEADS, RET_CHUNK, RET_HEAD_DIM
    scale = d ** -0.5
    log_gamma = jnp.log1p(-jnp.exp2(-5.0 - jnp.arange(H, dtype=F32)))
    idx = jnp.arange(C, dtype=F32)
    rel = idx[:, None] - idx[None, :]
    decay = jnp.where(rel >= 0, jnp.exp(log_gamma[:, None, None] * jnp.maximum(rel, 0.0)), 0.0) * scale
    k_w = jnp.exp(log_gamma[:, None] * (C - 1 - idx)[None, :]) * scale
    q_w = jnp.exp(log_gamma[:, None] * (idx + 1.0)[None, :])
    kw_t = jnp.broadcast_to(k_w[:, :, None], (H, C, d))
    qw_t = jnp.broadcast_to(q_w[:, :, None], (H, C, d))
    cd_t = jnp.broadcast_to(jnp.exp(log_gamma * C)[:, None, None], (H, 1, d))
    inv = 1.0 / (10000.0 ** (jnp.arange(0, d, 2, dtype=F32) / d))
    ang = jnp.arange(seq_len, dtype=F32)[:, None] * inv[None, :]
    cos, sin = jnp.cos(ang), jnp.sin(ang)
    cos_t = jnp.concatenate([cos, cos], -1)
    sin_t = jnp.concatenate([-sin, sin], -1)
    return cos_t, sin_t, decay, kw_t, qw_t, cd_t


def _retention_kernel(q_ref, kt_ref, v_ref, g_ref, dec_ref, kw_ref, qw_ref, cd_ref, o_ref, r_ref):
    @pl.when(pl.program_id(2) == 0)
    def _():
        r_ref[...] = jnp.zeros_like(r_ref)

    C = RET_CHUNK
    decay = dec_ref[0]
    kw = kw_ref[0]
    qw = qw_ref[0]
    cd = cd_ref[0]

    def local(n):
        sl = slice(n * C, (n + 1) * C)
        qb = q_ref[0, sl, :]
        kt = kt_ref[:, sl]
        vb = v_ref[0, sl, :]
        scores = jnp.dot(qb, kt, preferred_element_type=F32) * decay
        y_intra = jnp.dot(scores.astype(BF16), vb, preferred_element_type=F32)
        kv = jnp.dot(kt, (vb.astype(F32) * kw).astype(BF16), preferred_element_type=F32)
        q_dec = (qb.astype(F32) * qw).astype(BF16)
        return y_intra, kv, q_dec

    def recurrent(n, staged):
        y_intra, kv, q_dec = staged
        sl = slice(n * C, (n + 1) * C)
        r = r_ref[...]
        y = y_intra + jnp.dot(q_dec, r.astype(BF16), preferred_element_type=F32)
        r_ref[...] = cd * r + kv
        mu = jnp.mean(y, -1, keepdims=True)
        yc = y - mu
        var = jnp.mean(yc * yc, -1, keepdims=True)
        o_ref[0, sl, :] = (yc * lax.rsqrt(var + LN_EPS) * g_ref[0, sl, :]).astype(o_ref.dtype)

    _lookahead(q_ref.shape[1] // C, local, recurrent)


def _retention(q3, kt, v3, g3, tables):
    B, S, _ = v3.shape
    H, d, C = RET_HEADS, RET_HEAD_DIM, RET_CHUNK
    ts = min(SEQ_TILE, S)
    tiles = S // ts
    _, _, decay, kw_t, qw_t, cd_t = tables
    col = pl.BlockSpec((1, ts, d), lambda b, h, c: (b, c, h))
    per_head = lambda shape: pl.BlockSpec((1,) + shape, lambda b, h, c: (h, 0, 0))
    return pl.pallas_call(
        _retention_kernel,
        grid=(B, H, tiles),
        in_specs=[col, pl.BlockSpec((d, ts), lambda b, h, c: (h, b * tiles + c)), col, col,
                  per_head((C, C)), per_head((C, d)), per_head((C, d)), per_head((1, d))],
        out_specs=pl.BlockSpec((1, ts, d), lambda b, h, c: (b, c, h)),
        out_shape=jax.ShapeDtypeStruct((B, S, H * d), BF16),
        scratch_shapes=[pltpu.VMEM((d, d), F32)],
        compiler_params=_params(("parallel", "parallel", "arbitrary")),
        name="retention",
    )(q3, kt, v3, g3, decay, kw_t, qw_t, cd_t)


def _t5_bucket(dist):
    exact = T5_BUCKETS // 2
    n = jnp.maximum(dist, 0)
    large = exact + (jnp.log(jnp.maximum(n, 1).astype(F32) / exact) / math.log(T5_MAX_DIST / exact) * (T5_BUCKETS - exact)).astype(jnp.int32)
    large = jnp.minimum(large, T5_BUCKETS - 1)
    return jnp.where(n < exact, n, large)


def _dsa_bucket_tables():
    blk = DSA_BLOCK
    qi = jnp.arange(blk)[:, None]
    kj = jnp.arange(2 * blk)[None, :]
    tabs = []
    for window, dilation in DSA_PATTERNS:
        assert window // dilation <= blk
        for offset in (blk, 0):
            dist = qi + offset - kj
            valid = (dist >= 0) & (dist <= window // dilation)
            tabs.append(jnp.where(valid, _t5_bucket(dist * dilation), -1))
    return jnp.stack(tabs).reshape(len(DSA_PATTERNS), 2, blk, 2 * blk).astype(jnp.int32)


def _dsa_bias_kernel(rb_ref, bk_ref, o_ref):
    for v in range(2):
        bk = bk_ref[0, v]
        for h in range(DSA_HEADS):
            acc = jnp.full(bk.shape, -jnp.inf, F32)
            for b in range(T5_BUCKETS):
                acc = jnp.where(bk == b, rb_ref[b, h], acc)
            o_ref[0, v, h] = acc


def _dsa_bias(rel_bias):
    buckets = _dsa_bucket_tables()
    P, blk = len(DSA_PATTERNS), DSA_BLOCK
    return pl.pallas_call(
        _dsa_bias_kernel,
        grid=(P,),
        in_specs=[pl.BlockSpec(memory_space=pltpu.SMEM), pl.BlockSpec((1, 2, blk, 2 * blk), lambda p: (p, 0, 0, 0))],
        out_specs=pl.BlockSpec((1, 2, DSA_HEADS, blk, 2 * blk), lambda p: (p, 0, 0, 0, 0)),
        out_shape=jax.ShapeDtypeStruct((P, 2, DSA_HEADS, blk, 2 * blk), F32),
        compiler_params=_params(("parallel",)),
        name="dsa_bias",
    )(rel_bias, buckets)


def _dsa_kernel(*refs):
    P = len(DSA_PATTERNS)
    q_refs, k_refs, v_refs = refs[:P], refs[P : 2 * P], refs[2 * P : 3 * P]
    bias_ref, o_ref, acc_o, acc_m, acc_l = refs[3 * P :]
    blk = DSA_BLOCK
    tq = o_ref.shape[1]
    t = pl.program_id(2)
    lane = lax.broadcasted_iota(jnp.int32, (1, V7X_LANES), 1)
    head0 = lane < DSA_HEAD_DIM

    def block_logits(p, r, idx):
        blocks_per_residue = tq // (blk * r)
        rho = idx // blocks_per_residue
        nb = idx % blocks_per_residue
        l0 = pl.multiple_of(t * (tq // r) + nb * blk, blk)
        first = l0 == 0
        k_row = pl.multiple_of(jnp.where(first, l0, l0 - blk), blk)
        qb = q_refs[p][0, rho, pl.ds(l0, blk), :]
        zero = jnp.zeros_like(qb)
        q2 = jnp.concatenate([jnp.where(head0, qb, zero), jnp.where(head0, zero, qb)], axis=0)
        bias = bias_ref[p, pl.ds(first.astype(jnp.int32), 1), 0][0]
        kb = k_refs[p][0, rho, pl.ds(k_row, 2 * blk), :]
        logits = lax.dot_general(q2, kb, _NT, preferred_element_type=F32) + bias
        start = rho + nb * (blk * r)
        rows = pl.ds(start, blk, stride=r) if r > 1 else pl.ds(start, blk)
        return logits, (rho, k_row), rows

    def block_output(p, logits, key_rows, rows):
        rho, k_row = key_rows
        m = jnp.max(logits, -1, keepdims=True)
        e = jnp.exp(logits - m).astype(BF16)
        vb = v_refs[p][0, rho, pl.ds(k_row, 2 * blk), :]
        o2 = jnp.dot(e, jnp.concatenate([vb, jnp.ones_like(vb)], axis=1), preferred_element_type=F32)
        acc_o.at[p][rows, :] = jnp.where(head0, o2[:blk, :V7X_LANES], o2[blk:, :V7X_LANES])
        acc_m.at[p][rows, :] = jnp.where(head0, m[:blk], m[blk:])
        acc_l.at[p][rows, :] = jnp.where(head0, o2[:blk, V7X_LANES:], o2[blk:, V7X_LANES:])

    blocks = [(p, r, idx) for p, (_, r) in enumerate(DSA_PATTERNS) for idx in range(tq // blk)]
    staged = [block_logits(*blocks[i]) for i in range(DSA_GROUP)]
    for i, (p, _, _) in enumerate(blocks):
        if i + DSA_GROUP < len(blocks):
            staged.append(block_logits(*blocks[i + DSA_GROUP]))
        block_output(p, *staged[i])

    for c in range(tq // blk):
        sl = slice(c * blk, (c + 1) * blk)
        ms = [acc_m[bi, sl, :] for bi in range(len(DSA_PATTERNS))]
        m_all = functools.reduce(jnp.maximum, ms)
        num = den = None
        for bi, m in enumerate(ms):
            w = jnp.exp(m - m_all)
            n_i = w * acc_o[bi, sl, :]
            d_i = w * acc_l[bi, sl, :]
            num = n_i if num is None else num + n_i
            den = d_i if den is None else den + d_i
        o_ref[0, sl, :] = (num / den).astype(o_ref.dtype)


def _dilated_attention(ds, bias):
    B, _, S, _ = ds[0].shape
    pairs = DSA_W // V7X_LANES
    tq = DSA_TILE
    assert S % tq == 0 and S >= 2 * tq
    P, blk = len(DSA_PATTERNS), DSA_BLOCK
    bias5 = bias.reshape(P, 2, pairs, 2 * blk, 2 * blk)

    def seq(r, off):
        return pl.BlockSpec((1, r, S // r, V7X_LANES), lambda b, p, t: (b, 0, 0, off + p))

    return pl.pallas_call(
        _dsa_kernel,
        grid=(B, pairs, S // tq),
        in_specs=[seq(r, off) for off in (0, pairs, 2 * pairs) for _, r in DSA_PATTERNS]
        + [pl.BlockSpec((P, 2, 1, 2 * blk, 2 * blk), lambda b, p, t: (0, 0, p, 0, 0))],
        out_specs=pl.BlockSpec((1, tq, V7X_LANES), lambda b, p, t: (b, t, p)),
        out_shape=jax.ShapeDtypeStruct((B, S, DSA_W), BF16),
        scratch_shapes=[pltpu.VMEM((P, tq, V7X_LANES), F32)] * 3,
        compiler_params=_params(("parallel", "parallel", "arbitrary")),
        name="dilated_attention",
    )(*ds, *ds, *ds, bias5)


def _mlstm_kernel(q_ref, kt_ref, v_ref, og_ref, gr_ref, o_ref, c_ref, m_ref):
    @pl.when(pl.program_id(2) == 0)
    def _():
        c_ref[...] = jnp.zeros_like(c_ref)
        m_ref[...] = jnp.zeros_like(m_ref)

    C = MLSTM_CHUNK
    dv = MLSTM_HEAD_DIM
    L = V7X_LANES
    n_chunks = q_ref.shape[1] // C
    ri = lax.broadcasted_iota(jnp.int32, (C, C), 0)
    ci = lax.broadcasted_iota(jnp.int32, (C, C), 1)
    causal = ci <= ri
    eye = ci == ri
    rep = lambda col: jnp.broadcast_to(col, (C, L))

    def gate_vectors(n):
        sl = slice(n * C, (n + 1) * C)
        li_row = gr_ref[0, 0, 0:1, sl]
        lf_row = gr_ref[0, 0, 1:2, sl]
        li_col = jnp.sum(jnp.where(eye, li_row, 0.0), -1, keepdims=True)
        a_col = jnp.sum(jnp.where(causal, lf_row, 0.0), -1, keepdims=True)
        g_tot = jnp.sum(lf_row, -1, keepdims=True)
        return li_row, li_col, a_col, g_tot

    def decay_matrix(gv):
        li_row, li_col, a_col, g_tot = gv
        a_rep = rep(a_col)
        a_row = jnp.sum(jnp.where(eye, a_rep, 0.0), 0, keepdims=True)
        w_state = g_tot - a_col + li_col
        m_loc = jnp.max(w_state, 0, keepdims=True)
        ew_rep = rep(jnp.exp(w_state - m_loc))
        dmat = jnp.where(causal, a_rep - a_row + li_row, -jnp.inf)
        d_max = jnp.max(dmat, -1, keepdims=True)
        return a_rep, g_tot, m_loc, ew_rep, dmat, d_max

    def local_matmuls(n, dm):
        a_rep, g_tot, m_loc, ew_rep, dmat, d_max = dm
        sl = slice(n * C, (n + 1) * C)
        qb = q_ref[0, sl, :]
        kt = kt_ref[:, sl]
        vb = v_ref[0, sl, :]
        d_rep = rep(d_max)
        p = (jnp.exp(dmat - d_rep) * jnp.dot(qb, kt, preferred_element_type=F32)).astype(BF16)
        pv = jnp.dot(p, jnp.concatenate([vb, jnp.ones((C, L), BF16)], axis=1), preferred_element_type=F32)
        vw = jnp.concatenate([vb.astype(F32) * jnp.concatenate([ew_rep] * (dv // L), axis=1), ew_rep], axis=1)
        kv_ext = jnp.dot(kt, vw.astype(BF16), preferred_element_type=F32)
        return qb, pv, kv_ext, a_rep, d_rep, g_tot, m_loc

    def recurrent(n, staged):
        qb, pv, kv_ext, a_rep, d_rep, g_tot, m_loc = staged
        sl = slice(n * C, (n + 1) * C)
        c_s = c_ref[...]
        m_s = m_ref[...][:, 0:1]
        inter = jnp.dot(qb, c_s.astype(BF16), preferred_element_type=F32)
        inter_log = a_rep + m_s
        m_row = jnp.maximum(inter_log, d_rep)
        f_intra = jnp.exp(d_rep - m_row)
        f_inter = jnp.exp(inter_log - m_row)
        den = f_intra * pv[:, dv:] + f_inter * inter[:, dv:]
        r = 1.0 / jnp.maximum(jnp.abs(den), jnp.exp(-m_row))
        for lo in range(0, dv, L):
            num = f_intra * pv[:, lo : lo + L] + f_inter * inter[:, lo : lo + L]
            o_ref[0, sl, lo : lo + L] = (num * r * og_ref[0, sl, lo : lo + L]).astype(o_ref.dtype)
        m_new = jnp.maximum(g_tot + m_s, m_loc)
        a_old = jnp.exp(g_tot + m_s - m_new)
        a_new = jnp.exp(m_loc - m_new)
        c_ref[...] = a_old * c_s + a_new * kv_ext
        m_ref[...] = jnp.broadcast_to(m_new, m_ref.shape)

    gvs = [gate_vectors(n) for n in range(n_chunks)]
    dms = [decay_matrix(gv) for gv in gvs]
    staged = [local_matmuls(n, dm) for n, dm in enumerate(dms)]
    for n in range(n_chunks):
        recurrent(n, staged[n])


def _mlstm(q3, kt, v3, o3, gates_row):
    B, S, _ = v3.shape
    H, d = MLSTM_HEADS, MLSTM_HEAD_DIM
    ts = min(SEQ_TILE, S)
    tiles = S // ts
    col = pl.BlockSpec((1, ts, d), lambda b, h, c: (b, c, h))
    return pl.pallas_call(
        _mlstm_kernel,
        grid=(B, H, tiles),
        in_specs=[col, pl.BlockSpec((d, ts), lambda b, h, c: (h, b * tiles + c)), col, col,
                  pl.BlockSpec((1, 1, 2, ts), lambda b, h, c: (b, h, 0, c))],
        out_specs=pl.BlockSpec((1, ts, d), lambda b, h, c: (b, c, h)),
        out_shape=jax.ShapeDtypeStruct((B, S, H * d), BF16),
        scratch_shapes=[pltpu.VMEM((d, d + V7X_LANES), F32), pltpu.VMEM((1, V7X_LANES), F32)],
        compiler_params=_params(("parallel", "parallel", "arbitrary")),
        name="mlstm",
    )(q3, kt, v3, o3, gates_row)


def _even_mixer(x2d, B, S, w_in, rel_bias):
    tables = _retention_tables(S)
    q, kt, v, gate, d_nat, *d_dil = _even_in_proj(x2d, S, w_in.astype(BF16), tables[0], tables[1])
    y_r = _retention(q.reshape(B, S, -1), kt, v.reshape(B, S, -1), gate.reshape(B, S, -1), tables)
    y_d = _dilated_attention([d_nat.reshape(B, 1, S, -1)] + d_dil, _dsa_bias(rel_bias))
    return [y_r.reshape(B * S, RET_W), y_d.reshape(B * S, DSA_W)]


def _odd_mixer(x2d, B, S, w_in, gate_b, conv_w):
    H = MLSTM_HEADS
    wide = 4 * MLSTM_W
    wg = jnp.zeros((D_MODEL, V7X_LANES), F32).at[:, : 2 * H].set(w_in[:, wide:]).astype(BF16)
    gb = jnp.zeros((1, V7X_LANES), F32).at[0, : 2 * H].set(gate_b)
    q, kt, v, og, gates = _odd_in_proj(x2d, S, w_in.astype(BF16), wg, gb, conv_w)
    gates_row = gates[:, : 2 * H].reshape(B, S, 2, H).transpose(0, 3, 2, 1)
    y = _mlstm(q.reshape(B, S, -1), kt, v.reshape(B, S, -1), og.reshape(B, S, -1), gates_row)
    return [y.reshape(B * S, MLSTM_W)]


def kernel(x, even_w_in, even_w_out, rel_bias, odd_w_in, odd_gate_b, odd_conv_w, odd_w_out, ffn_w_up, ffn_conv_w, ffn_conv_b, ffn_w_down, ln_g, ln_b):
    B, S, D = x.shape
    x2d = x.reshape(B * S, D)
    w_up, w_down = ffn_w_up.astype(BF16), ffn_w_down.astype(BF16)
    for layer in range(DEPTH):
        j = layer // 2
        if layer % 2 == 0:
            ys, w_out = _even_mixer(x2d, B, S, even_w_in[j], rel_bias), even_w_out[j]
        else:
            ys, w_out = _odd_mixer(x2d, B, S, odd_w_in[j], odd_gate_b[j], odd_conv_w[j]), odd_w_out[j]
        x2d = _layer_tail(x2d, ys, S, layer, w_out.astype(BF16), (ln_g[layer, 0], ln_b[layer, 0]), w_up,
                          ffn_conv_w[layer], ffn_conv_b[layer], w_down, (ln_g[layer, 1], ln_b[layer, 1]))
    return x2d.reshape(B, S, D)
```

```python
import functools
import math

import jax
import jax.numpy as jnp
from jax import lax
from jax.experimental import pallas as pl
from jax.experimental.pallas import tpu as pltpu

F32 = jnp.float32
BF16 = jnp.bfloat16

D_MODEL = 1024
DEPTH = 2
RET_HEADS = 4
RET_HEAD_DIM = 128
RET_CHUNK = 128
DSA_HEADS = 8
DSA_HEAD_DIM = 64
DSA_PATTERNS = ((128, 1), (512, 4), (2048, 16))
DSA_BLOCK = 128
T5_BUCKETS = 32
T5_MAX_DIST = 2048
MLSTM_HEADS = 4
MLSTM_HEAD_DIM = 256
MLSTM_CHUNK = 128
MLSTM_CONV = 4
D_FF = 2816
FFN_CONV = 3
LN_EPS = 1e-5
DEEPNORM_ALPHA = (2.0 * DEPTH) ** 0.25

RET_W = RET_HEADS * RET_HEAD_DIM
DSA_W = DSA_HEADS * DSA_HEAD_DIM
MLSTM_W = MLSTM_HEADS * MLSTM_HEAD_DIM

V7X_LANES = 128
V7X_SUBLANES = 8
V7X_VMEM_BYTES = 64 * 1024 * 1024
VMEM_LIMIT_BYTES = V7X_VMEM_BYTES - 8 * 1024 * 1024

ROW_TILE = 512
PROJ_CHUNK = 512
FFN_CHUNK = 256
SEQ_TILE = 4096
DSA_TILE = DSA_BLOCK * DSA_PATTERNS[-1][1]
DSA_GROUP = 2

_NT = (((1,), (1,)), ((), ()))
_TN = (((0,), (0,)), ((), ()))


def _params(semantics):
    return pltpu.CompilerParams(dimension_semantics=semantics, vmem_limit_bytes=VMEM_LIMIT_BYTES)


def _resident(shape):
    nd = len(shape)
    return pl.BlockSpec(shape, lambda *_: (0,) * nd, pipeline_mode=pl.Buffered(1))


def _resident_layer(stacked_shape, layer):
    return pl.BlockSpec((None,) + tuple(stacked_shape[1:]), lambda *_: (layer, 0, 0), pipeline_mode=pl.Buffered(1))


def _layer_norm(z, g, b):
    mu = jnp.mean(z, -1, keepdims=True)
    zc = z - mu
    var = jnp.mean(zc * zc, -1, keepdims=True)
    return zc * lax.rsqrt(var + LN_EPS) * g + b


def _silu(z):
    return z * jax.nn.sigmoid(z)


def _log_sigmoid(x):
    return jnp.minimum(x, 0.0) - jnp.log1p(jnp.exp(-jnp.abs(x)))


def _lookahead(n, produce, consume, depth=1):
    staged = [produce(j) for j in range(min(depth, n))]
    for j in range(n):
        if j + depth < n:
            staged.append(produce(j + depth))
        consume(j, staged[j])
        staged[j] = None


def _shift_rows(x, prev, s):
    if s == 0:
        return x
    rolled = pltpu.roll(x, s, 0)
    head = pltpu.roll(prev, s, 0)
    rid = lax.broadcasted_iota(jnp.int32, (V7X_SUBLANES, x.shape[1]), 0)
    first = jnp.where(rid < s, head, rolled[:V7X_SUBLANES])
    return jnp.concatenate([first, rolled[V7X_SUBLANES:]], axis=0)


def _causal_dwconv(x, prev, w_ref, taps):
    acc = None
    for k in range(taps):
        term = _shift_rows(x, prev, taps - 1 - k) * w_ref[k : k + 1, :]
        acc = term if acc is None else acc + term
    return acc


def _even_in_proj_kernel(x_ref, w_ref, cos_ref, sin_ref, q_ref, kt_ref, v_ref, g_ref, *rest):
    d_refs, stage_ref = rest[:-1], rest[-1]
    xb = x_ref[...].astype(BF16)
    tm = xb.shape[0]
    cos = cos_ref[...]
    sin = sin_ref[...]
    tn, d = RET_W, RET_HEAD_DIM

    def consume(j, h):
        if j < 2:
            for hh in range(RET_HEADS):
                c = h[:, hh * d : (hh + 1) * d]
                rot = c * cos + pltpu.roll(c, d // 2, 1) * sin
                if j == 0:
                    q_ref[:, hh * d : (hh + 1) * d] = rot.astype(BF16)
                else:
                    kt_ref[hh * d : (hh + 1) * d, :] = rot.T.astype(BF16)
        elif j == 2:
            v_ref[...] = h.astype(BF16)
        elif j == 3:
            g_ref[...] = _silu(h)
        else:
            if j == 4:
                h = h * (DSA_HEAD_DIM ** -0.5)
            lo = (j - 4) * tn
            d_refs[0][:, lo : lo + tn] = h.astype(BF16)
            src, dst = stage_ref.at[0], stage_ref.at[1]
            for s in range(tn // V7X_LANES):
                src[s] = h[:, s * V7X_LANES : (s + 1) * V7X_LANES]
            prev_r = 1
            for d_ref, (_, r) in zip(d_refs[1:], DSA_PATTERNS[1:]):
                step = r // prev_r
                n_rows = tm // r
                for rho_prev in range(prev_r):
                    for a in range(step):
                        rho = rho_prev + prev_r * a
                        for s in range(tn // V7X_LANES):
                            rows = src.at[s][pl.ds(rho_prev * (tm // prev_r) + a, n_rows, stride=step), :]
                            if r != DSA_PATTERNS[-1][1]:
                                dst[s, rho * n_rows : (rho + 1) * n_rows, :] = rows
                            d_ref[0, rho, :, lo + s * V7X_LANES : lo + (s + 1) * V7X_LANES] = rows.astype(BF16)
                src, dst, prev_r = dst, src, r

    _lookahead(w_ref.shape[1] // tn, lambda j: jnp.dot(xb, w_ref[:, j * tn : (j + 1) * tn], preferred_element_type=F32), consume)


def _even_in_proj(x2d, seq_len, w, cos_t, sin_t):
    T, K = x2d.shape
    tm = min(ROW_TILE, seq_len)
    tps = seq_len // tm
    B = T // seq_len
    row = lambda i: (i, 0)
    pos = pl.BlockSpec((tm, RET_HEAD_DIM), lambda i: (i % tps, 0))
    shapes = [(T, RET_W), (RET_W, T), (T, RET_W), (T, RET_W), (T, 3 * DSA_W)]
    dtypes = [BF16, BF16, BF16, F32, BF16]
    specs = [pl.BlockSpec((tm, RET_W), row), pl.BlockSpec((RET_W, tm), lambda i: (0, i)),
             pl.BlockSpec((tm, RET_W), row), pl.BlockSpec((tm, RET_W), row), pl.BlockSpec((tm, 3 * DSA_W), row)]
    for _, r in DSA_PATTERNS[1:]:
        shapes.append((B, r, seq_len // r, 3 * DSA_W))
        dtypes.append(BF16)
        specs.append(pl.BlockSpec((1, r, tm // r, 3 * DSA_W), lambda i: (i // tps, 0, i % tps, 0)))
    return pl.pallas_call(
        _even_in_proj_kernel,
        grid=(T // tm,),
        in_specs=[pl.BlockSpec((tm, K), row), _resident(w.shape), pos, pos],
        out_specs=specs,
        out_shape=[jax.ShapeDtypeStruct(s, dt) for s, dt in zip(shapes, dtypes)],
        scratch_shapes=[pltpu.VMEM((2, RET_W // V7X_LANES, tm, V7X_LANES), F32)],
        compiler_params=_params(("parallel",)),
        name="even_in_proj",
    )(x2d, w, cos_t, sin_t)


def _odd_in_proj_kernel(x_ref, w_ref, wg_ref, gb_ref, cw_ref, q_ref, kt_ref, v_ref, o_ref, gt_ref, carry_ref, *, tiles_per_seq):
    @pl.when(pl.program_id(0) % tiles_per_seq == 0)
    def _():
        carry_ref[...] = jnp.zeros_like(carry_ref)

    xb = x_ref[...].astype(BF16)
    tm = xb.shape[0]
    tn = PROJ_CHUNK
    per = MLSTM_W // tn

    order = [c for pair in zip(range(2 * per), range(2 * per, 4 * per)) for c in pair]

    def produce(i):
        j = order[i]
        return jnp.dot(xb, w_ref[:, j * tn : (j + 1) * tn], preferred_element_type=F32)

    def consume(i, h):
        j = order[i]
        sl = slice(j * tn, (j + 1) * tn)
        if j < 2 * per:
            prev = carry_ref[:, sl]
            carry_ref[:, sl] = h[tm - V7X_SUBLANES :, :]
            act = _silu(_causal_dwconv(h, prev, cw_ref.at[:, sl], MLSTM_CONV))
            if j < per:
                q_ref[:, sl] = act.astype(BF16)
            else:
                kt_ref[(j - per) * tn : (j - per + 1) * tn, :] = (act * (MLSTM_HEAD_DIM ** -0.5)).T.astype(BF16)
        elif j < 3 * per:
            v_ref[:, (j - 2 * per) * tn : (j - 2 * per + 1) * tn] = h.astype(BF16)
        else:
            o_ref[:, (j - 3 * per) * tn : (j - 3 * per + 1) * tn] = jax.nn.sigmoid(h)

    _lookahead(4 * per, produce, consume)
    gates = (jnp.dot(xb, wg_ref[...], preferred_element_type=F32) + gb_ref[...]).T[: 2 * MLSTM_HEADS, :]
    gate_id = lax.broadcasted_iota(jnp.int32, (2 * MLSTM_HEADS, 1), 0)
    gt_ref[...] = jnp.where(gate_id < MLSTM_HEADS, gates, _log_sigmoid(gates))


def _odd_in_proj(x2d, seq_len, w, wg, gb, conv_w):
    T, K = x2d.shape
    tm = min(ROW_TILE, seq_len)
    W = MLSTM_W
    cw = jnp.zeros((V7X_SUBLANES, 2 * W), F32).at[:MLSTM_CONV].set(conv_w)
    row = lambda i: (i, 0)
    shapes = ((T, W), (W, T), (T, W), (T, W), (2 * MLSTM_HEADS, T))
    dtypes = (BF16, BF16, BF16, F32, F32)
    return pl.pallas_call(
        functools.partial(_odd_in_proj_kernel, tiles_per_seq=seq_len // tm),
        grid=(T // tm,),
        in_specs=[pl.BlockSpec((tm, K), row), _resident(w.shape), _resident(wg.shape), _resident(gb.shape),
                  _resident(cw.shape)],
        out_specs=[pl.BlockSpec((tm, W), row), pl.BlockSpec((W, tm), lambda i: (0, i)), pl.BlockSpec((tm, W), row),
                   pl.BlockSpec((tm, W), row), pl.BlockSpec((2 * MLSTM_HEADS, tm), lambda i: (0, i))],
        out_shape=[jax.ShapeDtypeStruct(s, dt) for s, dt in zip(shapes, dtypes)],
        scratch_shapes=[pltpu.VMEM((V7X_SUBLANES, 2 * MLSTM_W), F32)],
        compiler_params=_params(("arbitrary",)),
        name="odd_in_proj",
    )(x2d, w, wg, gb, cw)


def _tail_kernel(*refs, n_y, tiles_per_seq):
    x_ref = refs[0]
    y_refs = refs[1 : 1 + n_y]
    (wout_ref, g1_ref, b1_ref, wup_ref, cw_ref, cb_ref, wdn_ref, g_ref, b_ref,
     o_ref, carry_ref, act_ref, x1_ref) = refs[1 + n_y :]

    @pl.when(pl.program_id(0) % tiles_per_seq == 0)
    def _():
        carry_ref[...] = jnp.zeros_like(carry_ref)

    tm = x_ref.shape[0]
    for lo in range(0, tm, tm // 2):
        rows = slice(lo, lo + tm // 2)
        acc = None
        row = 0
        for y_ref in y_refs:
            kw = y_ref.shape[1]
            part = jnp.dot(y_ref[rows, :], wout_ref[row : row + kw, :], preferred_element_type=F32)
            acc = part if acc is None else acc + part
            row += kw
        x1_ref[rows, :] = _layer_norm(DEEPNORM_ALPHA * x_ref[rows, :] + acc, g1_ref[...], b1_ref[...])

    xb = x1_ref[...].astype(BF16)

    def up_dots(j):
        lo = j * FFN_CHUNK
        return (jnp.dot(xb, wup_ref[:, lo : lo + FFN_CHUNK], preferred_element_type=F32),
                jnp.dot(xb, wup_ref[:, D_FF + lo : D_FF + lo + FFN_CHUNK], preferred_element_type=F32))

    def activate(j, gate_up):
        gate, up = gate_up
        sl = slice(j * FFN_CHUNK, (j + 1) * FFN_CHUNK)
        prev = carry_ref[:, sl]
        carry_ref[:, sl] = gate[tm - V7X_SUBLANES :, :]
        conv = _causal_dwconv(gate, prev, cw_ref.at[:, sl], FFN_CONV) + cb_ref[:, sl]
        act_ref[:, sl] = (_silu(conv) * up).astype(BF16)

    _lookahead(D_FF // FFN_CHUNK, up_dots, activate)
    for lo in range(0, tm, tm // 2):
        rows = slice(lo, lo + tm // 2)
        ffn = jnp.dot(act_ref[rows, :], wdn_ref[...], preferred_element_type=F32)
        z = DEEPNORM_ALPHA * x1_ref[rows, :] + ffn
        o_ref[rows, :] = _layer_norm(z, g_ref[...], b_ref[...])


def _layer_tail(x2d, ys, seq_len, layer, w_out, ln1, w_up, conv_w, conv_b, w_down, ln2):
    T, D = x2d.shape
    tm = min(ROW_TILE, seq_len)
    cw = jnp.zeros((V7X_SUBLANES, D_FF), F32).at[:FFN_CONV].set(conv_w)
    row = lambda i: (i, 0)
    vec = lambda v: v.reshape(1, -1)
    return pl.pallas_call(
        functools.partial(_tail_kernel, n_y=len(ys), tiles_per_seq=seq_len // tm),
        grid=(T // tm,),
        in_specs=[pl.BlockSpec((tm, D), row)]
        + [pl.BlockSpec((tm, y.shape[1]), row) for y in ys]
        + [_resident(w_out.shape), _resident((1, D)), _resident((1, D)),
           _resident_layer(w_up.shape, layer), _resident(cw.shape), _resident((1, D_FF)),
           _resident_layer(w_down.shape, layer), _resident((1, D)), _resident((1, D))],
        out_specs=pl.BlockSpec((tm, D), row),
        out_shape=jax.ShapeDtypeStruct((T, D), F32),
        scratch_shapes=[pltpu.VMEM((V7X_SUBLANES, D_FF), F32), pltpu.VMEM((tm, D_FF), BF16), pltpu.VMEM((tm, D), F32)],
        compiler_params=_params(("arbitrary",)),
        name="layer_tail",
    )(x2d, *ys, w_out, vec(ln1[0]), vec(ln1[1]), w_up, cw, vec(conv_b), w_down, vec(ln2[0]), vec(ln2[1]))


def _retention_tables(seq_len):
    H, C, d = RET_HEADS, RET_CHUNK, RET_HEAD_DIM
    scale = d ** -0.5
    log_gamma = jnp.log1p(-jnp.exp2(-5.0 - jnp.arange(H, dtype=F32)))
    idx = jnp.arange(C, dtype=F32)
    rel = idx[:, None] - idx[None, :]
    decay = jnp.where(rel >= 0, jnp.exp(log_gamma[:, None, None] * jnp.maximum(rel, 0.0)), 0.0) * scale
    k_w = jnp.exp(log_gamma[:, None] * (C - 1 - idx)[None, :]) * scale
    q_w = jnp.exp(log_gamma[:, None] * (idx + 1.0)[None, :])
    kw_t = jnp.broadcast_to(k_w[:, :, None], (H, C, d))
    qw_t = jnp.broadcast_to(q_w[:, :, None], (H, C, d))
    cd_t = jnp.broadcast_to(jnp.exp(log_gamma * C)[:, None, None], (H, 1, d))
    inv = 1.0 / (10000.0 ** (jnp.arange(0, d, 2, dtype=F32) / d))
    ang = jnp.arange(seq_len, dtype=F32)[:, None] * inv[None, :]
    cos, sin = jnp.cos(ang), jnp.sin(ang)
    cos_t = jnp.concatenate([cos, cos], -1)
    sin_t = jnp.concatenate([-sin, sin], -1)
    return cos_t, sin_t, decay, kw_t, qw_t, cd_t


def _retention_kernel(q_ref, kt_ref, v_ref, g_ref, dec_ref, kw_ref, qw_ref, cd_ref, o_ref, r_ref):
    @pl.when(pl.program_id(2) == 0)
    def _():
        r_ref[...] = jnp.zeros_like(r_ref)

    C = RET_CHUNK
    decay = dec_ref[0]
    kw = kw_ref[0]
    qw = qw_ref[0]
    cd = cd_ref[0]

    def local(n):
        sl = slice(n * C, (n + 1) * C)
        qb = q_ref[0, sl, :]
        kt = kt_ref[:, sl]
        vb = v_ref[0, sl, :]
        scores = jnp.dot(qb, kt, preferred_element_type=F32) * decay
        y_intra = jnp.dot(scores.astype(BF16), vb, preferred_element_type=F32)
        kv = jnp.dot(kt, (vb.astype(F32) * kw).astype(BF16), preferred_element_type=F32)
        q_dec = (qb.astype(F32) * qw).astype(BF16)
        return y_intra, kv, q_dec

    def recurrent(n, staged):
        y_intra, kv, q_dec = staged
        sl = slice(n * C, (n + 1) * C)
        r = r_ref[...]
        y = y_intra + jnp.dot(q_dec, r.astype(BF16), preferred_element_type=F32)
        r_ref[...] = cd * r + kv
        mu = jnp.mean(y, -1, keepdims=True)
        yc = y - mu
        var = jnp.mean(yc * yc, -1, keepdims=True)
        o_ref[0, sl, :] = (yc * lax.rsqrt(var + LN_EPS) * g_ref[0, sl, :]).astype(o_ref.dtype)

    _lookahead(q_ref.shape[1] // C, local, recurrent)


def _retention(q3, kt, v3, g3, tables):
    B, S, _ = v3.shape
    H, d, C = RET_HEADS, RET_HEAD_DIM, RET_CHUNK
    ts = min(SEQ_TILE, S)
    tiles = S // ts
    _, _, decay, kw_t, qw_t, cd_t = tables
    col = pl.BlockSpec((1, ts, d), lambda b, h, c: (b, c, h))
    per_head = lambda shape: pl.BlockSpec((1,) + shape, lambda b, h, c: (h, 0, 0))
    return pl.pallas_call(
        _retention_kernel,
        grid=(B, H, tiles),
        in_specs=[col, pl.BlockSpec((d, ts), lambda b, h, c: (h, b * tiles + c)), col, col,
                  per_head((C, C)), per_head((C, d)), per_head((C, d)), per_head((1, d))],
        out_specs=pl.BlockSpec((1, ts, d), lambda b, h, c: (b, c, h)),
        out_shape=jax.ShapeDtypeStruct((B, S, H * d), BF16),
        scratch_shapes=[pltpu.VMEM((d, d), F32)],
        compiler_params=_params(("parallel", "parallel", "arbitrary")),
        name="retention",
    )(q3, kt, v3, g3, decay, kw_t, qw_t, cd_t)


def _t5_bucket(dist):
    exact = T5_BUCKETS // 2
    n = jnp.maximum(dist, 0)
    large = exact + (jnp.log(jnp.maximum(n, 1).astype(F32) / exact) / math.log(T5_MAX_DIST / exact) * (T5_BUCKETS - exact)).astype(jnp.int32)
    large = jnp.minimum(large, T5_BUCKETS - 1)
    return jnp.where(n < exact, n, large)


def _dsa_bucket_tables():
    blk = DSA_BLOCK
    qi = jnp.arange(blk)[:, None]
    kj = jnp.arange(2 * blk)[None, :]
    tabs = []
    for window, dilation in DSA_PATTERNS:
        assert window // dilation <= blk
        for offset in (blk, 0):
            dist = qi + offset - kj
            valid = (dist >= 0) & (dist <= window // dilation)
            tabs.append(jnp.where(valid, _t5_bucket(dist * dilation), -1))
    return jnp.stack(tabs).reshape(len(DSA_PATTERNS), 2, blk, 2 * blk).astype(jnp.int32)


def _dsa_bias_kernel(rb_ref, bk_ref, o_ref):
    for v in range(2):
        bk = bk_ref[0, v]
        for h in range(DSA_HEADS):
            acc = jnp.full(bk.shape, -jnp.inf, F32)
            for b in range(T5_BUCKETS):
                acc = jnp.where(bk == b, rb_ref[b, h], acc)
            o_ref[0, v, h] = acc


def _dsa_bias(rel_bias):
    buckets = _dsa_bucket_tables()
    P, blk = len(DSA_PATTERNS), DSA_BLOCK
    return pl.pallas_call(
        _dsa_bias_kernel,
        grid=(P,),
        in_specs=[pl.BlockSpec(memory_space=pltpu.SMEM), pl.BlockSpec((1, 2, blk, 2 * blk), lambda p: (p, 0, 0, 0))],
        out_specs=pl.BlockSpec((1, 2, DSA_HEADS, blk, 2 * blk), lambda p: (p, 0, 0, 0, 0)),
        out_shape=jax.ShapeDtypeStruct((P, 2, DSA_HEADS, blk, 2 * blk), F32),
        compiler_params=_params(("parallel",)),
        name="dsa_bias",
    )(rel_bias, buckets)


def _dsa_kernel(*refs):
    P = len(DSA_PATTERNS)
    q_refs, k_refs, v_refs = refs[:P], refs[P : 2 * P], refs[2 * P : 3 * P]
    bias_ref, o_ref, acc_o, acc_m, acc_l = refs[3 * P :]
    blk = DSA_BLOCK
    tq = o_ref.shape[1]
    t = pl.program_id(2)
    lane = lax.broadcasted_iota(jnp.int32, (1, V7X_LANES), 1)
    head0 = lane < DSA_HEAD_DIM

    def block_logits(p, r, idx):
        blocks_per_residue = tq // (blk * r)
        rho = idx // blocks_per_residue
        nb = idx % blocks_per_residue
        l0 = pl.multiple_of(t * (tq // r) + nb * blk, blk)
        first = l0 == 0
        k_row = pl.multiple_of(jnp.where(first, l0, l0 - blk), blk)
        qb = q_refs[p][0, rho, pl.ds(l0, blk), :]
        zero = jnp.zeros_like(qb)
        q2 = jnp.concatenate([jnp.where(head0, qb, zero), jnp.where(head0, zero, qb)], axis=0)
        bias = bias_ref[p, pl.ds(first.astype(jnp.int32), 1), 0][0]
        kb = k_refs[p][0, rho, pl.ds(k_row, 2 * blk), :]
        logits = lax.dot_general(q2, kb, _NT, preferred_element_type=F32) + bias
        start = rho + nb * (blk * r)
        rows = pl.ds(start, blk, stride=r) if r > 1 else pl.ds(start, blk)
        return logits, (rho, k_row), rows

    def block_output(p, logits, key_rows, rows):
        rho, k_row = key_rows
        m = jnp.max(logits, -1, keepdims=True)
        e = jnp.exp(logits - m).astype(BF16)
        vb = v_refs[p][0, rho, pl.ds(k_row, 2 * blk), :]
        o2 = jnp.dot(e, jnp.concatenate([vb, jnp.ones_like(vb)], axis=1), preferred_element_type=F32)
        acc_o.at[p][rows, :] = jnp.where(head0, o2[:blk, :V7X_LANES], o2[blk:, :V7X_LANES])
        acc_m.at[p][rows, :] = jnp.where(head0, m[:blk], m[blk:])
        acc_l.at[p][rows, :] = jnp.where(head0, o2[:blk, V7X_LANES:], o2[blk:, V7X_LANES:])

    blocks = [(p, r, idx) for p, (_, r) in enumerate(DSA_PATTERNS) for idx in range(tq // blk)]
    staged = [block_logits(*blocks[i]) for i in range(DSA_GROUP)]
    for i, (p, _, _) in enumerate(blocks):
        if i + DSA_GROUP < len(blocks):
            staged.append(block_logits(*blocks[i + DSA_GROUP]))
        block_output(p, *staged[i])

    for c in range(tq // blk):
        sl = slice(c * blk, (c + 1) * blk)
        ms = [acc_m[bi, sl, :] for bi in range(len(DSA_PATTERNS))]
        m_all = functools.reduce(jnp.maximum, ms)
        num = den = None
        for bi, m in enumerate(ms):
            w = jnp.exp(m - m_all)
            n_i = w * acc_o[bi, sl, :]
            d_i = w * acc_l[bi, sl, :]
            num = n_i if num is None else num + n_i
            den = d_i if den is None else den + d_i
        o_ref[0, sl, :] = (num / den).astype(o_ref.dtype)


def _dilated_attention(ds, bias):
    B, _, S, _ = ds[0].shape
    pairs = DSA_W // V7X_LANES
    tq = DSA_TILE
    assert S % tq == 0 and S >= 2 * tq
    P, blk = len(DSA_PATTERNS), DSA_BLOCK
    bias5 = bias.reshape(P, 2, pairs, 2 * blk, 2 * blk)

    def seq(r, off):
        return pl.BlockSpec((1, r, S // r, V7X_LANES), lambda b, p, t: (b, 0, 0, off + p))

    return pl.pallas_call(
        _dsa_kernel,
        grid=(B, pairs, S // tq),
        in_specs=[seq(r, off) for off in (0, pairs, 2 * pairs) for _, r in DSA_PATTERNS]
        + [pl.BlockSpec((P, 2, 1, 2 * blk, 2 * blk), lambda b, p, t: (0, 0, p, 0, 0))],
        out_specs=pl.BlockSpec((1, tq, V7X_LANES), lambda b, p, t: (b, t, p)),
        out_shape=jax.ShapeDtypeStruct((B, S, DSA_W), BF16),
        scratch_shapes=[pltpu.VMEM((P, tq, V7X_LANES), F32)] * 3,
        compiler_params=_params(("parallel", "parallel", "arbitrary")),
        name="dilated_attention",
    )(*ds, *ds, *ds, bias5)


def _mlstm_kernel(q_ref, kt_ref, v_ref, og_ref, gr_ref, o_ref, c_ref, m_ref):
    h = pl.program_id(1)

    @pl.when(pl.program_id(2) == 0)
    def _():
        c_ref[...] = jnp.zeros_like(c_ref)
        m_ref[...] = jnp.zeros_like(m_ref)

    C = MLSTM_CHUNK
    dv = MLSTM_HEAD_DIM
    L = V7X_LANES
    n_chunks = q_ref.shape[1] // C
    ri = lax.broadcasted_iota(jnp.int32, (C, C), 0)
    ci = lax.broadcasted_iota(jnp.int32, (C, C), 1)
    causal = ci <= ri
    eye = ci == ri
    rep = lambda col: jnp.broadcast_to(col, (C, L))

    def gate_vectors(n):
        sl = slice(n * C, (n + 1) * C)
        gates = gr_ref[:, sl]
        gate_id = lax.broadcasted_iota(jnp.int32, (2 * MLSTM_HEADS, 1), 0)
        li_row = jnp.sum(jnp.where(gate_id == h, gates, 0.0), 0, keepdims=True)
        lf_row = jnp.sum(jnp.where(gate_id == MLSTM_HEADS + h, gates, 0.0), 0, keepdims=True)
        li_col = jnp.sum(jnp.where(eye, li_row, 0.0), -1, keepdims=True)
        a_col = jnp.sum(jnp.where(causal, lf_row, 0.0), -1, keepdims=True)
        g_tot = jnp.sum(lf_row, -1, keepdims=True)
        return li_row, li_col, a_col, g_tot

    def decay_matrix(gv):
        li_row, li_col, a_col, g_tot = gv
        a_rep = rep(a_col)
        a_row = jnp.sum(jnp.where(eye, a_rep, 0.0), 0, keepdims=True)
        w_state = g_tot - a_col + li_col
        m_loc = jnp.max(w_state, 0, keepdims=True)
        ew_rep = rep(jnp.exp(w_state - m_loc))
        dmat = jnp.where(causal, a_rep - a_row + li_row, -jnp.inf)
        d_max = jnp.max(dmat, -1, keepdims=True)
        return a_rep, g_tot, m_loc, ew_rep, dmat, d_max

    def local_matmuls(n, dm):
        a_rep, g_tot, m_loc, ew_rep, dmat, d_max = dm
        sl = slice(n * C, (n + 1) * C)
        qb = q_ref[0, sl, :]
        kt = kt_ref[:, sl]
        vb = v_ref[0, sl, :]
        d_rep = rep(d_max)
        p = (jnp.exp(dmat - d_rep) * jnp.dot(qb, kt, preferred_element_type=F32)).astype(BF16)
        pv = jnp.dot(p, jnp.concatenate([vb, jnp.ones((C, L), BF16)], axis=1), preferred_element_type=F32)
        vw = jnp.concatenate([vb.astype(F32) * jnp.concatenate([ew_rep] * (dv // L), axis=1), ew_rep], axis=1)
        kv_ext = jnp.dot(kt, vw.astype(BF16), preferred_element_type=F32)
        return qb, pv, kv_ext, a_rep, d_rep, g_tot, m_loc

    def recurrent(n, staged):
        qb, pv, kv_ext, a_rep, d_rep, g_tot, m_loc = staged
        sl = slice(n * C, (n + 1) * C)
        c_s = c_ref[...]
        m_s = m_ref[...][:, 0:1]
        inter = jnp.dot(qb, c_s.astype(BF16), preferred_element_type=F32)
        inter_log = a_rep + m_s
        m_row = jnp.maximum(inter_log, d_rep)
        f_intra = jnp.exp(d_rep - m_row)
        f_inter = jnp.exp(inter_log - m_row)
        den = f_intra * pv[:, dv:] + f_inter * inter[:, dv:]
        r = 1.0 / jnp.maximum(jnp.abs(den), jnp.exp(-m_row))
        for lo in range(0, dv, L):
            num = f_intra * pv[:, lo : lo + L] + f_inter * inter[:, lo : lo + L]
            o_ref[0, sl, lo : lo + L] = (num * r * og_ref[0, sl, lo : lo + L]).astype(o_ref.dtype)
        m_new = jnp.maximum(g_tot + m_s, m_loc)
        a_old = jnp.exp(g_tot + m_s - m_new)
        a_new = jnp.exp(m_loc - m_new)
        c_ref[...] = a_old * c_s + a_new * kv_ext
        m_ref[...] = jnp.broadcast_to(m_new, m_ref.shape)

    gvs = [gate_vectors(n) for n in range(n_chunks)]
    dms = [decay_matrix(gv) for gv in gvs]
    _lookahead(n_chunks, lambda n: local_matmuls(n, dms[n]), recurrent)


def _mlstm(q3, kt, v3, o3, gates_row):
    B, S, _ = v3.shape
    H, d = MLSTM_HEADS, MLSTM_HEAD_DIM
    ts = min(SEQ_TILE, S)
    tiles = S // ts
    col = pl.BlockSpec((1, ts, d), lambda b, h, c: (b, c, h))
    return pl.pallas_call(
        _mlstm_kernel,
        grid=(B, H, tiles),
        in_specs=[col, pl.BlockSpec((d, ts), lambda b, h, c: (h, b * tiles + c)), col, col,
                  pl.BlockSpec((2 * H, ts), lambda b, h, c: (0, b * tiles + c))],
        out_specs=pl.BlockSpec((1, ts, d), lambda b, h, c: (b, c, h)),
        out_shape=jax.ShapeDtypeStruct((B, S, H * d), BF16),
        scratch_shapes=[pltpu.VMEM((d, d + V7X_LANES), F32), pltpu.VMEM((1, V7X_LANES), F32)],
        compiler_params=_params(("parallel", "parallel", "arbitrary")),
        name="mlstm",
    )(q3, kt, v3, o3, gates_row)


def _even_mixer(x2d, B, S, w_in, rel_bias):
    tables = _retention_tables(S)
    q, kt, v, gate, d_nat, *d_dil = _even_in_proj(x2d, S, w_in.astype(BF16), tables[0], tables[1])
    y_r = _retention(q.reshape(B, S, -1), kt, v.reshape(B, S, -1), gate.reshape(B, S, -1), tables)
    y_d = _dilated_attention([d_nat.reshape(B, 1, S, -1)] + d_dil, _dsa_bias(rel_bias))
    return [y_r.reshape(B * S, RET_W), y_d.reshape(B * S, DSA_W)]


def _odd_mixer(x2d, B, S, w_in, gate_b, conv_w):
    H = MLSTM_HEADS
    wide = 4 * MLSTM_W
    wg = jnp.zeros((D_MODEL, V7X_LANES), F32).at[:, : 2 * H].set(w_in[:, wide:]).astype(BF16)
    gb = jnp.zeros((1, V7X_LANES), F32).at[0, : 2 * H].set(gate_b)
    q, kt, v, og, gates = _odd_in_proj(x2d, S, w_in.astype(BF16), wg, gb, conv_w)
    y = _mlstm(q.reshape(B, S, -1), kt, v.reshape(B, S, -1), og.reshape(B, S, -1), gates)
    return [y.reshape(B * S, MLSTM_W)]


def kernel(x, even_w_in, even_w_out, rel_bias, odd_w_in, odd_gate_b, odd_conv_w, odd_w_out, ffn_w_up, ffn_conv_w, ffn_conv_b, ffn_w_down, ln_g, ln_b):
    B, S, D = x.shape
    x2d = x.reshape(B * S, D)
    w_up, w_down = ffn_w_up.astype(BF16), ffn_w_down.astype(BF16)
    for layer in range(DEPTH):
        j = layer // 2
        if layer % 2 == 0:
            ys, w_out = _even_mixer(x2d, B, S, even_w_in[j], rel_bias), even_w_out[j]
        else:
            ys, w_out = _odd_mixer(x2d, B, S, odd_w_in[j], odd_gate_b[j], odd_conv_w[j]), odd_w_out[j]
        x2d = _layer_tail(x2d, ys, S, layer, w_out.astype(BF16), (ln_g[layer, 0], ln_b[layer, 0]), w_up,
                          ffn_conv_w[layer], ffn_conv_b[layer], w_down, (ln_g[layer, 1], ln_b[layer, 1]))
    return x2d.reshape(B, S, D)
```

```python
import functools
import math

import jax
import jax.numpy as jnp
from jax import lax
from jax.experimental import pallas as pl
from jax.experimental.pallas import tpu as pltpu

F32 = jnp.float32
BF16 = jnp.bfloat16

D_MODEL = 1024
DEPTH = 2
RET_HEADS = 4
RET_HEAD_DIM = 128
RET_CHUNK = 128
DSA_HEADS = 8
DSA_HEAD_DIM = 64
DSA_PATTERNS = ((128, 1), (512, 4), (2048, 16))
DSA_BLOCK = 128
T5_BUCKETS = 32
T5_MAX_DIST = 2048
MLSTM_HEADS = 4
MLSTM_HEAD_DIM = 256
MLSTM_CHUNK = 128
MLSTM_CONV = 4
D_FF = 2816
FFN_CONV = 3
LN_EPS = 1e-5
DEEPNORM_ALPHA = (2.0 * DEPTH) ** 0.25

RET_W = RET_HEADS * RET_HEAD_DIM
DSA_W = DSA_HEADS * DSA_HEAD_DIM
MLSTM_W = MLSTM_HEADS * MLSTM_HEAD_DIM

V7X_LANES = 128
V7X_SUBLANES = 8
V7X_VMEM_BYTES = 64 * 1024 * 1024
VMEM_LIMIT_BYTES = V7X_VMEM_BYTES - 8 * 1024 * 1024

ROW_TILE = 512
PROJ_CHUNK = 512
FFN_CHUNK = 256
SEQ_TILE = 4096
DSA_TILE = DSA_BLOCK * DSA_PATTERNS[-1][1]
DSA_GROUP = 2

_NT = (((1,), (1,)), ((), ()))
_TN = (((0,), (0,)), ((), ()))


def _params(semantics):
    return pltpu.CompilerParams(dimension_semantics=semantics, vmem_limit_bytes=VMEM_LIMIT_BYTES)


def _resident(shape):
    nd = len(shape)
    return pl.BlockSpec(shape, lambda *_: (0,) * nd, pipeline_mode=pl.Buffered(1))


def _resident_layer(stacked_shape, layer):
    return pl.BlockSpec((None,) + tuple(stacked_shape[1:]), lambda *_: (layer, 0, 0), pipeline_mode=pl.Buffered(1))


def _layer_norm(z, g, b):
    mu = jnp.mean(z, -1, keepdims=True)
    zc = z - mu
    var = jnp.mean(zc * zc, -1, keepdims=True)
    return zc * lax.rsqrt(var + LN_EPS) * g + b


def _silu(z):
    return z * jax.nn.sigmoid(z)


def _log_sigmoid(x):
    return jnp.minimum(x, 0.0) - jnp.log1p(jnp.exp(-jnp.abs(x)))


def _lookahead(n, produce, consume, depth=1):
    staged = [produce(j) for j in range(min(depth, n))]
    for j in range(n):
        if j + depth < n:
            staged.append(produce(j + depth))
        consume(j, staged[j])
        staged[j] = None


def _shift_rows(x, prev, s):
    if s == 0:
        return x
    rolled = pltpu.roll(x, s, 0)
    head = pltpu.roll(prev, s, 0)
    rid = lax.broadcasted_iota(jnp.int32, (V7X_SUBLANES, x.shape[1]), 0)
    first = jnp.where(rid < s, head, rolled[:V7X_SUBLANES])
    return jnp.concatenate([first, rolled[V7X_SUBLANES:]], axis=0)


def _causal_dwconv(x, prev, w_ref, taps):
    acc = None
    for k in range(taps):
        term = _shift_rows(x, prev, taps - 1 - k) * w_ref[k : k + 1, :]
        acc = term if acc is None else acc + term
    return acc


def _even_in_proj_kernel(x_ref, w_ref, cos_ref, sin_ref, q_ref, kt_ref, v_ref, g_ref, *rest):
    d_refs, stage_ref = rest[:-1], rest[-1]
    xb = x_ref[...].astype(BF16)
    tm = xb.shape[0]
    cos = cos_ref[...]
    sin = sin_ref[...]
    tn, d = RET_W, RET_HEAD_DIM

    def consume(j, h):
        if j < 2:
            for hh in range(RET_HEADS):
                c = h[:, hh * d : (hh + 1) * d]
                rot = c * cos + pltpu.roll(c, d // 2, 1) * sin
                if j == 0:
                    q_ref[:, hh * d : (hh + 1) * d] = rot.astype(BF16)
                else:
                    kt_ref[hh * d : (hh + 1) * d, :] = rot.T.astype(BF16)
        elif j == 2:
            v_ref[...] = h.astype(BF16)
        elif j == 3:
            g_ref[...] = _silu(h).astype(BF16)
        else:
            if j == 4:
                h = h * (DSA_HEAD_DIM ** -0.5)
            lo = (j - 4) * tn
            d_refs[0][:, lo : lo + tn] = h.astype(BF16)
            src, dst = stage_ref.at[0], stage_ref.at[1]
            for s in range(tn // V7X_LANES):
                src[s] = h[:, s * V7X_LANES : (s + 1) * V7X_LANES]
            prev_r = 1
            for d_ref, (_, r) in zip(d_refs[1:], DSA_PATTERNS[1:]):
                step = r // prev_r
                n_rows = tm // r
                for rho_prev in range(prev_r):
                    for a in range(step):
                        rho = rho_prev + prev_r * a
                        for s in range(tn // V7X_LANES):
                            rows = src.at[s][pl.ds(rho_prev * (tm // prev_r) + a, n_rows, stride=step), :]
                            if r != DSA_PATTERNS[-1][1]:
                                dst[s, rho * n_rows : (rho + 1) * n_rows, :] = rows
                            d_ref[0, rho, :, lo + s * V7X_LANES : lo + (s + 1) * V7X_LANES] = rows.astype(BF16)
                src, dst, prev_r = dst, src, r

    _lookahead(w_ref.shape[1] // tn, lambda j: jnp.dot(xb, w_ref[:, j * tn : (j + 1) * tn], preferred_element_type=F32), consume)


def _even_in_proj(x2d, seq_len, w, cos_t, sin_t):
    T, K = x2d.shape
    tm = min(ROW_TILE, seq_len)
    tps = seq_len // tm
    B = T // seq_len
    row = lambda i: (i, 0)
    pos = pl.BlockSpec((tm, RET_HEAD_DIM), lambda i: (i % tps, 0))
    shapes = [(T, RET_W), (RET_W, T), (T, RET_W), (T, RET_W), (T, 3 * DSA_W)]
    dtypes = [BF16, BF16, BF16, BF16, BF16]
    specs = [pl.BlockSpec((tm, RET_W), row), pl.BlockSpec((RET_W, tm), lambda i: (0, i)),
             pl.BlockSpec((tm, RET_W), row), pl.BlockSpec((tm, RET_W), row), pl.BlockSpec((tm, 3 * DSA_W), row)]
    for _, r in DSA_PATTERNS[1:]:
        shapes.append((B, r, seq_len // r, 3 * DSA_W))
        dtypes.append(BF16)
        specs.append(pl.BlockSpec((1, r, tm // r, 3 * DSA_W), lambda i: (i // tps, 0, i % tps, 0)))
    return pl.pallas_call(
        _even_in_proj_kernel,
        grid=(T // tm,),
        in_specs=[pl.BlockSpec((tm, K), row), _resident(w.shape), pos, pos],
        out_specs=specs,
        out_shape=[jax.ShapeDtypeStruct(s, dt) for s, dt in zip(shapes, dtypes)],
        scratch_shapes=[pltpu.VMEM((2, RET_W // V7X_LANES, tm, V7X_LANES), F32)],
        compiler_params=_params(("parallel",)),
        name="even_in_proj",
    )(x2d, w, cos_t, sin_t)


def _odd_in_proj_kernel(x_ref, w32_ref, wg_ref, gb_ref, cw_ref, q_ref, kt_ref, v_ref, o_ref, gt_ref, carry_ref, w_ref, *, tiles_per_seq):
    tn = PROJ_CHUNK
    per = MLSTM_W // tn

    @pl.when(pl.program_id(0) == 0)
    def _():
        for j in range(4 * per):
            w_ref[:, j * tn : (j + 1) * tn] = w32_ref[:, j * tn : (j + 1) * tn].astype(BF16)

    @pl.when(pl.program_id(0) % tiles_per_seq == 0)
    def _():
        carry_ref[...] = jnp.zeros_like(carry_ref)

    xb = x_ref[...].astype(BF16)
    tm = xb.shape[0]

    order = [c for pair in zip(range(2 * per), range(2 * per, 4 * per)) for c in pair]

    def produce(i):
        j = order[i]
        return jnp.dot(xb, w_ref[:, j * tn : (j + 1) * tn], preferred_element_type=F32)

    def consume(i, h):
        j = order[i]
        sl = slice(j * tn, (j + 1) * tn)
        if j < 2 * per:
            prev = carry_ref[:, sl]
            carry_ref[:, sl] = h[tm - V7X_SUBLANES :, :]
            act = _silu(_causal_dwconv(h, prev, cw_ref.at[:, sl], MLSTM_CONV))
            if j < per:
                q_ref[:, sl] = act.astype(BF16)
            else:
                kt_ref[(j - per) * tn : (j - per + 1) * tn, :] = (act * (MLSTM_HEAD_DIM ** -0.5)).T.astype(BF16)
        elif j < 3 * per:
            v_ref[:, (j - 2 * per) * tn : (j - 2 * per + 1) * tn] = h.astype(BF16)
        else:
            o_ref[:, (j - 3 * per) * tn : (j - 3 * per + 1) * tn] = jax.nn.sigmoid(h).astype(BF16)

    _lookahead(4 * per, produce, consume)
    gates = (jnp.dot(xb, wg_ref[...], preferred_element_type=F32) + gb_ref[...]).T[: 2 * MLSTM_HEADS, :]
    gate_id = lax.broadcasted_iota(jnp.int32, (2 * MLSTM_HEADS, 1), 0)
    gt_ref[...] = jnp.where(gate_id < MLSTM_HEADS, gates, _log_sigmoid(gates))


def _odd_in_proj(x2d, seq_len, w, wg, gb, conv_w):
    T, K = x2d.shape
    tm = min(ROW_TILE, seq_len)
    W = MLSTM_W
    cw = jnp.zeros((V7X_SUBLANES, 2 * W), F32).at[:MLSTM_CONV].set(conv_w)
    row = lambda i: (i, 0)
    shapes = ((T, W), (W, T), (T, W), (T, W), (2 * MLSTM_HEADS, T))
    dtypes = (BF16, BF16, BF16, BF16, F32)
    return pl.pallas_call(
        functools.partial(_odd_in_proj_kernel, tiles_per_seq=seq_len // tm),
        grid=(T // tm,),
        in_specs=[pl.BlockSpec((tm, K), row), _resident(w.shape), _resident(wg.shape), _resident(gb.shape),
                  _resident(cw.shape)],
        out_specs=[pl.BlockSpec((tm, W), row), pl.BlockSpec((W, tm), lambda i: (0, i)), pl.BlockSpec((tm, W), row),
                   pl.BlockSpec((tm, W), row), pl.BlockSpec((2 * MLSTM_HEADS, tm), lambda i: (0, i))],
        out_shape=[jax.ShapeDtypeStruct(s, dt) for s, dt in zip(shapes, dtypes)],
        scratch_shapes=[pltpu.VMEM((V7X_SUBLANES, 2 * MLSTM_W), F32), pltpu.VMEM((K, 4 * W), BF16)],
        compiler_params=_params(("arbitrary",)),
        name="odd_in_proj",
    )(x2d, w, wg, gb, cw)


def _tail_kernel(*refs, n_y, tiles_per_seq):
    x_ref = refs[0]
    y_refs = refs[1 : 1 + n_y]
    (wout_ref, g1_ref, b1_ref, wup_ref, cw_ref, cb_ref, wdn_ref, g_ref, b_ref,
     o_ref, carry_ref, act_ref, x1_ref) = refs[1 + n_y :]

    @pl.when(pl.program_id(0) % tiles_per_seq == 0)
    def _():
        carry_ref[...] = jnp.zeros_like(carry_ref)

    tm = x_ref.shape[0]
    for lo in range(0, tm, tm // 2):
        rows = slice(lo, lo + tm // 2)
        acc = None
        row = 0
        for y_ref in y_refs:
            kw = y_ref.shape[1]
            part = jnp.dot(y_ref[rows, :], wout_ref[row : row + kw, :], preferred_element_type=F32)
            acc = part if acc is None else acc + part
            row += kw
        x1_ref[rows, :] = _layer_norm(DEEPNORM_ALPHA * x_ref[rows, :] + acc, g1_ref[...], b1_ref[...])

    xb = x1_ref[...].astype(BF16)

    def up_dots(j):
        lo = j * FFN_CHUNK
        return (jnp.dot(xb, wup_ref[:, lo : lo + FFN_CHUNK], preferred_element_type=F32),
                jnp.dot(xb, wup_ref[:, D_FF + lo : D_FF + lo + FFN_CHUNK], preferred_element_type=F32))

    def activate(j, gate_up):
        gate, up = gate_up
        sl = slice(j * FFN_CHUNK, (j + 1) * FFN_CHUNK)
        prev = carry_ref[:, sl]
        carry_ref[:, sl] = gate[tm - V7X_SUBLANES :, :]
        conv = _causal_dwconv(gate, prev, cw_ref.at[:, sl], FFN_CONV) + cb_ref[:, sl]
        act_ref[:, sl] = (_silu(conv) * up).astype(BF16)

    _lookahead(D_FF // FFN_CHUNK, up_dots, activate)
    for lo in range(0, tm, tm // 2):
        rows = slice(lo, lo + tm // 2)
        ffn = jnp.dot(act_ref[rows, :], wdn_ref[...], preferred_element_type=F32)
        z = DEEPNORM_ALPHA * x1_ref[rows, :] + ffn
        o_ref[rows, :] = _layer_norm(z, g_ref[...], b_ref[...])


def _layer_tail(x2d, ys, seq_len, layer, w_out, ln1, w_up, conv_w, conv_b, w_down, ln2):
    T, D = x2d.shape
    tm = min(ROW_TILE, seq_len)
    cw = jnp.zeros((V7X_SUBLANES, D_FF), F32).at[:FFN_CONV].set(conv_w)
    row = lambda i: (i, 0)
    vec = lambda v: v.reshape(1, -1)
    return pl.pallas_call(
        functools.partial(_tail_kernel, n_y=len(ys), tiles_per_seq=seq_len // tm),
        grid=(T // tm,),
        in_specs=[pl.BlockSpec((tm, D), row)]
        + [pl.BlockSpec((tm, y.shape[1]), row) for y in ys]
        + [_resident(w_out.shape), _resident((1, D)), _resident((1, D)),
           _resident_layer(w_up.shape, layer), _resident(cw.shape), _resident((1, D_FF)),
           _resident_layer(w_down.shape, layer), _resident((1, D)), _resident((1, D))],
        out_specs=pl.BlockSpec((tm, D), row),
        out_shape=jax.ShapeDtypeStruct((T, D), F32),
        scratch_shapes=[pltpu.VMEM((V7X_SUBLANES, D_FF), F32), pltpu.VMEM((tm, D_FF), BF16), pltpu.VMEM((tm, D), F32)],
        compiler_params=_params(("arbitrary",)),
        name="layer_tail",
    )(x2d, *ys, w_out, vec(ln1[0]), vec(ln1[1]), w_up, cw, vec(conv_b), w_down, vec(ln2[0]), vec(ln2[1]))


def _retention_tables(seq_len):
    H, C, d = RET_HEADS, RET_CHUNK, RET_HEAD_DIM
    scale = d ** -0.5
    log_gamma = jnp.log1p(-jnp.exp2(-5.0 - jnp.arange(H, dtype=F32)))
    idx = jnp.arange(C, dtype=F32)
    rel = idx[:, None] - idx[None, :]
    decay = jnp.where(rel >= 0, jnp.exp(log_gamma[:, None, None] * jnp.maximum(rel, 0.0)), 0.0) * scale
    k_w = jnp.exp(log_gamma[:, None] * (C - 1 - idx)[None, :]) * scale
    q_w = jnp.exp(log_gamma[:, None] * (idx + 1.0)[None, :])
    kw_t = jnp.broadcast_to(k_w[:, :, None], (H, C, d))
    qw_t = jnp.broadcast_to(q_w[:, :, None], (H, C, d))
    cd_t = jnp.broadcast_to(jnp.exp(log_gamma * C)[:, None, None], (H, 1, d))
    inv = 1.0 / (10000.0 ** (jnp.arange(0, d, 2, dtype=F32) / d))
    ang = jnp.arange(seq_len, dtype=F32)[:, None] * inv[None, :]
    cos, sin = jnp.cos(ang), jnp.sin(ang)
    cos_t = jnp.concatenate([cos, cos], -1)
    sin_t = jnp.concatenate([-sin, sin], -1)
    return cos_t, sin_t, decay, kw_t, qw_t, cd_t


def _retention_kernel(q_ref, kt_ref, v_ref, g_ref, dec_ref, kw_ref, qw_ref, cd_ref, o_ref, r_ref):
    @pl.when(pl.program_id(2) == 0)
    def _():
        r_ref[...] = jnp.zeros_like(r_ref)

    C = RET_CHUNK
    decay = dec_ref[0]
    kw = kw_ref[0]
    qw = qw_ref[0]
    cd = cd_ref[0]

    def local(n):
        sl = slice(n * C, (n + 1) * C)
        qb = q_ref[0, sl, :]
        kt = kt_ref[:, sl]
        vb = v_ref[0, sl, :]
        scores = jnp.dot(qb, kt, preferred_element_type=F32) * decay
        y_intra = jnp.dot(scores.astype(BF16), vb, preferred_element_type=F32)
        kv = jnp.dot(kt, (vb.astype(F32) * kw).astype(BF16), preferred_element_type=F32)
        q_dec = (qb.astype(F32) * qw).astype(BF16)
        return y_intra, kv, q_dec

    def recurrent(n, staged):
        y_intra, kv, q_dec = staged
        sl = slice(n * C, (n + 1) * C)
        r = r_ref[...]
        y = y_intra + jnp.dot(q_dec, r.astype(BF16), preferred_element_type=F32)
        r_ref[...] = cd * r + kv
        mu = jnp.mean(y, -1, keepdims=True)
        yc = y - mu
        var = jnp.mean(yc * yc, -1, keepdims=True)
        o_ref[0, sl, :] = (yc * lax.rsqrt(var + LN_EPS) * g_ref[0, sl, :]).astype(o_ref.dtype)

    _lookahead(q_ref.shape[1] // C, local, recurrent)


def _retention(q3, kt, v3, g3, tables):
    B, S, _ = v3.shape
    H, d, C = RET_HEADS, RET_HEAD_DIM, RET_CHUNK
    ts = min(SEQ_TILE, S)
    tiles = S // ts
    _, _, decay, kw_t, qw_t, cd_t = tables
    col = pl.BlockSpec((1, ts, d), lambda b, h, c: (b, c, h))
    per_head = lambda shape: pl.BlockSpec((1,) + shape, lambda b, h, c: (h, 0, 0))
    return pl.pallas_call(
        _retention_kernel,
        grid=(B, H, tiles),
        in_specs=[col, pl.BlockSpec((d, ts), lambda b, h, c: (h, b * tiles + c)), col, col,
                  per_head((C, C)), per_head((C, d)), per_head((C, d)), per_head((1, d))],
        out_specs=pl.BlockSpec((1, ts, d), lambda b, h, c: (b, c, h)),
        out_shape=jax.ShapeDtypeStruct((B, S, H * d), BF16),
        scratch_shapes=[pltpu.VMEM((d, d), F32)],
        compiler_params=_params(("parallel", "parallel", "arbitrary")),
        name="retention",
    )(q3, kt, v3, g3, decay, kw_t, qw_t, cd_t)


def _t5_bucket(dist):
    exact = T5_BUCKETS // 2
    n = jnp.maximum(dist, 0)
    large = exact + (jnp.log(jnp.maximum(n, 1).astype(F32) / exact) / math.log(T5_MAX_DIST / exact) * (T5_BUCKETS - exact)).astype(jnp.int32)
    large = jnp.minimum(large, T5_BUCKETS - 1)
    return jnp.where(n < exact, n, large)


def _dsa_bucket_tables():
    blk = DSA_BLOCK
    qi = jnp.arange(blk)[:, None]
    kj = jnp.arange(2 * blk)[None, :]
    tabs = []
    for window, dilation in DSA_PATTERNS:
        assert window // dilation <= blk
        dist = qi + blk - kj
        valid = (dist >= 0) & (dist <= window // dilation)
        tabs.append(jnp.where(valid, _t5_bucket(dist * dilation), -1))
    return jnp.stack(tabs).astype(jnp.int32)


def _dsa_bias_kernel(rb_ref, bk_ref, o_ref):
    blk = DSA_BLOCK
    rows = 4 * V7X_SUBLANES
    masked = jnp.full((rows, blk), -jnp.inf, F32)
    for lo in range(0, blk, rows):
        bk = bk_ref[0, lo : lo + rows, :]
        accs = [jnp.full(bk.shape, -jnp.inf, F32) for _ in range(DSA_HEADS)]
        for b in range(T5_BUCKETS):
            hit = bk == b
            accs = [jnp.where(hit, rb_ref[b, h], acc) for h, acc in enumerate(accs)]
        for h, acc in enumerate(accs):
            o_ref[0, 0, h, lo : lo + rows, :] = acc
            o_ref[0, 1, h, lo : lo + rows, :] = jnp.concatenate([acc[:, blk:], masked], axis=1)


def _dsa_bias(rel_bias):
    buckets = _dsa_bucket_tables()
    P, blk = len(DSA_PATTERNS), DSA_BLOCK
    return pl.pallas_call(
        _dsa_bias_kernel,
        grid=(P,),
        in_specs=[pl.BlockSpec(memory_space=pltpu.SMEM), pl.BlockSpec((1, blk, 2 * blk), lambda p: (p, 0, 0))],
        out_specs=pl.BlockSpec((1, 2, DSA_HEADS, blk, 2 * blk), lambda p: (p, 0, 0, 0, 0)),
        out_shape=jax.ShapeDtypeStruct((P, 2, DSA_HEADS, blk, 2 * blk), F32),
        compiler_params=_params(("parallel",)),
        name="dsa_bias",
    )(rel_bias, buckets)


def _dsa_kernel(*refs):
    P = len(DSA_PATTERNS)
    q_refs, k_refs, v_refs = refs[:P], refs[P : 2 * P], refs[2 * P : 3 * P]
    bias_ref, o_ref, acc_o, acc_m, acc_l = refs[3 * P :]
    blk = DSA_BLOCK
    tq = o_ref.shape[1]
    t = pl.program_id(2)
    lane = lax.broadcasted_iota(jnp.int32, (1, V7X_LANES), 1)
    head0 = lane < DSA_HEAD_DIM

    def block_logits(p, r, idx):
        blocks_per_residue = tq // (blk * r)
        rho = idx // blocks_per_residue
        nb = idx % blocks_per_residue
        l0 = pl.multiple_of(t * (tq // r) + nb * blk, blk)
        first = l0 == 0
        k_row = pl.multiple_of(jnp.where(first, l0, l0 - blk), blk)
        qb = q_refs[p][0, rho, pl.ds(l0, blk), :]
        zero = jnp.zeros_like(qb)
        q2 = jnp.concatenate([jnp.where(head0, qb, zero), jnp.where(head0, zero, qb)], axis=0)
        bias = bias_ref[p, pl.ds(first.astype(jnp.int32), 1), 0][0]
        kb = k_refs[p][0, rho, pl.ds(k_row, 2 * blk), :]
        logits = lax.dot_general(q2, kb, _NT, preferred_element_type=F32) + bias
        start = rho + nb * (blk * r)
        rows = pl.ds(start, blk, stride=r) if r > 1 else pl.ds(start, blk)
        return logits, (rho, k_row), rows

    def block_output(p, logits, key_rows, rows):
        rho, k_row = key_rows
        m = jnp.max(logits, -1, keepdims=True)
        e = jnp.exp(logits - m).astype(BF16)
        vb = v_refs[p][0, rho, pl.ds(k_row, 2 * blk), :]
        o2 = jnp.dot(e, jnp.concatenate([vb, jnp.ones_like(vb)], axis=1), preferred_element_type=F32)
        acc_o.at[p][rows, :] = jnp.where(head0, o2[:blk, :V7X_LANES], o2[blk:, :V7X_LANES])
        acc_m.at[p][rows, :] = jnp.where(head0, m[:blk], m[blk:])
        acc_l.at[p][rows, :] = jnp.where(head0, o2[:blk, V7X_LANES:], o2[blk:, V7X_LANES:])

    blocks = [(p, r, idx) for p, (_, r) in enumerate(DSA_PATTERNS) for idx in range(tq // blk)]
    staged = [block_logits(*blocks[i]) for i in range(DSA_GROUP)]
    for i, (p, _, _) in enumerate(blocks):
        if i + DSA_GROUP < len(blocks):
            staged.append(block_logits(*blocks[i + DSA_GROUP]))
        block_output(p, *staged[i])

    for c in range(tq // blk):
        sl = slice(c * blk, (c + 1) * blk)
        ms = [acc_m[bi, sl, :] for bi in range(len(DSA_PATTERNS))]
        m_all = functools.reduce(jnp.maximum, ms)
        num = den = None
        for bi, m in enumerate(ms):
            w = jnp.exp(m - m_all)
            n_i = w * acc_o[bi, sl, :]
            d_i = w * acc_l[bi, sl, :]
            num = n_i if num is None else num + n_i
            den = d_i if den is None else den + d_i
        o_ref[0, sl, :] = (num / den).astype(o_ref.dtype)


def _dilated_attention(ds, bias):
    B, _, S, _ = ds[0].shape
    pairs = DSA_W // V7X_LANES
    tq = DSA_TILE
    assert S % tq == 0 and S >= 2 * tq
    P, blk = len(DSA_PATTERNS), DSA_BLOCK
    bias5 = bias.reshape(P, 2, pairs, 2 * blk, 2 * blk)

    def seq(r, off):
        return pl.BlockSpec((1, r, S // r, V7X_LANES), lambda b, p, t: (b, 0, 0, off + p))

    return pl.pallas_call(
        _dsa_kernel,
        grid=(B, pairs, S // tq),
        in_specs=[seq(r, off) for off in (0, pairs, 2 * pairs) for _, r in DSA_PATTERNS]
        + [pl.BlockSpec((P, 2, 1, 2 * blk, 2 * blk), lambda b, p, t: (0, 0, p, 0, 0))],
        out_specs=pl.BlockSpec((1, tq, V7X_LANES), lambda b, p, t: (b, t, p)),
        out_shape=jax.ShapeDtypeStruct((B, S, DSA_W), BF16),
        scratch_shapes=[pltpu.VMEM((P, tq, V7X_LANES), F32)] * 3,
        compiler_params=_params(("parallel", "parallel", "arbitrary")),
        name="dilated_attention",
    )(*ds, *ds, *ds, bias5)


def _mlstm_kernel(q_ref, kt_ref, v_ref, og_ref, gr_ref, o_ref, c_ref, m_ref):
    h = pl.program_id(1)

    @pl.when(pl.program_id(2) == 0)
    def _():
        c_ref[...] = jnp.zeros_like(c_ref)
        m_ref[...] = jnp.zeros_like(m_ref)

    C = MLSTM_CHUNK
    dv = MLSTM_HEAD_DIM
    L = V7X_LANES
    n_chunks = q_ref.shape[1] // C
    ri = lax.broadcasted_iota(jnp.int32, (C, C), 0)
    ci = lax.broadcasted_iota(jnp.int32, (C, C), 1)
    causal = ci <= ri
    eye = ci == ri
    rep = lambda col: jnp.broadcast_to(col, (C, L))

    def gate_vectors(n):
        sl = slice(n * C, (n + 1) * C)
        gates = gr_ref[:, sl]
        gate_id = lax.broadcasted_iota(jnp.int32, (2 * MLSTM_HEADS, 1), 0)
        li_row = jnp.sum(jnp.where(gate_id == h, gates, 0.0), 0, keepdims=True)
        lf_row = jnp.sum(jnp.where(gate_id == MLSTM_HEADS + h, gates, 0.0), 0, keepdims=True)
        li_col = jnp.sum(jnp.where(eye, li_row, 0.0), -1, keepdims=True)
        a_col = jnp.sum(jnp.where(causal, lf_row, 0.0), -1, keepdims=True)
        g_tot = jnp.sum(lf_row, -1, keepdims=True)
        return li_row, li_col, a_col, g_tot

    def decay_matrix(gv):
        li_row, li_col, a_col, g_tot = gv
        a_rep = rep(a_col)
        a_row = jnp.sum(jnp.where(eye, a_rep, 0.0), 0, keepdims=True)
        w_state = g_tot - a_col + li_col
        m_loc = jnp.max(w_state, 0, keepdims=True)
        ew_rep = rep(jnp.exp(w_state - m_loc))
        dmat = jnp.where(causal, a_rep - a_row + li_row, -jnp.inf)
        d_max = jnp.max(dmat, -1, keepdims=True)
        return a_rep, g_tot, m_loc, ew_rep, dmat, d_max

    def local_matmuls(n, dm):
        a_rep, g_tot, m_loc, ew_rep, dmat, d_max = dm
        sl = slice(n * C, (n + 1) * C)
        qb = q_ref[0, sl, :]
        kt = kt_ref[:, sl]
        vb = v_ref[0, sl, :]
        d_rep = rep(d_max)
        p = (jnp.exp(dmat - d_rep) * jnp.dot(qb, kt, preferred_element_type=F32)).astype(BF16)
        pv = jnp.dot(p, jnp.concatenate([vb, jnp.ones((C, L), BF16)], axis=1), preferred_element_type=F32)
        vw = jnp.concatenate([vb.astype(F32) * jnp.concatenate([ew_rep] * (dv // L), axis=1), ew_rep], axis=1)
        kv_ext = jnp.dot(kt, vw.astype(BF16), preferred_element_type=F32)
        return qb, pv, kv_ext, a_rep, d_rep, g_tot, m_loc

    def recurrent(n, staged):
        qb, pv, kv_ext, a_rep, d_rep, g_tot, m_loc = staged
        sl = slice(n * C, (n + 1) * C)
        c_s = c_ref[...]
        m_s = m_ref[...][:, 0:1]
        inter = jnp.dot(qb, c_s.astype(BF16), preferred_element_type=F32)
        inter_log = a_rep + m_s
        m_row = jnp.maximum(inter_log, d_rep)
        f_intra = jnp.exp(d_rep - m_row)
        f_inter = jnp.exp(inter_log - m_row)
        den = f_intra * pv[:, dv:] + f_inter * inter[:, dv:]
        r = 1.0 / jnp.maximum(jnp.abs(den), jnp.exp(-m_row))
        for lo in range(0, dv, L):
            num = f_intra * pv[:, lo : lo + L] + f_inter * inter[:, lo : lo + L]
            o_ref[0, sl, lo : lo + L] = (num * r * og_ref[0, sl, lo : lo + L]).astype(o_ref.dtype)
        m_new = jnp.maximum(g_tot + m_s, m_loc)
        a_old = jnp.exp(g_tot + m_s - m_new)
        a_new = jnp.exp(m_loc - m_new)
        c_ref[...] = a_old * c_s + a_new * kv_ext
        m_ref[...] = jnp.broadcast_to(m_new, m_ref.shape)

    gvs = [gate_vectors(n) for n in range(n_chunks)]
    dms = [decay_matrix(gv) for gv in gvs]
    _lookahead(n_chunks, lambda n: local_matmuls(n, dms[n]), recurrent)


def _mlstm(q3, kt, v3, o3, gates_row):
    B, S, _ = v3.shape
    H, d = MLSTM_HEADS, MLSTM_HEAD_DIM
    ts = min(SEQ_TILE, S)
    tiles = S // ts
    col = pl.BlockSpec((1, ts, d), lambda b, h, c: (b, c, h))
    return pl.pallas_call(
        _mlstm_kernel,
        grid=(B, H, tiles),
        in_specs=[col, pl.BlockSpec((d, ts), lambda b, h, c: (h, b * tiles + c)), col, col,
                  pl.BlockSpec((2 * H, ts), lambda b, h, c: (0, b * tiles + c))],
        out_specs=pl.BlockSpec((1, ts, d), lambda b, h, c: (b, c, h)),
        out_shape=jax.ShapeDtypeStruct((B, S, H * d), BF16),
        scratch_shapes=[pltpu.VMEM((d, d + V7X_LANES), F32), pltpu.VMEM((1, V7X_LANES), F32)],
        compiler_params=_params(("parallel", "parallel", "arbitrary")),
        name="mlstm",
    )(q3, kt, v3, o3, gates_row)


def _even_mixer(x2d, B, S, w_in, rel_bias):
    tables = _retention_tables(S)
    q, kt, v, gate, d_nat, *d_dil = _even_in_proj(x2d, S, w_in.astype(BF16), tables[0], tables[1])
    y_r = _retention(q.reshape(B, S, -1), kt, v.reshape(B, S, -1), gate.reshape(B, S, -1), tables)
    y_d = _dilated_attention([d_nat.reshape(B, 1, S, -1)] + d_dil, _dsa_bias(rel_bias))
    return [y_r.reshape(B * S, RET_W), y_d.reshape(B * S, DSA_W)]


def _odd_mixer(x2d, B, S, w_in, gate_b, conv_w):
    H = MLSTM_HEADS
    wide = 4 * MLSTM_W
    wg = jnp.zeros((D_MODEL, V7X_LANES), F32).at[:, : 2 * H].set(w_in[:, wide:]).astype(BF16)
    gb = jnp.zeros((1, V7X_LANES), F32).at[0, : 2 * H].set(gate_b)
    q, kt, v, og, gates = _odd_in_proj(x2d, S, w_in, wg, gb, conv_w)
    y = _mlstm(q.reshape(B, S, -1), kt, v.reshape(B, S, -1), og.reshape(B, S, -1), gates)
    return [y.reshape(B * S, MLSTM_W)]


def kernel(x, even_w_in, even_w_out, rel_bias, odd_w_in, odd_gate_b, odd_conv_w, odd_w_out, ffn_w_up, ffn_conv_w, ffn_conv_b, ffn_w_down, ln_g, ln_b):
    B, S, D = x.shape
    x2d = x.reshape(B * S, D)
    w_up, w_down = ffn_w_up.astype(BF16), ffn_w_down.astype(BF16)
    for layer in range(DEPTH):
        j = layer // 2
        if layer % 2 == 0:
            ys, w_out = _even_mixer(x2d, B, S, even_w_in[j], rel_bias), even_w_out[j]
        else:
            ys, w_out = _odd_mixer(x2d, B, S, odd_w_in[j], odd_gate_b[j], odd_conv_w[j]), odd_w_out[j]
        x2d = _layer_tail(x2d, ys, S, layer, w_out.astype(BF16), (ln_g[layer, 0], ln_b[layer, 0]), w_up,
                          ffn_conv_w[layer], ffn_conv_b[layer], w_down, (ln_g[layer, 1], ln_b[layer, 1]))
    return x2d.reshape(B, S, D)
```

```python
import functools
import math

import jax
import jax.numpy as jnp
from jax import lax
from jax.experimental import pallas as pl
from jax.experimental.pallas import tpu as pltpu

F32 = jnp.float32
BF16 = jnp.bfloat16

D_MODEL = 1024
DEPTH = 2
RET_HEADS = 4
RET_HEAD_DIM = 128
RET_CHUNK = 128
DSA_HEADS = 8
DSA_HEAD_DIM = 64
DSA_PATTERNS = ((128, 1), (512, 4), (2048, 16))
DSA_BLOCK = 128
T5_BUCKETS = 32
T5_MAX_DIST = 2048
MLSTM_HEADS = 4
MLSTM_HEAD_DIM = 256
MLSTM_CHUNK = 128
MLSTM_CONV = 4
D_FF = 2816
FFN_CONV = 3
LN_EPS = 1e-5
DEEPNORM_ALPHA = (2.0 * DEPTH) ** 0.25

RET_W = RET_HEADS * RET_HEAD_DIM
DSA_W = DSA_HEADS * DSA_HEAD_DIM
MLSTM_W = MLSTM_HEADS * MLSTM_HEAD_DIM

V7X_LANES = 128
V7X_SUBLANES = 8
V7X_VMEM_BYTES = 64 * 1024 * 1024
VMEM_LIMIT_BYTES = V7X_VMEM_BYTES - 8 * 1024 * 1024

ROW_TILE = 512
PROJ_CHUNK = 512
FFN_CHUNK = 256
SEQ_TILE = 4096
DSA_TILE = DSA_BLOCK * DSA_PATTERNS[-1][1]
DSA_GROUP = 2

_NT = (((1,), (1,)), ((), ()))
_TN = (((0,), (0,)), ((), ()))


def _params(semantics):
    return pltpu.CompilerParams(dimension_semantics=semantics, vmem_limit_bytes=VMEM_LIMIT_BYTES)


def _resident(shape):
    nd = len(shape)
    return pl.BlockSpec(shape, lambda *_: (0,) * nd, pipeline_mode=pl.Buffered(1))


def _resident_layer(stacked_shape, layer):
    return pl.BlockSpec((None,) + tuple(stacked_shape[1:]), lambda *_: (layer, 0, 0), pipeline_mode=pl.Buffered(1))


def _layer_norm(z, g, b):
    mu = jnp.mean(z, -1, keepdims=True)
    zc = z - mu
    var = jnp.mean(zc * zc, -1, keepdims=True)
    return zc * lax.rsqrt(var + LN_EPS) * g + b


def _silu(z):
    return z * jax.nn.sigmoid(z)


def _log_sigmoid(x):
    return jnp.minimum(x, 0.0) - jnp.log1p(jnp.exp(-jnp.abs(x)))


def _lookahead(n, produce, consume, depth=1):
    staged = [produce(j) for j in range(min(depth, n))]
    for j in range(n):
        if j + depth < n:
            staged.append(produce(j + depth))
        consume(j, staged[j])
        staged[j] = None


def _shift_rows(x, prev, s):
    if s == 0:
        return x
    rolled = pltpu.roll(x, s, 0)
    head = pltpu.roll(prev, s, 0)
    rid = lax.broadcasted_iota(jnp.int32, (V7X_SUBLANES, x.shape[1]), 0)
    first = jnp.where(rid < s, head, rolled[:V7X_SUBLANES])
    return jnp.concatenate([first, rolled[V7X_SUBLANES:]], axis=0)


def _causal_dwconv(x, prev, w_ref, taps):
    acc = None
    for k in range(taps):
        term = _shift_rows(x, prev, taps - 1 - k) * w_ref[k : k + 1, :]
        acc = term if acc is None else acc + term
    return acc


def _even_in_proj_kernel(x_ref, w_ref, cos_ref, sin_ref, q_ref, kt_ref, v_ref, g_ref, *rest):
    d_refs, stage_ref = rest[:-1], rest[-1]
    xb = x_ref[...].astype(BF16)
    tm = xb.shape[0]
    cos = cos_ref[...]
    sin = sin_ref[...]
    tn, d = RET_W, RET_HEAD_DIM

    def consume(j, h):
        if j < 2:
            for hh in range(RET_HEADS):
                c = h[:, hh * d : (hh + 1) * d]
                rot = c * cos + pltpu.roll(c, d // 2, 1) * sin
                if j == 0:
                    q_ref[:, hh * d : (hh + 1) * d] = rot.astype(BF16)
                else:
                    kt_ref[hh * d : (hh + 1) * d, :] = rot.T.astype(BF16)
        elif j == 2:
            v_ref[...] = h.astype(BF16)
        elif j == 3:
            g_ref[...] = _silu(h).astype(BF16)
        else:
            if j == 4:
                h = h * (DSA_HEAD_DIM ** -0.5)
            lo = (j - 4) * tn
            d_refs[0][:, lo : lo + tn] = h.astype(BF16)
            src, dst = stage_ref.at[0], stage_ref.at[1]
            for s in range(tn // V7X_LANES):
                src[s] = h[:, s * V7X_LANES : (s + 1) * V7X_LANES]
            prev_r = 1
            for d_ref, (_, r) in zip(d_refs[1:], DSA_PATTERNS[1:]):
                step = r // prev_r
                n_rows = tm // r
                for rho_prev in range(prev_r):
                    for a in range(step):
                        rho = rho_prev + prev_r * a
                        for s in range(tn // V7X_LANES):
                            rows = src.at[s][pl.ds(rho_prev * (tm // prev_r) + a, n_rows, stride=step), :]
                            if r != DSA_PATTERNS[-1][1]:
                                dst[s, rho * n_rows : (rho + 1) * n_rows, :] = rows
                            d_ref[0, rho, :, lo + s * V7X_LANES : lo + (s + 1) * V7X_LANES] = rows.astype(BF16)
                src, dst, prev_r = dst, src, r

    _lookahead(w_ref.shape[1] // tn, lambda j: jnp.dot(xb, w_ref[:, j * tn : (j + 1) * tn], preferred_element_type=F32), consume)


def _even_in_proj(x2d, seq_len, w, cos_t, sin_t):
    T, K = x2d.shape
    tm = min(ROW_TILE, seq_len)
    tps = seq_len // tm
    B = T // seq_len
    row = lambda i: (i, 0)
    pos = pl.BlockSpec((tm, RET_HEAD_DIM), lambda i: (i % tps, 0))
    shapes = [(T, RET_W), (RET_W, T), (T, RET_W), (T, RET_W), (T, 3 * DSA_W)]
    dtypes = [BF16, BF16, BF16, BF16, BF16]
    specs = [pl.BlockSpec((tm, RET_W), row), pl.BlockSpec((RET_W, tm), lambda i: (0, i)),
             pl.BlockSpec((tm, RET_W), row), pl.BlockSpec((tm, RET_W), row), pl.BlockSpec((tm, 3 * DSA_W), row)]
    for _, r in DSA_PATTERNS[1:]:
        shapes.append((B, r, seq_len // r, 3 * DSA_W))
        dtypes.append(BF16)
        specs.append(pl.BlockSpec((1, r, tm // r, 3 * DSA_W), lambda i: (i // tps, 0, i % tps, 0)))
    return pl.pallas_call(
        _even_in_proj_kernel,
        grid=(T // tm,),
        in_specs=[pl.BlockSpec((tm, K), row), _resident(w.shape), pos, pos],
        out_specs=specs,
        out_shape=[jax.ShapeDtypeStruct(s, dt) for s, dt in zip(shapes, dtypes)],
        scratch_shapes=[pltpu.VMEM((2, RET_W // V7X_LANES, tm, V7X_LANES), F32)],
        compiler_params=_params(("parallel",)),
        name="even_in_proj",
    )(x2d, w, cos_t, sin_t)


def _odd_in_proj_kernel(x_ref, w_ref, wg_ref, gb_ref, cw_ref, q_ref, kt_ref, v_ref, o_ref, gt_ref, carry_ref, *, tiles_per_seq):
    tn = PROJ_CHUNK
    per = MLSTM_W // tn

    @pl.when(pl.program_id(0) % tiles_per_seq == 0)
    def _():
        carry_ref[...] = jnp.zeros_like(carry_ref)

    xb = x_ref[...].astype(BF16)
    tm = xb.shape[0]

    order = [c for pair in zip(range(2 * per), range(2 * per, 4 * per)) for c in pair]

    def produce(i):
        j = order[i]
        return jnp.dot(xb, w_ref[:, j * tn : (j + 1) * tn], preferred_element_type=F32)

    def consume(i, h):
        j = order[i]
        sl = slice(j * tn, (j + 1) * tn)
        if j < 2 * per:
            prev = carry_ref[:, sl]
            carry_ref[:, sl] = h[tm - V7X_SUBLANES :, :]
            act = _silu(_causal_dwconv(h, prev, cw_ref.at[:, sl], MLSTM_CONV))
            if j < per:
                q_ref[:, sl] = act.astype(BF16)
            else:
                kt_ref[(j - per) * tn : (j - per + 1) * tn, :] = (act * (MLSTM_HEAD_DIM ** -0.5)).T.astype(BF16)
        elif j < 3 * per:
            v_ref[:, (j - 2 * per) * tn : (j - 2 * per + 1) * tn] = h.astype(BF16)
        else:
            o_ref[:, (j - 3 * per) * tn : (j - 3 * per + 1) * tn] = jax.nn.sigmoid(h).astype(BF16)

    _lookahead(4 * per, produce, consume)
    gates = (jnp.dot(xb, wg_ref[...], preferred_element_type=F32) + gb_ref[...]).T[: 2 * MLSTM_HEADS, :]
    gate_id = lax.broadcasted_iota(jnp.int32, (2 * MLSTM_HEADS, 1), 0)
    gt_ref[...] = jnp.where(gate_id < MLSTM_HEADS, gates, _log_sigmoid(gates))


def _odd_in_proj(x2d, seq_len, w, wg, gb, conv_w):
    T, K = x2d.shape
    tm = min(ROW_TILE, seq_len)
    W = MLSTM_W
    cw = jnp.zeros((V7X_SUBLANES, 2 * W), F32).at[:MLSTM_CONV].set(conv_w)
    row = lambda i: (i, 0)
    shapes = ((T, W), (W, T), (T, W), (T, W), (2 * MLSTM_HEADS, T))
    dtypes = (BF16, BF16, BF16, BF16, F32)
    return pl.pallas_call(
        functools.partial(_odd_in_proj_kernel, tiles_per_seq=seq_len // tm),
        grid=(T // tm,),
        in_specs=[pl.BlockSpec((tm, K), row), _resident(w.shape), _resident(wg.shape), _resident(gb.shape),
                  _resident(cw.shape)],
        out_specs=[pl.BlockSpec((tm, W), row), pl.BlockSpec((W, tm), lambda i: (0, i)), pl.BlockSpec((tm, W), row),
                   pl.BlockSpec((tm, W), row), pl.BlockSpec((2 * MLSTM_HEADS, tm), lambda i: (0, i))],
        out_shape=[jax.ShapeDtypeStruct(s, dt) for s, dt in zip(shapes, dtypes)],
        scratch_shapes=[pltpu.VMEM((V7X_SUBLANES, 2 * MLSTM_W), F32)],
        compiler_params=_params(("arbitrary",)),
        name="odd_in_proj",
    )(x2d, w, wg, gb, cw)


def _tail_kernel(*refs, n_y, tiles_per_seq):
    x_ref = refs[0]
    y_refs = refs[1 : 1 + n_y]
    (wout_ref, g1_ref, b1_ref, wup_ref, cw_ref, cb_ref, wdn_ref, g_ref, b_ref,
     o_ref, carry_ref, act_ref, x1_ref) = refs[1 + n_y :]

    @pl.when(pl.program_id(0) % tiles_per_seq == 0)
    def _():
        carry_ref[...] = jnp.zeros_like(carry_ref)

    tm = x_ref.shape[0]
    for lo in range(0, tm, tm // 2):
        rows = slice(lo, lo + tm // 2)
        acc = None
        row = 0
        for y_ref in y_refs:
            kw = y_ref.shape[1]
            part = jnp.dot(y_ref[rows, :], wout_ref[row : row + kw, :], preferred_element_type=F32)
            acc = part if acc is None else acc + part
            row += kw
        x1_ref[rows, :] = _layer_norm(DEEPNORM_ALPHA * x_ref[rows, :] + acc, g1_ref[...], b1_ref[...])

    xb = x1_ref[...].astype(BF16)

    def up_dots(j):
        lo = j * FFN_CHUNK
        return (jnp.dot(xb, wup_ref[:, lo : lo + FFN_CHUNK], preferred_element_type=F32),
                jnp.dot(xb, wup_ref[:, D_FF + lo : D_FF + lo + FFN_CHUNK], preferred_element_type=F32))

    def activate(j, gate_up):
        gate, up = gate_up
        sl = slice(j * FFN_CHUNK, (j + 1) * FFN_CHUNK)
        prev = carry_ref[:, sl]
        carry_ref[:, sl] = gate[tm - V7X_SUBLANES :, :]
        conv = _causal_dwconv(gate, prev, cw_ref.at[:, sl], FFN_CONV) + cb_ref[:, sl]
        act_ref[:, sl] = (_silu(conv) * up).astype(BF16)

    _lookahead(D_FF // FFN_CHUNK, up_dots, activate)
    for lo in range(0, tm, tm // 2):
        rows = slice(lo, lo + tm // 2)
        ffn = jnp.dot(act_ref[rows, :], wdn_ref[...], preferred_element_type=F32)
        z = DEEPNORM_ALPHA * x1_ref[rows, :] + ffn
        o_ref[rows, :] = _layer_norm(z, g_ref[...], b_ref[...])


def _layer_tail(x2d, ys, seq_len, layer, w_out, ln1, w_up, conv_w, conv_b, w_down, ln2):
    T, D = x2d.shape
    tm = min(ROW_TILE, seq_len)
    cw = jnp.zeros((V7X_SUBLANES, D_FF), F32).at[:FFN_CONV].set(conv_w)
    row = lambda i: (i, 0)
    vec = lambda v: v.reshape(1, -1)
    return pl.pallas_call(
        functools.partial(_tail_kernel, n_y=len(ys), tiles_per_seq=seq_len // tm),
        grid=(T // tm,),
        in_specs=[pl.BlockSpec((tm, D), row)]
        + [pl.BlockSpec((tm, y.shape[1]), row) for y in ys]
        + [_resident(w_out.shape), _resident((1, D)), _resident((1, D)),
           _resident_layer(w_up.shape, layer), _resident(cw.shape), _resident((1, D_FF)),
           _resident_layer(w_down.shape, layer), _resident((1, D)), _resident((1, D))],
        out_specs=pl.BlockSpec((tm, D), row),
        out_shape=jax.ShapeDtypeStruct((T, D), F32),
        scratch_shapes=[pltpu.VMEM((V7X_SUBLANES, D_FF), F32), pltpu.VMEM((tm, D_FF), BF16), pltpu.VMEM((tm, D), F32)],
        compiler_params=_params(("arbitrary",)),
        name="layer_tail",
    )(x2d, *ys, w_out, vec(ln1[0]), vec(ln1[1]), w_up, cw, vec(conv_b), w_down, vec(ln2[0]), vec(ln2[1]))


def _retention_tables(seq_len):
    H, C, d = RET_HEADS, RET_CHUNK, RET_HEAD_DIM
    scale = d ** -0.5
    log_gamma = jnp.log1p(-jnp.exp2(-5.0 - jnp.arange(H, dtype=F32)))
    idx = jnp.arange(C, dtype=F32)
    rel = idx[:, None] - idx[None, :]
    decay = jnp.where(rel >= 0, jnp.exp(log_gamma[:, None, None] * jnp.maximum(rel, 0.0)), 0.0) * scale
    k_w = jnp.exp(log_gamma[:, None] * (C - 1 - idx)[None, :]) * scale
    q_w = jnp.exp(log_gamma[:, None] * (idx + 1.0)[None, :])
    kw_t = jnp.broadcast_to(k_w[:, :, None], (H, C, d))
    qw_t = jnp.broadcast_to(q_w[:, :, None], (H, C, d))
    cd_t = jnp.broadcast_to(jnp.exp(log_gamma * C)[:, None, None], (H, 1, d))
    inv = 1.0 / (10000.0 ** (jnp.arange(0, d, 2, dtype=F32) / d))
    ang = jnp.arange(seq_len, dtype=F32)[:, None] * inv[None, :]
    cos, sin = jnp.cos(ang), jnp.sin(ang)
    cos_t = jnp.concatenate([cos, cos], -1)
    sin_t = jnp.concatenate([-sin, sin], -1)
    return cos_t, sin_t, decay, kw_t, qw_t, cd_t


def _retention_kernel(q_ref, kt_ref, v_ref, g_ref, dec_ref, kw_ref, qw_ref, cd_ref, o_ref, r_ref):
    @pl.when(pl.program_id(2) == 0)
    def _():
        r_ref[...] = jnp.zeros_like(r_ref)

    C = RET_CHUNK
    decay = dec_ref[0]
    kw = kw_ref[0]
    qw = qw_ref[0]
    cd = cd_ref[0]

    def local(n):
        sl = slice(n * C, (n + 1) * C)
        qb = q_ref[0, sl, :]
        kt = kt_ref[:, sl]
        vb = v_ref[0, sl, :]
        scores = jnp.dot(qb, kt, preferred_element_type=F32) * decay
        y_intra = jnp.dot(scores.astype(BF16), vb, preferred_element_type=F32)
        kv = jnp.dot(kt, (vb.astype(F32) * kw).astype(BF16), preferred_element_type=F32)
        q_dec = (qb.astype(F32) * qw).astype(BF16)
        return y_intra, kv, q_dec

    def recurrent(n, staged):
        y_intra, kv, q_dec = staged
        sl = slice(n * C, (n + 1) * C)
        r = r_ref[...]
        y = y_intra + jnp.dot(q_dec, r.astype(BF16), preferred_element_type=F32)
        r_ref[...] = cd * r + kv
        mu = jnp.mean(y, -1, keepdims=True)
        yc = y - mu
        var = jnp.mean(yc * yc, -1, keepdims=True)
        o_ref[0, sl, :] = (yc * lax.rsqrt(var + LN_EPS) * g_ref[0, sl, :]).astype(o_ref.dtype)

    _lookahead(q_ref.shape[1] // C, local, recurrent)


def _retention(q3, kt, v3, g3, tables):
    B, S, _ = v3.shape
    H, d, C = RET_HEADS, RET_HEAD_DIM, RET_CHUNK
    ts = min(SEQ_TILE, S)
    tiles = S // ts
    _, _, decay, kw_t, qw_t, cd_t = tables
    col = pl.BlockSpec((1, ts, d), lambda b, h, c: (b, c, h))
    per_head = lambda shape: pl.BlockSpec((1,) + shape, lambda b, h, c: (h, 0, 0))
    return pl.pallas_call(
        _retention_kernel,
        grid=(B, H, tiles),
        in_specs=[col, pl.BlockSpec((d, ts), lambda b, h, c: (h, b * tiles + c)), col, col,
                  per_head((C, C)), per_head((C, d)), per_head((C, d)), per_head((1, d))],
        out_specs=pl.BlockSpec((1, ts, d), lambda b, h, c: (b, c, h)),
        out_shape=jax.ShapeDtypeStruct((B, S, H * d), BF16),
        scratch_shapes=[pltpu.VMEM((d, d), F32)],
        compiler_params=_params(("parallel", "parallel", "arbitrary")),
        name="retention",
    )(q3, kt, v3, g3, decay, kw_t, qw_t, cd_t)


def _t5_bucket(dist):
    exact = T5_BUCKETS // 2
    n = jnp.maximum(dist, 0)
    large = exact + (jnp.log(jnp.maximum(n, 1).astype(F32) / exact) / math.log(T5_MAX_DIST / exact) * (T5_BUCKETS - exact)).astype(jnp.int32)
    large = jnp.minimum(large, T5_BUCKETS - 1)
    return jnp.where(n < exact, n, large)


def _dsa_bucket_tables():
    blk = DSA_BLOCK
    qi = jnp.arange(blk)[:, None]
    kj = jnp.arange(2 * blk)[None, :]
    tabs = []
    for window, dilation in DSA_PATTERNS:
        assert window // dilation <= blk
        dist = qi + blk - kj
        valid = (dist >= 0) & (dist <= window // dilation)
        tabs.append(jnp.where(valid, _t5_bucket(dist * dilation), -1))
    return jnp.stack(tabs).astype(jnp.int32)


def _dsa_bias_kernel(rb_ref, bk_ref, o_ref):
    blk = DSA_BLOCK
    rows = 4 * V7X_SUBLANES
    masked = jnp.full((rows, blk), -jnp.inf, F32)
    for lo in range(0, blk, rows):
        bk = bk_ref[0, lo : lo + rows, :]
        accs = [jnp.full(bk.shape, -jnp.inf, F32) for _ in range(DSA_HEADS)]
        for b in range(T5_BUCKETS):
            hit = bk == b
            accs = [jnp.where(hit, rb_ref[b, h], acc) for h, acc in enumerate(accs)]
        for h, acc in enumerate(accs):
            o_ref[0, 0, h, lo : lo + rows, :] = acc
            o_ref[0, 1, h, lo : lo + rows, :] = jnp.concatenate([acc[:, blk:], masked], axis=1)


def _dsa_bias(rel_bias):
    buckets = _dsa_bucket_tables()
    P, blk = len(DSA_PATTERNS), DSA_BLOCK
    return pl.pallas_call(
        _dsa_bias_kernel,
        grid=(P,),
        in_specs=[pl.BlockSpec(memory_space=pltpu.SMEM), pl.BlockSpec((1, blk, 2 * blk), lambda p: (p, 0, 0))],
        out_specs=pl.BlockSpec((1, 2, DSA_HEADS, blk, 2 * blk), lambda p: (p, 0, 0, 0, 0)),
        out_shape=jax.ShapeDtypeStruct((P, 2, DSA_HEADS, blk, 2 * blk), F32),
        compiler_params=_params(("parallel",)),
        name="dsa_bias",
    )(rel_bias, buckets)


def _dsa_kernel(*refs):
    P = len(DSA_PATTERNS)
    q_refs, k_refs, v_refs = refs[:P], refs[P : 2 * P], refs[2 * P : 3 * P]
    bias_ref, o_ref, acc_o, acc_m, acc_l = refs[3 * P :]
    blk = DSA_BLOCK
    tq = o_ref.shape[1]
    t = pl.program_id(2)
    lane = lax.broadcasted_iota(jnp.int32, (1, V7X_LANES), 1)
    head0 = lane < DSA_HEAD_DIM

    def block_logits(p, r, idx):
        blocks_per_residue = tq // (blk * r)
        rho = idx // blocks_per_residue
        nb = idx % blocks_per_residue
        l0 = pl.multiple_of(t * (tq // r) + nb * blk, blk)
        first = l0 == 0
        k_row = pl.multiple_of(jnp.where(first, l0, l0 - blk), blk)
        qb = q_refs[p][0, rho, pl.ds(l0, blk), :]
        zero = jnp.zeros_like(qb)
        q2 = jnp.concatenate([jnp.where(head0, qb, zero), jnp.where(head0, zero, qb)], axis=0)
        bias = bias_ref[p, pl.ds(first.astype(jnp.int32), 1), 0][0]
        kb = k_refs[p][0, rho, pl.ds(k_row, 2 * blk), :]
        logits = lax.dot_general(q2, kb, _NT, preferred_element_type=F32) + bias
        start = rho + nb * (blk * r)
        rows = pl.ds(start, blk, stride=r) if r > 1 else pl.ds(start, blk)
        return logits, (rho, k_row), rows

    def block_output(p, logits, key_rows, rows):
        rho, k_row = key_rows
        m = jnp.max(logits, -1, keepdims=True)
        e = jnp.exp(logits - m).astype(BF16)
        vb = v_refs[p][0, rho, pl.ds(k_row, 2 * blk), :]
        o2 = jnp.dot(e, jnp.concatenate([vb, jnp.ones_like(vb)], axis=1), preferred_element_type=F32)
        acc_o.at[p][rows, :] = jnp.where(head0, o2[:blk, :V7X_LANES], o2[blk:, :V7X_LANES])
        acc_m.at[p][rows, :] = jnp.where(head0, m[:blk], m[blk:])
        acc_l.at[p][rows, :] = jnp.where(head0, o2[:blk, V7X_LANES:], o2[blk:, V7X_LANES:])

    blocks = [(p, r, idx) for p, (_, r) in enumerate(DSA_PATTERNS) for idx in range(tq // blk)]
    staged = [block_logits(*blocks[i]) for i in range(DSA_GROUP)]
    for i, (p, _, _) in enumerate(blocks):
        if i + DSA_GROUP < len(blocks):
            staged.append(block_logits(*blocks[i + DSA_GROUP]))
        block_output(p, *staged[i])

    for c in range(tq // blk):
        sl = slice(c * blk, (c + 1) * blk)
        ms = [acc_m[bi, sl, :] for bi in range(len(DSA_PATTERNS))]
        m_all = functools.reduce(jnp.maximum, ms)
        num = den = None
        for bi, m in enumerate(ms):
            w = jnp.exp(m - m_all)
            n_i = w * acc_o[bi, sl, :]
            d_i = w * acc_l[bi, sl, :]
            num = n_i if num is None else num + n_i
            den = d_i if den is None else den + d_i
        o_ref[0, sl, :] = (num / den).astype(o_ref.dtype)


def _dilated_attention(ds, bias):
    B, _, S, _ = ds[0].shape
    pairs = DSA_W // V7X_LANES
    tq = DSA_TILE
    assert S % tq == 0 and S >= 2 * tq
    P, blk = len(DSA_PATTERNS), DSA_BLOCK
    bias5 = bias.reshape(P, 2, pairs, 2 * blk, 2 * blk)

    def seq(r, off):
        return pl.BlockSpec((1, r, S // r, V7X_LANES), lambda b, p, t: (b, 0, 0, off + p))

    return pl.pallas_call(
        _dsa_kernel,
        grid=(B, pairs, S // tq),
        in_specs=[seq(r, off) for off in (0, pairs, 2 * pairs) for _, r in DSA_PATTERNS]
        + [pl.BlockSpec((P, 2, 1, 2 * blk, 2 * blk), lambda b, p, t: (0, 0, p, 0, 0))],
        out_specs=pl.BlockSpec((1, tq, V7X_LANES), lambda b, p, t: (b, t, p)),
        out_shape=jax.ShapeDtypeStruct((B, S, DSA_W), BF16),
        scratch_shapes=[pltpu.VMEM((P, tq, V7X_LANES), F32)] * 3,
        compiler_params=_params(("parallel", "parallel", "arbitrary")),
        name="dilated_attention",
    )(*ds, *ds, *ds, bias5)


def _mlstm_kernel(q_ref, kt_ref, v_ref, og_ref, gr_ref, o_ref, c_ref, m_ref):
    h = pl.program_id(1)

    @pl.when(pl.program_id(2) == 0)
    def _():
        c_ref[...] = jnp.zeros_like(c_ref)
        m_ref[...] = jnp.zeros_like(m_ref)

    C = MLSTM_CHUNK
    dv = MLSTM_HEAD_DIM
    L = V7X_LANES
    n_chunks = q_ref.shape[1] // C
    ri = lax.broadcasted_iota(jnp.int32, (C, C), 0)
    ci = lax.broadcasted_iota(jnp.int32, (C, C), 1)
    causal = ci <= ri
    eye = ci == ri
    rep = lambda col: jnp.broadcast_to(col, (C, L))

    def gate_vectors(n):
        sl = slice(n * C, (n + 1) * C)
        gates = gr_ref[:, sl]
        gate_id = lax.broadcasted_iota(jnp.int32, (2 * MLSTM_HEADS, 1), 0)
        li_row = jnp.sum(jnp.where(gate_id == h, gates, 0.0), 0, keepdims=True)
        lf_row = jnp.sum(jnp.where(gate_id == MLSTM_HEADS + h, gates, 0.0), 0, keepdims=True)
        li_col = jnp.sum(jnp.where(eye, li_row, 0.0), -1, keepdims=True)
        a_col = jnp.sum(jnp.where(causal, lf_row, 0.0), -1, keepdims=True)
        g_tot = jnp.sum(lf_row, -1, keepdims=True)
        return li_row, li_col, a_col, g_tot

    def decay_matrix(gv):
        li_row, li_col, a_col, g_tot = gv
        a_rep = rep(a_col)
        a_row = jnp.sum(jnp.where(eye, a_rep, 0.0), 0, keepdims=True)
        w_state = g_tot - a_col + li_col
        m_loc = jnp.max(w_state, 0, keepdims=True)
        ew_rep = rep(jnp.exp(w_state - m_loc))
        dmat = jnp.where(causal, a_rep - a_row + li_row, -jnp.inf)
        d_max = jnp.max(dmat, -1, keepdims=True)
        return a_rep, g_tot, m_loc, ew_rep, dmat, d_max

    def local_matmuls(n, dm):
        a_rep, g_tot, m_loc, ew_rep, dmat, d_max = dm
        sl = slice(n * C, (n + 1) * C)
        qb = q_ref[0, sl, :]
        kt = kt_ref[:, sl]
        vb = v_ref[0, sl, :]
        d_rep = rep(d_max)
        p = (jnp.exp(dmat - d_rep) * jnp.dot(qb, kt, preferred_element_type=F32)).astype(BF16)
        pv = jnp.dot(p, jnp.concatenate([vb, jnp.ones((C, L), BF16)], axis=1), preferred_element_type=F32)
        vw = jnp.concatenate([vb.astype(F32) * jnp.concatenate([ew_rep] * (dv // L), axis=1), ew_rep], axis=1)
        kv_ext = jnp.dot(kt, vw.astype(BF16), preferred_element_type=F32)
        return qb, pv, kv_ext, a_rep, d_rep, g_tot, m_loc

    def recurrent(n, staged):
        qb, pv, kv_ext, a_rep, d_rep, g_tot, m_loc = staged
        sl = slice(n * C, (n + 1) * C)
        c_s = c_ref[...]
        m_s = m_ref[...][:, 0:1]
        inter = jnp.dot(qb, c_s.astype(BF16), preferred_element_type=F32)
        inter_log = a_rep + m_s
        m_row = jnp.maximum(inter_log, d_rep)
        f_intra = jnp.exp(d_rep - m_row)
        f_inter = jnp.exp(inter_log - m_row)
        den = f_intra * pv[:, dv:] + f_inter * inter[:, dv:]
        r = 1.0 / jnp.maximum(jnp.abs(den), jnp.exp(-m_row))
        for lo in range(0, dv, L):
            num = f_intra * pv[:, lo : lo + L] + f_inter * inter[:, lo : lo + L]
            o_ref[0, sl, lo : lo + L] = (num * r * og_ref[0, sl, lo : lo + L]).astype(o_ref.dtype)
        m_new = jnp.maximum(g_tot + m_s, m_loc)
        a_old = jnp.exp(g_tot + m_s - m_new)
        a_new = jnp.exp(m_loc - m_new)
        c_ref[...] = a_old * c_s + a_new * kv_ext
        m_ref[...] = jnp.broadcast_to(m_new, m_ref.shape)

    gvs = [gate_vectors(n) for n in range(n_chunks)]
    dms = [decay_matrix(gv) for gv in gvs]
    _lookahead(n_chunks, lambda n: local_matmuls(n, dms[n]), recurrent)


def _mlstm(q3, kt, v3, o3, gates_row):
    B, S, _ = v3.shape
    H, d = MLSTM_HEADS, MLSTM_HEAD_DIM
    ts = min(SEQ_TILE, S)
    tiles = S // ts
    col = pl.BlockSpec((1, ts, d), lambda b, h, c: (b, c, h))
    return pl.pallas_call(
        _mlstm_kernel,
        grid=(B, H, tiles),
        in_specs=[col, pl.BlockSpec((d, ts), lambda b, h, c: (h, b * tiles + c)), col, col,
                  pl.BlockSpec((2 * H, ts), lambda b, h, c: (0, b * tiles + c))],
        out_specs=pl.BlockSpec((1, ts, d), lambda b, h, c: (b, c, h)),
        out_shape=jax.ShapeDtypeStruct((B, S, H * d), BF16),
        scratch_shapes=[pltpu.VMEM((d, d + V7X_LANES), F32), pltpu.VMEM((1, V7X_LANES), F32)],
        compiler_params=_params(("parallel", "parallel", "arbitrary")),
        name="mlstm",
    )(q3, kt, v3, o3, gates_row)


def _even_mixer(x2d, B, S, w_in, rel_bias):
    tables = _retention_tables(S)
    q, kt, v, gate, d_nat, *d_dil = _even_in_proj(x2d, S, w_in.astype(BF16), tables[0], tables[1])
    y_r = _retention(q.reshape(B, S, -1), kt, v.reshape(B, S, -1), gate.reshape(B, S, -1), tables)
    y_d = _dilated_attention([d_nat.reshape(B, 1, S, -1)] + d_dil, _dsa_bias(rel_bias))
    return [y_r.reshape(B * S, RET_W), y_d.reshape(B * S, DSA_W)]


def _odd_mixer(x2d, B, S, w_in, gate_b, conv_w):
    H = MLSTM_HEADS
    wide = 4 * MLSTM_W
    wg = jnp.zeros((D_MODEL, V7X_LANES), F32).at[:, : 2 * H].set(w_in[:, wide:]).astype(BF16)
    gb = jnp.zeros((1, V7X_LANES), F32).at[0, : 2 * H].set(gate_b)
    q, kt, v, og, gates = _odd_in_proj(x2d, S, w_in.astype(BF16), wg, gb, conv_w)
    y = _mlstm(q.reshape(B, S, -1), kt, v.reshape(B, S, -1), og.reshape(B, S, -1), gates)
    return [y.reshape(B * S, MLSTM_W)]


def kernel(x, even_w_in, even_w_out, rel_bias, odd_w_in, odd_gate_b, odd_conv_w, odd_w_out, ffn_w_up, ffn_conv_w, ffn_conv_b, ffn_w_down, ln_g, ln_b):
    B, S, D = x.shape
    x2d = x.reshape(B * S, D)
    w_up, w_down = ffn_w_up.astype(BF16), ffn_w_down.astype(BF16)
    for layer in range(DEPTH):
        j = layer // 2
        if layer % 2 == 0:
            ys, w_out = _even_mixer(x2d, B, S, even_w_in[j], rel_bias), even_w_out[j]
        else:
            ys, w_out = _odd_mixer(x2d, B, S, odd_w_in[j], odd_gate_b[j], odd_conv_w[j]), odd_w_out[j]
        x2d = _layer_tail(x2d, ys, S, layer, w_out.astype(BF16), (ln_g[layer, 0], ln_b[layer, 0]), w_up,
                          ffn_conv_w[layer], ffn_conv_b[layer], w_down, (ln_g[layer, 1], ln_b[layer, 1]))
    return x2d.reshape(B, S, D)
```

```python
import functools
import math

import jax
import jax.numpy as jnp
from jax import lax
from jax.experimental import pallas as pl
from jax.experimental.pallas import tpu as pltpu

F32 = jnp.float32
BF16 = jnp.bfloat16

D_MODEL = 1024
DEPTH = 2
RET_HEADS = 4
RET_HEAD_DIM = 128
RET_CHUNK = 128
DSA_HEADS = 8
DSA_HEAD_DIM = 64
DSA_PATTERNS = ((128, 1), (512, 4), (2048, 16))
DSA_BLOCK = 128
T5_BUCKETS = 32
T5_MAX_DIST = 2048
MLSTM_HEADS = 4
MLSTM_HEAD_DIM = 256
MLSTM_CHUNK = 128
MLSTM_CONV = 4
D_FF = 2816
FFN_CONV = 3
LN_EPS = 1e-5
DEEPNORM_ALPHA = (2.0 * DEPTH) ** 0.25
LOG2_E = math.log2(math.e)

RET_W = RET_HEADS * RET_HEAD_DIM
DSA_W = DSA_HEADS * DSA_HEAD_DIM
MLSTM_W = MLSTM_HEADS * MLSTM_HEAD_DIM

V7X_LANES = 128
V7X_SUBLANES = 8
V7X_VMEM_BYTES = 64 * 1024 * 1024
VMEM_LIMIT_BYTES = V7X_VMEM_BYTES - 8 * 1024 * 1024

ROW_TILE = 512
PROJ_CHUNK = 512
FFN_CHUNK = 256
SEQ_TILE = 4096
DSA_TILE = DSA_BLOCK * DSA_PATTERNS[-1][1]
DSA_GROUP = 2

_NT = (((1,), (1,)), ((), ()))
_TN = (((0,), (0,)), ((), ()))


def _params(semantics):
    return pltpu.CompilerParams(dimension_semantics=semantics, vmem_limit_bytes=VMEM_LIMIT_BYTES)


def _resident(shape):
    nd = len(shape)
    return pl.BlockSpec(shape, lambda *_: (0,) * nd, pipeline_mode=pl.Buffered(1))


def _resident_layer(stacked_shape, layer):
    return pl.BlockSpec((None,) + tuple(stacked_shape[1:]), lambda *_: (layer, 0, 0), pipeline_mode=pl.Buffered(1))


def _layer_norm(z, g, b):
    mu = jnp.mean(z, -1, keepdims=True)
    zc = z - mu
    var = jnp.mean(zc * zc, -1, keepdims=True)
    return zc * lax.rsqrt(var + LN_EPS) * g + b


def _silu(z):
    return z * jax.nn.sigmoid(z)


def _log_sigmoid(x):
    return jnp.minimum(x, 0.0) - jnp.log1p(jnp.exp(-jnp.abs(x)))


def _lookahead(n, produce, consume, depth=1):
    staged = [produce(j) for j in range(min(depth, n))]
    for j in range(n):
        if j + depth < n:
            staged.append(produce(j + depth))
        consume(j, staged[j])
        staged[j] = None


def _shift_rows(x, prev, s):
    if s == 0:
        return x
    rolled = pltpu.roll(x, s, 0)
    head = pltpu.roll(prev, s, 0)
    rid = lax.broadcasted_iota(jnp.int32, (V7X_SUBLANES, x.shape[1]), 0)
    first = jnp.where(rid < s, head, rolled[:V7X_SUBLANES])
    return jnp.concatenate([first, rolled[V7X_SUBLANES:]], axis=0)


def _causal_dwconv(x, prev, w_ref, taps):
    acc = None
    for k in range(taps):
        term = _shift_rows(x, prev, taps - 1 - k) * w_ref[k : k + 1, :]
        acc = term if acc is None else acc + term
    return acc


def _even_in_proj_kernel(x_ref, w_ref, cos_ref, sin_ref, q_ref, kt_ref, v_ref, g_ref, *rest):
    d_refs, stage_ref = rest[:-1], rest[-1]
    xb = x_ref[...].astype(BF16)
    tm = xb.shape[0]
    cos = cos_ref[...]
    sin = sin_ref[...]
    tn, d = RET_W, RET_HEAD_DIM

    def consume(j, h):
        if j < 2:
            for hh in range(RET_HEADS):
                c = h[:, hh * d : (hh + 1) * d]
                rot = c * cos + pltpu.roll(c, d // 2, 1) * sin
                if j == 0:
                    q_ref[:, hh * d : (hh + 1) * d] = rot.astype(BF16)
                else:
                    kt_ref[hh * d : (hh + 1) * d, :] = rot.T.astype(BF16)
        elif j == 2:
            v_ref[...] = h.astype(BF16)
        elif j == 3:
            g_ref[...] = _silu(h).astype(BF16)
        else:
            if j == 4:
                h = h * (DSA_HEAD_DIM ** -0.5 * LOG2_E)
            lo = (j - 4) * tn
            d_refs[0][:, lo : lo + tn] = h.astype(BF16)
            src, dst = stage_ref.at[0], stage_ref.at[1]
            for s in range(tn // V7X_LANES):
                src[s] = h[:, s * V7X_LANES : (s + 1) * V7X_LANES]
            prev_r = 1
            for d_ref, (_, r) in zip(d_refs[1:], DSA_PATTERNS[1:]):
                step = r // prev_r
                n_rows = tm // r
                for rho_prev in range(prev_r):
                    for a in range(step):
                        rho = rho_prev + prev_r * a
                        for s in range(tn // V7X_LANES):
                            rows = src.at[s][pl.ds(rho_prev * (tm // prev_r) + a, n_rows, stride=step), :]
                            if r != DSA_PATTERNS[-1][1]:
                                dst[s, rho * n_rows : (rho + 1) * n_rows, :] = rows
                            d_ref[0, rho, :, lo + s * V7X_LANES : lo + (s + 1) * V7X_LANES] = rows.astype(BF16)
                src, dst, prev_r = dst, src, r

    _lookahead(w_ref.shape[1] // tn, lambda j: jnp.dot(xb, w_ref[:, j * tn : (j + 1) * tn], preferred_element_type=F32), consume)


def _even_in_proj(x2d, seq_len, w, cos_t, sin_t):
    T, K = x2d.shape
    tm = min(ROW_TILE, seq_len)
    tps = seq_len // tm
    B = T // seq_len
    row = lambda i: (i, 0)
    pos = pl.BlockSpec((tm, RET_HEAD_DIM), lambda i: (i % tps, 0))
    shapes = [(T, RET_W), (RET_W, T), (T, RET_W), (T, RET_W), (T, 3 * DSA_W)]
    dtypes = [BF16, BF16, BF16, BF16, BF16]
    specs = [pl.BlockSpec((tm, RET_W), row), pl.BlockSpec((RET_W, tm), lambda i: (0, i)),
             pl.BlockSpec((tm, RET_W), row), pl.BlockSpec((tm, RET_W), row), pl.BlockSpec((tm, 3 * DSA_W), row)]
    for _, r in DSA_PATTERNS[1:]:
        shapes.append((B, r, seq_len // r, 3 * DSA_W))
        dtypes.append(BF16)
        specs.append(pl.BlockSpec((1, r, tm // r, 3 * DSA_W), lambda i: (i // tps, 0, i % tps, 0)))
    return pl.pallas_call(
        _even_in_proj_kernel,
        grid=(T // tm,),
        in_specs=[pl.BlockSpec((tm, K), row), _resident(w.shape), pos, pos],
        out_specs=specs,
        out_shape=[jax.ShapeDtypeStruct(s, dt) for s, dt in zip(shapes, dtypes)],
        scratch_shapes=[pltpu.VMEM((2, RET_W // V7X_LANES, tm, V7X_LANES), F32)],
        compiler_params=_params(("parallel",)),
        name="even_in_proj",
    )(x2d, w, cos_t, sin_t)


def _odd_in_proj_kernel(x_ref, w_ref, wg_ref, gb_ref, cw_ref, q_ref, kt_ref, v_ref, o_ref, gt_ref, carry_ref, *, tiles_per_seq):
    tn = PROJ_CHUNK
    per = MLSTM_W // tn

    @pl.when(pl.program_id(0) % tiles_per_seq == 0)
    def _():
        carry_ref[...] = jnp.zeros_like(carry_ref)

    xb = x_ref[...].astype(BF16)
    tm = xb.shape[0]

    order = [c for pair in zip(range(2 * per), range(2 * per, 4 * per)) for c in pair]

    def produce(i):
        j = order[i]
        return jnp.dot(xb, w_ref[:, j * tn : (j + 1) * tn], preferred_element_type=F32)

    def consume(i, h):
        j = order[i]
        sl = slice(j * tn, (j + 1) * tn)
        if j < 2 * per:
            prev = carry_ref[:, sl]
            carry_ref[:, sl] = h[tm - V7X_SUBLANES :, :]
            act = _silu(_causal_dwconv(h, prev, cw_ref.at[:, sl], MLSTM_CONV))
            if j < per:
                q_ref[:, sl] = act.astype(BF16)
            else:
                kt_ref[(j - per) * tn : (j - per + 1) * tn, :] = (act * (MLSTM_HEAD_DIM ** -0.5)).T.astype(BF16)
        elif j < 3 * per:
            v_ref[:, (j - 2 * per) * tn : (j - 2 * per + 1) * tn] = h.astype(BF16)
        else:
            o_ref[:, (j - 3 * per) * tn : (j - 3 * per + 1) * tn] = jax.nn.sigmoid(h).astype(BF16)

    _lookahead(4 * per, produce, consume)
    gates = (jnp.dot(xb, wg_ref[...], preferred_element_type=F32) + gb_ref[...]).T[: 2 * MLSTM_HEADS, :]
    gate_id = lax.broadcasted_iota(jnp.int32, (2 * MLSTM_HEADS, 1), 0)
    gt_ref[...] = jnp.where(gate_id < MLSTM_HEADS, gates, _log_sigmoid(gates))


def _odd_in_proj(x2d, seq_len, w, wg, gb, conv_w):
    T, K = x2d.shape
    tm = min(ROW_TILE, seq_len)
    W = MLSTM_W
    cw = jnp.zeros((V7X_SUBLANES, 2 * W), F32).at[:MLSTM_CONV].set(conv_w)
    row = lambda i: (i, 0)
    shapes = ((T, W), (W, T), (T, W), (T, W), (2 * MLSTM_HEADS, T))
    dtypes = (BF16, BF16, BF16, BF16, F32)
    return pl.pallas_call(
        functools.partial(_odd_in_proj_kernel, tiles_per_seq=seq_len // tm),
        grid=(T // tm,),
        in_specs=[pl.BlockSpec((tm, K), row), _resident(w.shape), _resident(wg.shape), _resident(gb.shape),
                  _resident(cw.shape)],
        out_specs=[pl.BlockSpec((tm, W), row), pl.BlockSpec((W, tm), lambda i: (0, i)), pl.BlockSpec((tm, W), row),
                   pl.BlockSpec((tm, W), row), pl.BlockSpec((2 * MLSTM_HEADS, tm), lambda i: (0, i))],
        out_shape=[jax.ShapeDtypeStruct(s, dt) for s, dt in zip(shapes, dtypes)],
        scratch_shapes=[pltpu.VMEM((V7X_SUBLANES, 2 * MLSTM_W), F32)],
        compiler_params=_params(("arbitrary",)),
        name="odd_in_proj",
    )(x2d, w, wg, gb, cw)


def _tail_kernel(*refs, n_y, tiles_per_seq):
    x_ref = refs[0]
    y_refs = refs[1 : 1 + n_y]
    (wout_ref, g1_ref, b1_ref, wup_ref, cw_ref, cb_ref, wdn_ref, g_ref, b_ref,
     o_ref, carry_ref, act_ref, x1_ref) = refs[1 + n_y :]

    @pl.when(pl.program_id(0) % tiles_per_seq == 0)
    def _():
        carry_ref[...] = jnp.zeros_like(carry_ref)

    tm = x_ref.shape[0]
    for lo in range(0, tm, tm // 2):
        rows = slice(lo, lo + tm // 2)
        acc = None
        row = 0
        for y_ref in y_refs:
            kw = y_ref.shape[1]
            part = jnp.dot(y_ref[rows, :], wout_ref[row : row + kw, :], preferred_element_type=F32)
            acc = part if acc is None else acc + part
            row += kw
        x1_ref[rows, :] = _layer_norm(DEEPNORM_ALPHA * x_ref[rows, :] + acc, g1_ref[...], b1_ref[...])

    xb = x1_ref[...].astype(BF16)

    def up_dots(j):
        lo = j * FFN_CHUNK
        return (jnp.dot(xb, wup_ref[:, lo : lo + FFN_CHUNK], preferred_element_type=F32),
                jnp.dot(xb, wup_ref[:, D_FF + lo : D_FF + lo + FFN_CHUNK], preferred_element_type=F32))

    def activate(j, gate_up):
        gate, up = gate_up
        sl = slice(j * FFN_CHUNK, (j + 1) * FFN_CHUNK)
        prev = carry_ref[:, sl]
        carry_ref[:, sl] = gate[tm - V7X_SUBLANES :, :]
        conv = _causal_dwconv(gate, prev, cw_ref.at[:, sl], FFN_CONV) + cb_ref[:, sl]
        act_ref[:, sl] = (_silu(conv) * up).astype(BF16)

    _lookahead(D_FF // FFN_CHUNK, up_dots, activate)
    for lo in range(0, tm, tm // 2):
        rows = slice(lo, lo + tm // 2)
        ffn = jnp.dot(act_ref[rows, :], wdn_ref[...], preferred_element_type=F32)
        z = DEEPNORM_ALPHA * x1_ref[rows, :] + ffn
        o_ref[rows, :] = _layer_norm(z, g_ref[...], b_ref[...])


def _layer_tail(x2d, ys, seq_len, layer, w_out, ln1, w_up, conv_w, conv_b, w_down, ln2):
    T, D = x2d.shape
    tm = min(ROW_TILE, seq_len)
    cw = jnp.zeros((V7X_SUBLANES, D_FF), F32).at[:FFN_CONV].set(conv_w)
    row = lambda i: (i, 0)
    vec = lambda v: v.reshape(1, -1)
    return pl.pallas_call(
        functools.partial(_tail_kernel, n_y=len(ys), tiles_per_seq=seq_len // tm),
        grid=(T // tm,),
        in_specs=[pl.BlockSpec((tm, D), row)]
        + [pl.BlockSpec((tm, y.shape[1]), row) for y in ys]
        + [_resident(w_out.shape), _resident((1, D)), _resident((1, D)),
           _resident_layer(w_up.shape, layer), _resident(cw.shape), _resident((1, D_FF)),
           _resident_layer(w_down.shape, layer), _resident((1, D)), _resident((1, D))],
        out_specs=pl.BlockSpec((tm, D), row),
        out_shape=jax.ShapeDtypeStruct((T, D), F32),
        scratch_shapes=[pltpu.VMEM((V7X_SUBLANES, D_FF), F32), pltpu.VMEM((tm, D_FF), BF16), pltpu.VMEM((tm, D), F32)],
        compiler_params=_params(("arbitrary",)),
        name="layer_tail",
    )(x2d, *ys, w_out, vec(ln1[0]), vec(ln1[1]), w_up, cw, vec(conv_b), w_down, vec(ln2[0]), vec(ln2[1]))


def _retention_tables(seq_len):
    H, C, d = RET_HEADS, RET_CHUNK, RET_HEAD_DIM
    scale = d ** -0.5
    log_gamma = jnp.log1p(-jnp.exp2(-5.0 - jnp.arange(H, dtype=F32)))
    idx = jnp.arange(C, dtype=F32)
    rel = idx[:, None] - idx[None, :]
    decay = jnp.where(rel >= 0, jnp.exp(log_gamma[:, None, None] * jnp.maximum(rel, 0.0)), 0.0) * scale
    k_w = jnp.exp(log_gamma[:, None] * (C - 1 - idx)[None, :]) * scale
    q_w = jnp.exp(log_gamma[:, None] * (idx + 1.0)[None, :])
    kw_t = jnp.broadcast_to(k_w[:, :, None], (H, C, d))
    qw_t = jnp.broadcast_to(q_w[:, :, None], (H, C, d))
    cd_t = jnp.broadcast_to(jnp.exp(log_gamma * C)[:, None, None], (H, 1, d))
    inv = 1.0 / (10000.0 ** (jnp.arange(0, d, 2, dtype=F32) / d))
    ang = jnp.arange(seq_len, dtype=F32)[:, None] * inv[None, :]
    cos, sin = jnp.cos(ang), jnp.sin(ang)
    cos_t = jnp.concatenate([cos, cos], -1)
    sin_t = jnp.concatenate([-sin, sin], -1)
    return cos_t, sin_t, decay, kw_t, qw_t, cd_t


def _retention_kernel(q_ref, kt_ref, v_ref, g_ref, dec_ref, kw_ref, qw_ref, cd_ref, o_ref, r_ref):
    @pl.when(pl.program_id(2) == 0)
    def _():
        r_ref[...] = jnp.zeros_like(r_ref)

    C = RET_CHUNK
    decay = dec_ref[0]
    kw = kw_ref[0]
    qw = qw_ref[0]
    cd = cd_ref[0]

    def local(n):
        sl = slice(n * C, (n + 1) * C)
        qb = q_ref[0, sl, :]
        kt = kt_ref[:, sl]
        vb = v_ref[0, sl, :]
        scores = jnp.dot(qb, kt, preferred_element_type=F32) * decay
        y_intra = jnp.dot(scores.astype(BF16), vb, preferred_element_type=F32)
        kv = jnp.dot(kt, (vb.astype(F32) * kw).astype(BF16), preferred_element_type=F32)
        q_dec = (qb.astype(F32) * qw).astype(BF16)
        return y_intra, kv, q_dec

    def recurrent(n, staged):
        y_intra, kv, q_dec = staged
        sl = slice(n * C, (n + 1) * C)
        r = r_ref[...]
        y = y_intra + jnp.dot(q_dec, r.astype(BF16), preferred_element_type=F32)
        r_ref[...] = cd * r + kv
        mu = jnp.mean(y, -1, keepdims=True)
        yc = y - mu
        var = jnp.mean(yc * yc, -1, keepdims=True)
        o_ref[0, sl, :] = (yc * lax.rsqrt(var + LN_EPS) * g_ref[0, sl, :]).astype(o_ref.dtype)

    _lookahead(q_ref.shape[1] // C, local, recurrent)


def _retention(q3, kt, v3, g3, tables):
    B, S, _ = v3.shape
    H, d, C = RET_HEADS, RET_HEAD_DIM, RET_CHUNK
    ts = min(SEQ_TILE, S)
    tiles = S // ts
    _, _, decay, kw_t, qw_t, cd_t = tables
    col = pl.BlockSpec((1, ts, d), lambda b, h, c: (b, c, h))
    per_head = lambda shape: pl.BlockSpec((1,) + shape, lambda b, h, c: (h, 0, 0))
    return pl.pallas_call(
        _retention_kernel,
        grid=(B, H, tiles),
        in_specs=[col, pl.BlockSpec((d, ts), lambda b, h, c: (h, b * tiles + c)), col, col,
                  per_head((C, C)), per_head((C, d)), per_head((C, d)), per_head((1, d))],
        out_specs=pl.BlockSpec((1, ts, d), lambda b, h, c: (b, c, h)),
        out_shape=jax.ShapeDtypeStruct((B, S, H * d), BF16),
        scratch_shapes=[pltpu.VMEM((d, d), F32)],
        compiler_params=_params(("parallel", "parallel", "arbitrary")),
        name="retention",
    )(q3, kt, v3, g3, decay, kw_t, qw_t, cd_t)


def _t5_bucket(dist):
    exact = T5_BUCKETS // 2
    n = jnp.maximum(dist, 0)
    large = exact + (jnp.log(jnp.maximum(n, 1).astype(F32) / exact) / math.log(T5_MAX_DIST / exact) * (T5_BUCKETS - exact)).astype(jnp.int32)
    large = jnp.minimum(large, T5_BUCKETS - 1)
    return jnp.where(n < exact, n, large)


def _dsa_bucket_tables():
    blk = DSA_BLOCK
    qi = jnp.arange(blk)[:, None]
    kj = jnp.arange(2 * blk)[None, :]
    tabs = []
    for window, dilation in DSA_PATTERNS:
        assert window // dilation <= blk
        dist = qi + blk - kj
        valid = (dist >= 0) & (dist <= window // dilation)
        tabs.append(jnp.where(valid, _t5_bucket(dist * dilation), -1))
    return jnp.stack(tabs).astype(jnp.int32)


def _dsa_bias_kernel(rb_ref, bk_ref, o_ref):
    blk = DSA_BLOCK
    rows = 4 * V7X_SUBLANES
    masked = jnp.full((rows, blk), -jnp.inf, F32)
    for lo in range(0, blk, rows):
        bk = bk_ref[0, lo : lo + rows, :]
        accs = [jnp.full(bk.shape, -jnp.inf, F32) for _ in range(DSA_HEADS)]
        for b in range(T5_BUCKETS):
            hit = bk == b
            accs = [jnp.where(hit, rb_ref[b, h] * LOG2_E, acc) for h, acc in enumerate(accs)]
        for h, acc in enumerate(accs):
            o_ref[0, 0, h, lo : lo + rows, :] = acc
            o_ref[0, 1, h, lo : lo + rows, :] = jnp.concatenate([acc[:, blk:], masked], axis=1)


def _dsa_bias(rel_bias):
    buckets = _dsa_bucket_tables()
    P, blk = len(DSA_PATTERNS), DSA_BLOCK
    return pl.pallas_call(
        _dsa_bias_kernel,
        grid=(P,),
        in_specs=[pl.BlockSpec(memory_space=pltpu.SMEM), pl.BlockSpec((1, blk, 2 * blk), lambda p: (p, 0, 0))],
        out_specs=pl.BlockSpec((1, 2, DSA_HEADS, blk, 2 * blk), lambda p: (p, 0, 0, 0, 0)),
        out_shape=jax.ShapeDtypeStruct((P, 2, DSA_HEADS, blk, 2 * blk), F32),
        compiler_params=_params(("parallel",)),
        name="dsa_bias",
    )(rel_bias, buckets)


def _dsa_kernel(*refs):
    P = len(DSA_PATTERNS)
    q_refs, k_refs, v_refs = refs[:P], refs[P : 2 * P], refs[2 * P : 3 * P]
    bias_ref, o_ref, acc_o, acc_m, acc_l = refs[3 * P :]
    blk = DSA_BLOCK
    tq = o_ref.shape[1]
    t = pl.program_id(2)
    lane = lax.broadcasted_iota(jnp.int32, (1, V7X_LANES), 1)
    head0 = lane < DSA_HEAD_DIM

    def block_logits(p, r, idx):
        blocks_per_residue = tq // (blk * r)
        rho = idx // blocks_per_residue
        nb = idx % blocks_per_residue
        l0 = pl.multiple_of(t * (tq // r) + nb * blk, blk)
        first = l0 == 0
        k_row = pl.multiple_of(jnp.where(first, l0, l0 - blk), blk)
        qb = q_refs[p][0, rho, pl.ds(l0, blk), :]
        zero = jnp.zeros_like(qb)
        q2 = jnp.concatenate([jnp.where(head0, qb, zero), jnp.where(head0, zero, qb)], axis=0)
        bias = bias_ref[p, pl.ds(first.astype(jnp.int32), 1), 0][0]
        kb = k_refs[p][0, rho, pl.ds(k_row, 2 * blk), :]
        logits = lax.dot_general(q2, kb, _NT, preferred_element_type=F32) + bias
        start = rho + nb * (blk * r)
        rows = pl.ds(start, blk, stride=r) if r > 1 else pl.ds(start, blk)
        return logits, (rho, k_row), rows

    def block_output(p, logits, key_rows, rows):
        rho, k_row = key_rows
        m = jnp.max(logits, -1, keepdims=True)
        e = jnp.exp2(logits - m).astype(BF16)
        vb = v_refs[p][0, rho, pl.ds(k_row, 2 * blk), :]
        o2 = jnp.dot(e, jnp.concatenate([vb, jnp.ones_like(vb)], axis=1), preferred_element_type=F32)
        acc_o.at[p][rows, :] = jnp.where(head0, o2[:blk, :V7X_LANES], o2[blk:, :V7X_LANES])
        acc_m.at[p][rows, :] = jnp.where(head0, m[:blk], m[blk:])
        acc_l.at[p][rows, :] = jnp.where(head0, o2[:blk, V7X_LANES:], o2[blk:, V7X_LANES:])

    blocks = [(p, r, idx) for p, (_, r) in enumerate(DSA_PATTERNS) for idx in range(tq // blk)]
    staged = [block_logits(*blocks[i]) for i in range(DSA_GROUP)]
    for i, (p, _, _) in enumerate(blocks):
        if i + DSA_GROUP < len(blocks):
            staged.append(block_logits(*blocks[i + DSA_GROUP]))
        block_output(p, *staged[i])

    for c in range(tq // blk):
        sl = slice(c * blk, (c + 1) * blk)
        ms = [acc_m[bi, sl, :] for bi in range(len(DSA_PATTERNS))]
        m_all = functools.reduce(jnp.maximum, ms)
        num = den = None
        for bi, m in enumerate(ms):
            w = jnp.exp2(m - m_all)
            n_i = w * acc_o[bi, sl, :]
            d_i = w * acc_l[bi, sl, :]
            num = n_i if num is None else num + n_i
            den = d_i if den is None else den + d_i
        o_ref[0, sl, :] = (num / den).astype(o_ref.dtype)


def _dilated_attention(ds, bias):
    B, _, S, _ = ds[0].shape
    pairs = DSA_W // V7X_LANES
    tq = DSA_TILE
    assert S % tq == 0 and S >= 2 * tq
    P, blk = len(DSA_PATTERNS), DSA_BLOCK
    bias5 = bias.reshape(P, 2, pairs, 2 * blk, 2 * blk)

    def seq(r, off):
        return pl.BlockSpec((1, r, S // r, V7X_LANES), lambda b, p, t: (b, 0, 0, off + p))

    return pl.pallas_call(
        _dsa_kernel,
        grid=(B, pairs, S // tq),
        in_specs=[seq(r, off) for off in (0, pairs, 2 * pairs) for _, r in DSA_PATTERNS]
        + [pl.BlockSpec((P, 2, 1, 2 * blk, 2 * blk), lambda b, p, t: (0, 0, p, 0, 0))],
        out_specs=pl.BlockSpec((1, tq, V7X_LANES), lambda b, p, t: (b, t, p)),
        out_shape=jax.ShapeDtypeStruct((B, S, DSA_W), BF16),
        scratch_shapes=[pltpu.VMEM((P, tq, V7X_LANES), F32)] * 3,
        compiler_params=_params(("parallel", "parallel", "arbitrary")),
        name="dilated_attention",
    )(*ds, *ds, *ds, bias5)


def _mlstm_kernel(q_ref, kt_ref, v_ref, og_ref, gr_ref, o_ref, c_ref, m_ref):
    h = pl.program_id(1)

    @pl.when(pl.program_id(2) == 0)
    def _():
        c_ref[...] = jnp.zeros_like(c_ref)
        m_ref[...] = jnp.zeros_like(m_ref)

    C = MLSTM_CHUNK
    dv = MLSTM_HEAD_DIM
    L = V7X_LANES
    n_chunks = q_ref.shape[1] // C
    ri = lax.broadcasted_iota(jnp.int32, (C, C), 0)
    ci = lax.broadcasted_iota(jnp.int32, (C, C), 1)
    causal = ci <= ri
    eye = ci == ri
    rep = lambda col: jnp.broadcast_to(col, (C, L))

    def gate_vectors(n):
        sl = slice(n * C, (n + 1) * C)
        gates = gr_ref[:, sl]
        gate_id = lax.broadcasted_iota(jnp.int32, (2 * MLSTM_HEADS, 1), 0)
        li_row = jnp.sum(jnp.where(gate_id == h, gates, 0.0), 0, keepdims=True)
        lf_row = jnp.sum(jnp.where(gate_id == MLSTM_HEADS + h, gates, 0.0), 0, keepdims=True)
        li_col = jnp.sum(jnp.where(eye, li_row, 0.0), -1, keepdims=True)
        a_col = jnp.sum(jnp.where(causal, lf_row, 0.0), -1, keepdims=True)
        g_tot = jnp.sum(lf_row, -1, keepdims=True)
        return li_row, li_col, a_col, g_tot

    def decay_matrix(gv):
        li_row, li_col, a_col, g_tot = gv
        a_rep = rep(a_col)
        a_row = jnp.sum(jnp.where(eye, a_rep, 0.0), 0, keepdims=True)
        w_state = g_tot - a_col + li_col
        m_loc = jnp.max(w_state, 0, keepdims=True)
        ew_rep = rep(jnp.exp(w_state - m_loc))
        dmat = jnp.where(causal, a_rep - a_row + li_row, -jnp.inf)
        d_max = jnp.max(dmat, -1, keepdims=True)
        return a_rep, g_tot, m_loc, ew_rep, dmat, d_max

    def local_matmuls(n, dm, scalars):
        a_rep, _, _, ew_rep, dmat, d_max = dm
        m_s, a_old, a_new = scalars
        ew_rep = ew_rep * a_new
        sl = slice(n * C, (n + 1) * C)
        qb = q_ref[0, sl, :]
        kt = kt_ref[:, sl]
        vb = v_ref[0, sl, :]
        d_rep = rep(d_max)
        p = (jnp.exp(dmat - d_rep) * jnp.dot(qb, kt, preferred_element_type=F32)).astype(BF16)
        pv = jnp.dot(p, jnp.concatenate([vb, jnp.ones((C, L), BF16)], axis=1), preferred_element_type=F32)
        vw = jnp.concatenate([vb.astype(F32) * jnp.concatenate([ew_rep] * (dv // L), axis=1), ew_rep], axis=1)
        kv_ext = jnp.dot(kt, vw.astype(BF16), preferred_element_type=F32)
        return qb, pv, kv_ext, a_rep, d_rep, m_s, a_old

    def recurrent(n, staged):
        qb, pv, kv_ext, a_rep, d_rep, m_s, a_old = staged
        sl = slice(n * C, (n + 1) * C)
        c_s = c_ref[...]
        inter =jnp.dot(qb, c_s.astype(BF16), preferred_element_type=F32)
        inter_log = a_rep + m_s
        m_row = jnp.maximum(inter_log, d_rep)
        f_intra = jnp.exp(d_rep - m_row)
        f_inter = jnp.exp(inter_log - m_row)
        den = f_intra * pv[:, dv:] + f_inter * inter[:, dv:]
        r = 1.0 / jnp.maximum(jnp.abs(den), jnp.exp(-m_row))
        for lo in range(0, dv, L):
            num = f_intra * pv[:, lo : lo + L] + f_inter * inter[:, lo : lo + L]
            o_ref[0, sl, lo : lo + L] = (num * r * og_ref[0, sl, lo : lo + L]).astype(o_ref.dtype)
        c_ref[...] = a_old * c_s + kv_ext

    gvs = [gate_vectors(n) for n in range(n_chunks)]
    dms = [decay_matrix(gv) for gv in gvs]
    m_s = m_ref[...][:, 0:1]
    scalars = []
    for _, g_tot, m_loc, _, _, _ in dms:
        m_new = jnp.maximum(g_tot + m_s, m_loc)
        scalars.append((m_s, jnp.exp(g_tot + m_s - m_new), jnp.exp(m_loc - m_new)))
        m_s = m_new
    m_ref[...] = jnp.broadcast_to(m_s, m_ref.shape)
    _lookahead(n_chunks, lambda n: local_matmuls(n, dms[n], scalars[n]), recurrent)


def _mlstm(q3, kt, v3, o3, gates_row):
    B, S, _ = v3.shape
    H, d = MLSTM_HEADS, MLSTM_HEAD_DIM
    ts = min(SEQ_TILE, S)
    tiles = S // ts
    col = pl.BlockSpec((1, ts, d), lambda b, h, c: (b, c, h))
    return pl.pallas_call(
        _mlstm_kernel,
        grid=(B, H, tiles),
        in_specs=[col, pl.BlockSpec((d, ts), lambda b, h, c: (h, b * tiles + c)), col, col,
                  pl.BlockSpec((2 * H, ts), lambda b, h, c: (0, b * tiles + c))],
        out_specs=pl.BlockSpec((1, ts, d), lambda b, h, c: (b, c, h)),
        out_shape=jax.ShapeDtypeStruct((B, S, H * d), BF16),
        scratch_shapes=[pltpu.VMEM((d, d + V7X_LANES), F32), pltpu.VMEM((1, V7X_LANES), F32)],
        compiler_params=_params(("parallel", "parallel", "arbitrary")),
        name="mlstm",
    )(q3, kt, v3, o3, gates_row)


def _even_mixer(x2d, B, S, w_in, rel_bias):
    tables = _retention_tables(S)
    q, kt, v, gate, d_nat, *d_dil = _even_in_proj(x2d, S, w_in.astype(BF16), tables[0], tables[1])
    y_r = _retention(q.reshape(B, S, -1), kt, v.reshape(B, S, -1), gate.reshape(B, S, -1), tables)
    y_d = _dilated_attention([d_nat.reshape(B, 1, S, -1)] + d_dil, _dsa_bias(rel_bias))
    return [y_r.reshape(B * S, RET_W), y_d.reshape(B * S, DSA_W)]


def _odd_mixer(x2d, B, S, w_in, gate_b, conv_w):
    H = MLSTM_HEADS
    wide = 4 * MLSTM_W
    wg = jnp.zeros((D_MODEL, V7X_LANES), F32).at[:, : 2 * H].set(w_in[:, wide:]).astype(BF16)
    gb = jnp.zeros((1, V7X_LANES), F32).at[0, : 2 * H].set(gate_b)
    q, kt, v, og, gates = _odd_in_proj(x2d, S, w_in.astype(BF16), wg, gb, conv_w)
    y = _mlstm(q.reshape(B, S, -1), kt, v.reshape(B, S, -1), og.reshape(B, S, -1), gates)
    return [y.reshape(B * S, MLSTM_W)]


def kernel(x, even_w_in, even_w_out, rel_bias, odd_w_in, odd_gate_b, odd_conv_w, odd_w_out, ffn_w_up, ffn_conv_w, ffn_conv_b, ffn_w_down, ln_g, ln_b):
    B, S, D = x.shape
    x2d = x.reshape(B * S, D)
    w_up, w_down = ffn_w_up.astype(BF16), ffn_w_down.astype(BF16)
    for layer in range(DEPTH):
        j = layer // 2
        if layer % 2 == 0:
            ys, w_out = _even_mixer(x2d, B, S, even_w_in[j], rel_bias), even_w_out[j]
        else:
            ys, w_out = _odd_mixer(x2d, B, S, odd_w_in[j], odd_gate_b[j], odd_conv_w[j]), odd_w_out[j]
        x2d = _layer_tail(x2d, ys, S, layer, w_out.astype(BF16), (ln_g[layer, 0], ln_b[layer, 0]), w_up,
                          ffn_conv_w[layer], ffn_conv_b[layer], w_down, (ln_g[layer, 1], ln_b[layer, 1]))
    return x2d.reshape(B, S, D)
```

```python
import functools
import math

import jax
import jax.numpy as jnp
from jax import lax
from jax.experimental import pallas as pl
from jax.experimental.pallas import tpu as pltpu

F32 = jnp.float32
BF16 = jnp.bfloat16

D_MODEL = 1024
DEPTH = 2
RET_HEADS = 4
RET_HEAD_DIM = 128
RET_CHUNK = 128
DSA_HEADS = 8
DSA_HEAD_DIM = 64
DSA_PATTERNS = ((128, 1), (512, 4), (2048, 16))
DSA_BLOCK = 128
T5_BUCKETS = 32
T5_MAX_DIST = 2048
MLSTM_HEADS = 4
MLSTM_HEAD_DIM = 256
MLSTM_CHUNK = 128
MLSTM_CONV = 4
D_FF = 2816
FFN_CONV = 3
LN_EPS = 1e-5
DEEPNORM_ALPHA = (2.0 * DEPTH) ** 0.25
LOG2_E = math.log2(math.e)

RET_W = RET_HEADS * RET_HEAD_DIM
DSA_W = DSA_HEADS * DSA_HEAD_DIM
MLSTM_W = MLSTM_HEADS * MLSTM_HEAD_DIM

V7X_LANES = 128
V7X_SUBLANES = 8
V7X_VMEM_BYTES = 64 * 1024 * 1024
VMEM_LIMIT_BYTES = V7X_VMEM_BYTES - 8 * 1024 * 1024

ROW_TILE = 512
PROJ_CHUNK = 512
FFN_CHUNK = 256
SEQ_TILE = 4096
DSA_TILE = DSA_BLOCK * DSA_PATTERNS[-1][1]
DSA_GROUP = 2

_NT = (((1,), (1,)), ((), ()))
_TN = (((0,), (0,)), ((), ()))


def _params(semantics):
    return pltpu.CompilerParams(dimension_semantics=semantics, vmem_limit_bytes=VMEM_LIMIT_BYTES)


def _resident(shape):
    nd = len(shape)
    return pl.BlockSpec(shape, lambda *_: (0,) * nd, pipeline_mode=pl.Buffered(1))


def _resident_layer(stacked_shape, layer):
    return pl.BlockSpec((None,) + tuple(stacked_shape[1:]), lambda *_: (layer, 0, 0), pipeline_mode=pl.Buffered(1))


def _layer_norm(z, g, b):
    mu = jnp.mean(z, -1, keepdims=True)
    zc = z - mu
    var = jnp.mean(zc * zc, -1, keepdims=True)
    return zc * lax.rsqrt(var + LN_EPS) * g + b


def _silu(z):
    return z * jax.nn.sigmoid(z)


def _log_sigmoid(x):
    return jnp.minimum(x, 0.0) - jnp.log1p(jnp.exp(-jnp.abs(x)))


def _lookahead(n, produce, consume, depth=1):
    staged = [produce(j) for j in range(min(depth, n))]
    for j in range(n):
        if j + depth < n:
            staged.append(produce(j + depth))
        consume(j, staged[j])
        staged[j] = None


def _shift_rows(x, prev, s):
    if s == 0:
        return x
    rolled = pltpu.roll(x, s, 0)
    head = pltpu.roll(prev, s, 0)
    rid = lax.broadcasted_iota(jnp.int32, (V7X_SUBLANES, x.shape[1]), 0)
    first = jnp.where(rid < s, head, rolled[:V7X_SUBLANES])
    return jnp.concatenate([first, rolled[V7X_SUBLANES:]], axis=0)


def _causal_dwconv(x, prev, w_ref, taps):
    acc = None
    for k in range(taps):
        term = _shift_rows(x, prev, taps - 1 - k) * w_ref[k : k + 1, :]
        acc = term if acc is None else acc + term
    return acc


def _even_in_proj_kernel(x_ref, w_ref, cos_ref, sin_ref, q_ref, kt_ref, v_ref, g_ref, *rest):
    d_refs, stage_ref = rest[:-1], rest[-1]
    xb = x_ref[...].astype(BF16)
    tm = xb.shape[0]
    cos = cos_ref[...]
    sin = sin_ref[...]
    tn, d = RET_W, RET_HEAD_DIM

    def consume(j, h):
        if j < 2:
            for hh in range(RET_HEADS):
                c = h[:, hh * d : (hh + 1) * d]
                rot = c * cos + pltpu.roll(c, d // 2, 1) * sin
                if j == 0:
                    q_ref[:, hh * d : (hh + 1) * d] = rot.astype(BF16)
                else:
                    kt_ref[hh * d : (hh + 1) * d, :] = rot.T.astype(BF16)
        elif j == 2:
            v_ref[...] = h.astype(BF16)
        elif j == 3:
            g_ref[...] = _silu(h).astype(BF16)
        else:
            if j == 4:
                h = h * (DSA_HEAD_DIM ** -0.5 * LOG2_E)
            lo = (j - 4) * tn
            d_refs[0][:, lo : lo + tn] = h.astype(BF16)
            src, dst = stage_ref.at[0], stage_ref.at[1]
            for s in range(tn // V7X_LANES):
                src[s] = h[:, s * V7X_LANES : (s + 1) * V7X_LANES]
            prev_r = 1
            for d_ref, (_, r) in zip(d_refs[1:], DSA_PATTERNS[1:]):
                step = r // prev_r
                n_rows = tm // r
                for rho_prev in range(prev_r):
                    for a in range(step):
                        rho = rho_prev + prev_r * a
                        for s in range(tn // V7X_LANES):
                            rows = src.at[s][pl.ds(rho_prev * (tm // prev_r) + a, n_rows, stride=step), :]
                            if r != DSA_PATTERNS[-1][1]:
                                dst[s, rho * n_rows : (rho + 1) * n_rows, :] = rows
                            d_ref[0, rho, :, lo + s * V7X_LANES : lo + (s + 1) * V7X_LANES] = rows.astype(BF16)
                src, dst, prev_r = dst, src, r

    _lookahead(w_ref.shape[1] // tn, lambda j: jnp.dot(xb, w_ref[:, j * tn : (j + 1) * tn], preferred_element_type=F32), consume)


def _even_in_proj(x2d, seq_len, w, cos_t, sin_t):
    T, K = x2d.shape
    tm = min(ROW_TILE, seq_len)
    tps = seq_len // tm
    B = T // seq_len
    row = lambda i: (i, 0)
    pos = pl.BlockSpec((tm, RET_HEAD_DIM), lambda i: (i % tps, 0))
    shapes = [(T, RET_W), (RET_W, T), (T, RET_W), (T, RET_W), (T, 3 * DSA_W)]
    dtypes = [BF16, BF16, BF16, BF16, BF16]
    specs = [pl.BlockSpec((tm, RET_W), row), pl.BlockSpec((RET_W, tm), lambda i: (0, i)),
             pl.BlockSpec((tm, RET_W), row), pl.BlockSpec((tm, RET_W), row), pl.BlockSpec((tm, 3 * DSA_W), row)]
    for _, r in DSA_PATTERNS[1:]:
        shapes.append((B, r, seq_len // r, 3 * DSA_W))
        dtypes.append(BF16)
        specs.append(pl.BlockSpec((1, r, tm // r, 3 * DSA_W), lambda i: (i // tps, 0, i % tps, 0)))
    return pl.pallas_call(
        _even_in_proj_kernel,
        grid=(T // tm,),
        in_specs=[pl.BlockSpec((tm, K), row), _resident(w.shape), pos, pos],
        out_specs=specs,
        out_shape=[jax.ShapeDtypeStruct(s, dt) for s, dt in zip(shapes, dtypes)],
        scratch_shapes=[pltpu.VMEM((2, RET_W // V7X_LANES, tm, V7X_LANES), F32)],
        compiler_params=_params(("parallel",)),
        name="even_in_proj",
    )(x2d, w, cos_t, sin_t)


def _odd_in_proj_kernel(x_ref, w_ref, wg_ref, gb_ref, cw_ref, q_ref, kt_ref, v_ref, o_ref, gt_ref, carry_ref, *, tiles_per_seq):
    tn = PROJ_CHUNK
    per = MLSTM_W // tn

    @pl.when(pl.program_id(0) % tiles_per_seq == 0)
    def _():
        carry_ref[...] = jnp.zeros_like(carry_ref)

    xb = x_ref[...].astype(BF16)
    tm = xb.shape[0]

    order = [c for pair in zip(range(2 * per), range(2 * per, 4 * per)) for c in pair]

    def produce(i):
        j = order[i]
        return jnp.dot(xb, w_ref[:, j * tn : (j + 1) * tn], preferred_element_type=F32)

    def consume(i, h):
        j = order[i]
        sl = slice(j * tn, (j + 1) * tn)
        if j < 2 * per:
            prev = carry_ref[:, sl]
            carry_ref[:, sl] = h[tm - V7X_SUBLANES :, :]
            act = _silu(_causal_dwconv(h, prev, cw_ref.at[:, sl], MLSTM_CONV))
            if j < per:
                q_ref[:, sl] = act.astype(BF16)
            else:
                kt_ref[(j - per) * tn : (j - per + 1) * tn, :] = (act * (MLSTM_HEAD_DIM ** -0.5)).T.astype(BF16)
        elif j < 3 * per:
            v_ref[:, (j - 2 * per) * tn : (j - 2 * per + 1) * tn] = h.astype(BF16)
        else:
            o_ref[:, (j - 3 * per) * tn : (j - 3 * per + 1) * tn] = jax.nn.sigmoid(h).astype(BF16)

    _lookahead(4 * per, produce, consume)
    gates = (jnp.dot(xb, wg_ref[...], preferred_element_type=F32) + gb_ref[...]).T[: 2 * MLSTM_HEADS, :]
    gate_id = lax.broadcasted_iota(jnp.int32, (2 * MLSTM_HEADS, 1), 0)
    gt_ref[...] = jnp.where(gate_id < MLSTM_HEADS, gates, _log_sigmoid(gates))


def _odd_in_proj(x2d, seq_len, w, wg, gb, conv_w):
    T, K = x2d.shape
    tm = min(ROW_TILE, seq_len)
    W = MLSTM_W
    cw = jnp.zeros((V7X_SUBLANES, 2 * W), F32).at[:MLSTM_CONV].set(conv_w)
    row = lambda i: (i, 0)
    shapes = ((T, W), (W, T), (T, W), (T, W), (2 * MLSTM_HEADS, T))
    dtypes = (BF16, BF16, BF16, BF16, F32)
    return pl.pallas_call(
        functools.partial(_odd_in_proj_kernel, tiles_per_seq=seq_len // tm),
        grid=(T // tm,),
        in_specs=[pl.BlockSpec((tm, K), row), _resident(w.shape), _resident(wg.shape), _resident(gb.shape),
                  _resident(cw.shape)],
        out_specs=[pl.BlockSpec((tm, W), row), pl.BlockSpec((W, tm), lambda i: (0, i)), pl.BlockSpec((tm, W), row),
                   pl.BlockSpec((tm, W), row), pl.BlockSpec((2 * MLSTM_HEADS, tm), lambda i: (0, i))],
        out_shape=[jax.ShapeDtypeStruct(s, dt) for s, dt in zip(shapes, dtypes)],
        scratch_shapes=[pltpu.VMEM((V7X_SUBLANES, 2 * MLSTM_W), F32)],
        compiler_params=_params(("arbitrary",)),
        name="odd_in_proj",
    )(x2d, w, wg, gb, cw)


def _tail_kernel(*refs, n_y, tiles_per_seq):
    x_ref = refs[0]
    y_refs = refs[1 : 1 + n_y]
    (wout_ref, g1_ref, b1_ref, wup_ref, cw_ref, cb_ref, wdn_ref, g_ref, b_ref,
     o_ref, carry_ref, act_ref, x1_ref) = refs[1 + n_y :]

    @pl.when(pl.program_id(0) % tiles_per_seq == 0)
    def _():
        carry_ref[...] = jnp.zeros_like(carry_ref)

    tm = x_ref.shape[0]
    for lo in range(0, tm, tm // 2):
        rows = slice(lo, lo + tm // 2)
        acc = None
        row = 0
        for y_ref in y_refs:
            kw = y_ref.shape[1]
            part = jnp.dot(y_ref[rows, :], wout_ref[row : row + kw, :], preferred_element_type=F32)
            acc = part if acc is None else acc + part
            row += kw
        x1_ref[rows, :] = _layer_norm(DEEPNORM_ALPHA * x_ref[rows, :] + acc, g1_ref[...], b1_ref[...])

    xb = x1_ref[...].astype(BF16)

    def up_dots(j):
        lo = j * FFN_CHUNK
        return (jnp.dot(xb, wup_ref[:, lo : lo + FFN_CHUNK], preferred_element_type=F32),
                jnp.dot(xb, wup_ref[:, D_FF + lo : D_FF + lo + FFN_CHUNK], preferred_element_type=F32))

    def activate(j, gate_up):
        gate, up = gate_up
        sl = slice(j * FFN_CHUNK, (j + 1) * FFN_CHUNK)
        prev = carry_ref[:, sl]
        carry_ref[:, sl] = gate[tm - V7X_SUBLANES :, :]
        conv = _causal_dwconv(gate, prev, cw_ref.at[:, sl], FFN_CONV) + cb_ref[:, sl]
        act_ref[:, sl] = (_silu(conv) * up).astype(BF16)

    _lookahead(D_FF // FFN_CHUNK, up_dots, activate)
    for lo in range(0, tm, tm // 2):
        rows = slice(lo, lo + tm // 2)
        ffn = jnp.dot(act_ref[rows, :], wdn_ref[...], preferred_element_type=F32)
        z = DEEPNORM_ALPHA * x1_ref[rows, :] + ffn
        o_ref[rows, :] = _layer_norm(z, g_ref[...], b_ref[...])


def _layer_tail(x2d, ys, seq_len, layer, w_out, ln1, w_up, conv_w, conv_b, w_down, ln2):
    T, D = x2d.shape
    tm = min(ROW_TILE, seq_len)
    cw = jnp.zeros((V7X_SUBLANES, D_FF), F32).at[:FFN_CONV].set(conv_w)
    row = lambda i: (i, 0)
    vec = lambda v: v.reshape(1, -1)
    return pl.pallas_call(
        functools.partial(_tail_kernel, n_y=len(ys), tiles_per_seq=seq_len // tm),
        grid=(T // tm,),
        in_specs=[pl.BlockSpec((tm, D), row)]
        + [pl.BlockSpec((tm, y.shape[1]), row) for y in ys]
        + [_resident(w_out.shape), _resident((1, D)), _resident((1, D)),
           _resident_layer(w_up.shape, layer), _resident(cw.shape), _resident((1, D_FF)),
           _resident_layer(w_down.shape, layer), _resident((1, D)), _resident((1, D))],
        out_specs=pl.BlockSpec((tm, D), row),
        out_shape=jax.ShapeDtypeStruct((T, D), F32),
        scratch_shapes=[pltpu.VMEM((V7X_SUBLANES, D_FF), F32), pltpu.VMEM((tm, D_FF), BF16), pltpu.VMEM((tm, D), F32)],
        compiler_params=_params(("arbitrary",)),
        name="layer_tail",
    )(x2d, *ys, w_out, vec(ln1[0]), vec(ln1[1]), w_up, cw, vec(conv_b), w_down, vec(ln2[0]), vec(ln2[1]))


def _retention_tables(seq_len):
    H, C, d = RET_HEADS, RET_CHUNK, RET_HEAD_DIM
    scale = d ** -0.5
    log_gamma = jnp.log1p(-jnp.exp2(-5.0 - jnp.arange(H, dtype=F32)))
    idx = jnp.arange(C, dtype=F32)
    rel = idx[:, None] - idx[None, :]
    decay = jnp.where(rel >= 0, jnp.exp(log_gamma[:, None, None] * jnp.maximum(rel, 0.0)), 0.0) * scale
    k_w = jnp.exp(log_gamma[:, None] * (C - 1 - idx)[None, :]) * scale
    q_w = jnp.exp(log_gamma[:, None] * (idx + 1.0)[None, :])
    kw_t = jnp.broadcast_to(k_w[:, :, None], (H, C, d))
    qw_t = jnp.broadcast_to(q_w[:, :, None], (H, C, d))
    cd_t = jnp.broadcast_to(jnp.exp(log_gamma * C)[:, None, None], (H, 1, d))
    inv = 1.0 / (10000.0 ** (jnp.arange(0, d, 2, dtype=F32) / d))
    ang = jnp.arange(seq_len, dtype=F32)[:, None] * inv[None, :]
    cos, sin = jnp.cos(ang), jnp.sin(ang)
    cos_t = jnp.concatenate([cos, cos], -1)
    sin_t = jnp.concatenate([-sin, sin], -1)
    return cos_t, sin_t, decay, kw_t, qw_t, cd_t


def _retention_kernel(q_ref, kt_ref, v_ref, g_ref, dec_ref, kw_ref, qw_ref, cd_ref, o_ref, r_ref):
    @pl.when(pl.program_id(2) == 0)
    def _():
        r_ref[...] = jnp.zeros_like(r_ref)

    C = RET_CHUNK
    decay = dec_ref[0]
    kw = kw_ref[0]
    qw = qw_ref[0]
    cd = cd_ref[0]

    def local(n):
        sl = slice(n * C, (n + 1) * C)
        qb = q_ref[0, sl, :]
        kt = kt_ref[:, sl]
        vb = v_ref[0, sl, :]
        scores = jnp.dot(qb, kt, preferred_element_type=F32) * decay
        y_intra = jnp.dot(scores.astype(BF16), vb, preferred_element_type=F32)
        kv = jnp.dot(kt, (vb.astype(F32) * kw).astype(BF16), preferred_element_type=F32)
        q_dec = (qb.astype(F32) * qw).astype(BF16)
        return y_intra, kv, q_dec

    def recurrent(n, staged):
        y_intra, kv, q_dec = staged
        sl = slice(n * C, (n + 1) * C)
        r = r_ref[...]
        y = y_intra + jnp.dot(q_dec, r.astype(BF16), preferred_element_type=F32)
        r_ref[...] = cd * r + kv
        mu = jnp.mean(y, -1, keepdims=True)
        yc = y - mu
        var = jnp.mean(yc * yc, -1, keepdims=True)
        o_ref[0, sl, :] = (yc * lax.rsqrt(var + LN_EPS) * g_ref[0, sl, :]).astype(o_ref.dtype)

    _lookahead(q_ref.shape[1] // C, local, recurrent)


def _retention(q3, kt, v3, g3, tables):
    B, S, _ = v3.shape
    H, d, C = RET_HEADS, RET_HEAD_DIM, RET_CHUNK
    ts = min(SEQ_TILE, S)
    tiles = S // ts
    _, _, decay, kw_t, qw_t, cd_t = tables
    col = pl.BlockSpec((1, ts, d), lambda b, h, c: (b, c, h))
    per_head = lambda shape: pl.BlockSpec((1,) + shape, lambda b, h, c: (h, 0, 0))
    return pl.pallas_call(
        _retention_kernel,
        grid=(B, H, tiles),
        in_specs=[col, pl.BlockSpec((d, ts), lambda b, h, c: (h, b * tiles + c)), col, col,
                  per_head((C, C)), per_head((C, d)), per_head((C, d)), per_head((1, d))],
        out_specs=pl.BlockSpec((1, ts, d), lambda b, h, c: (b, c, h)),
        out_shape=jax.ShapeDtypeStruct((B, S, H * d), BF16),
        scratch_shapes=[pltpu.VMEM((d, d), F32)],
        compiler_params=_params(("parallel", "parallel", "arbitrary")),
        name="retention",
    )(q3, kt, v3, g3, decay, kw_t, qw_t, cd_t)


def _t5_bucket(dist):
    exact = T5_BUCKETS // 2
    n = jnp.maximum(dist, 0)
    large = exact + (jnp.log(jnp.maximum(n, 1).astype(F32) / exact) / math.log(T5_MAX_DIST / exact) * (T5_BUCKETS - exact)).astype(jnp.int32)
    large = jnp.minimum(large, T5_BUCKETS - 1)
    return jnp.where(n < exact, n, large)


def _dsa_bucket_tables():
    blk = DSA_BLOCK
    qi = jnp.arange(blk)[:, None]
    kj = jnp.arange(2 * blk)[None, :]
    tabs = []
    for window, dilation in DSA_PATTERNS:
        assert window // dilation <= blk
        dist = qi + blk - kj
        valid = (dist >= 0) & (dist <= window // dilation)
        tabs.append(jnp.where(valid, _t5_bucket(dist * dilation), -1))
    return jnp.stack(tabs).astype(jnp.int32)


def _dsa_bias_kernel(rb_ref, bk_ref, o_ref):
    blk = DSA_BLOCK
    rows = 4 * V7X_SUBLANES
    masked = jnp.full((rows, blk), -jnp.inf, F32)
    for lo in range(0, blk, rows):
        bk = bk_ref[0, lo : lo + rows, :]
        accs = [jnp.full(bk.shape, -jnp.inf, F32) for _ in range(DSA_HEADS)]
        for b in range(T5_BUCKETS):
            hit = bk == b
            accs = [jnp.where(hit, rb_ref[b, h] * LOG2_E, acc) for h, acc in enumerate(accs)]
        for h, acc in enumerate(accs):
            o_ref[0, 0, h, lo : lo + rows, :] = acc
            o_ref[0, 1, h, lo : lo + rows, :] = jnp.concatenate([acc[:, blk:], masked], axis=1)


def _dsa_bias(rel_bias):
    buckets = _dsa_bucket_tables()
    P, blk = len(DSA_PATTERNS), DSA_BLOCK
    return pl.pallas_call(
        _dsa_bias_kernel,
        grid=(P,),
        in_specs=[pl.BlockSpec(memory_space=pltpu.SMEM), pl.BlockSpec((1, blk, 2 * blk), lambda p: (p, 0, 0))],
        out_specs=pl.BlockSpec((1, 2, DSA_HEADS, blk, 2 * blk), lambda p: (p, 0, 0, 0, 0)),
        out_shape=jax.ShapeDtypeStruct((P, 2, DSA_HEADS, blk, 2 * blk), F32),
        compiler_params=_params(("parallel",)),
        name="dsa_bias",
    )(rel_bias, buckets)


def _dsa_kernel(*refs):
    P = len(DSA_PATTERNS)
    q_refs, k_refs, v_refs = refs[:P], refs[P : 2 * P], refs[2 * P : 3 * P]
    bias_ref, o_ref, acc_o, acc_m, acc_l = refs[3 * P :]
    blk = DSA_BLOCK
    tq = o_ref.shape[1]
    t = pl.program_id(2)
    lane = lax.broadcasted_iota(jnp.int32, (1, V7X_LANES), 1)
    head0 = lane < DSA_HEAD_DIM

    def block_logits(p, r, idx):
        blocks_per_residue = tq // (blk * r)
        rho = idx // blocks_per_residue
        nb = idx % blocks_per_residue
        l0 = pl.multiple_of(t * (tq // r) + nb * blk, blk)
        first = l0 == 0
        k_row = pl.multiple_of(jnp.where(first, l0, l0 - blk), blk)
        qb = q_refs[p][0, rho, pl.ds(l0, blk), :]
        zero = jnp.zeros_like(qb)
        q2 = jnp.concatenate([jnp.where(head0, qb, zero), jnp.where(head0, zero, qb)], axis=0)
        bias = bias_ref[p, pl.ds(first.astype(jnp.int32), 1), 0][0]
        kb = k_refs[p][0, rho, pl.ds(k_row, 2 * blk), :]
        logits = lax.dot_general(q2, kb, _NT, preferred_element_type=F32) + bias
        start = rho + nb * (blk * r)
        rows = pl.ds(start, blk, stride=r) if r > 1 else pl.ds(start, blk)
        return logits, (rho, k_row), rows

    def block_output(p, logits, key_rows, rows):
        rho, k_row = key_rows
        m = jnp.max(logits, -1, keepdims=True)
        e = jnp.exp2(logits - m).astype(BF16)
        vb = v_refs[p][0, rho, pl.ds(k_row, 2 * blk), :]
        o2 = jnp.dot(e, jnp.concatenate([vb, jnp.ones_like(vb)], axis=1), preferred_element_type=F32)
        o_p = jnp.where(head0, o2[:blk, :V7X_LANES], o2[blk:, :V7X_LANES])
        m_p = jnp.where(head0, m[:blk], m[blk:])
        l_p = jnp.where(head0, o2[:blk, V7X_LANES:], o2[blk:, V7X_LANES:])
        if p > 0:
            acc_o.at[p - 1][rows, :] = o_p
            acc_m.at[p - 1][rows, :] = m_p
            acc_l.at[p - 1][rows, :] = l_p
            return
        os_ = [o_p] + [acc_o[b, rows, :] for b in range(P - 1)]
        ms = [m_p] + [acc_m[b, rows, :] for b in range(P - 1)]
        ls = [l_p] + [acc_l[b, rows, :] for b in range(P - 1)]
        m_all = functools.reduce(jnp.maximum, ms)
        ws = [jnp.exp2(m_b - m_all) for m_b in ms]
        num = functools.reduce(jnp.add, [w * o_b for w, o_b in zip(ws, os_)])
        den = functools.reduce(jnp.add, [w * l_b for w, l_b in zip(ws, ls)])
        o_ref[0, rows, :] = (num / den).astype(o_ref.dtype)

    blocks = [(p, DSA_PATTERNS[p][1], idx) for p in reversed(range(P)) for idx in range(tq // blk)]
    staged = [block_logits(*blocks[i]) for i in range(DSA_GROUP)]
    for i, (p, _, _) in enumerate(blocks):
        if i + DSA_GROUP < len(blocks):
            staged.append(block_logits(*blocks[i + DSA_GROUP]))
        block_output(p, *staged[i])


def _dilated_attention(ds, bias):
    B, _, S, _ = ds[0].shape
    pairs = DSA_W // V7X_LANES
    tq = DSA_TILE
    assert S % tq == 0 and S >= 2 * tq
    P, blk = len(DSA_PATTERNS), DSA_BLOCK
    bias5 = bias.reshape(P, 2, pairs, 2 * blk, 2 * blk)

    def seq(r, off):
        return pl.BlockSpec((1, r, S // r, V7X_LANES), lambda b, p, t: (b, 0, 0, off + p))

    return pl.pallas_call(
        _dsa_kernel,
        grid=(B, pairs, S // tq),
        in_specs=[seq(r, off) for off in (0, pairs, 2 * pairs) for _, r in DSA_PATTERNS]
        + [pl.BlockSpec((P, 2, 1, 2 * blk, 2 * blk), lambda b, p, t: (0, 0, p, 0, 0))],
        out_specs=pl.BlockSpec((1, tq, V7X_LANES), lambda b, p, t: (b, t, p)),
        out_shape=jax.ShapeDtypeStruct((B, S, DSA_W), BF16),
        scratch_shapes=[pltpu.VMEM((P - 1, tq, V7X_LANES), F32)] * 3,
        compiler_params=_params(("parallel", "parallel", "arbitrary")),
        name="dilated_attention",
    )(*ds, *ds, *ds, bias5)


def _mlstm_kernel(q_ref, kt_ref, v_ref, og_ref, gr_ref, o_ref, c_ref, m_ref):
    h = pl.program_id(1)

    @pl.when(pl.program_id(2) == 0)
    def _():
        c_ref[...] = jnp.zeros_like(c_ref)
        m_ref[...] = jnp.zeros_like(m_ref)

    C = MLSTM_CHUNK
    dv = MLSTM_HEAD_DIM
    L = V7X_LANES
    n_chunks = q_ref.shape[1] // C
    ri = lax.broadcasted_iota(jnp.int32, (C, C), 0)
    ci = lax.broadcasted_iota(jnp.int32, (C, C), 1)
    causal = ci <= ri
    eye = ci == ri
    rep = lambda col: jnp.broadcast_to(col, (C, L))

    def gate_vectors(n):
        sl = slice(n * C, (n + 1) * C)
        gates = gr_ref[:, sl]
        gate_id = lax.broadcasted_iota(jnp.int32, (2 * MLSTM_HEADS, 1), 0)
        li_row = jnp.sum(jnp.where(gate_id == h, gates, 0.0), 0, keepdims=True)
        lf_row = jnp.sum(jnp.where(gate_id == MLSTM_HEADS + h, gates, 0.0), 0, keepdims=True)
        li_col = jnp.sum(jnp.where(eye, li_row, 0.0), -1, keepdims=True)
        a_col = jnp.sum(jnp.where(causal, lf_row, 0.0), -1, keepdims=True)
        g_tot = jnp.sum(lf_row, -1, keepdims=True)
        return li_row, li_col, a_col, g_tot

    def decay_matrix(gv):
        li_row, li_col, a_col, g_tot = gv
        a_rep = rep(a_col)
        a_row = jnp.sum(jnp.where(eye, a_rep, 0.0), 0, keepdims=True)
        w_state = g_tot - a_col + li_col
        m_loc = jnp.max(w_state, 0, keepdims=True)
        ew_rep = rep(jnp.exp(w_state - m_loc))
        dmat = jnp.where(causal, a_rep - a_row + li_row, -jnp.inf)
        d_max = jnp.max(dmat, -1, keepdims=True)
        return a_rep, g_tot, m_loc, ew_rep, dmat, d_max

    def local_matmuls(n, dm, scalars):
        a_rep, _, _, ew_rep, dmat, d_max = dm
        m_s, a_old, a_new = scalars
        ew_rep = ew_rep * a_new
        sl = slice(n * C, (n + 1) * C)
        qb = q_ref[0, sl, :]
        kt = kt_ref[:, sl]
        vb = v_ref[0, sl, :]
        d_rep = rep(d_max)
        p = (jnp.exp(dmat - d_rep) * jnp.dot(qb, kt, preferred_element_type=F32)).astype(BF16)
        pv = jnp.dot(p, jnp.concatenate([vb, jnp.ones((C, L), BF16)], axis=1), preferred_element_type=F32)
        vw = jnp.concatenate([vb.astype(F32) * jnp.concatenate([ew_rep] * (dv // L), axis=1), ew_rep], axis=1)
        kv_ext = jnp.dot(kt, vw.astype(BF16), preferred_element_type=F32)
        return qb, pv, kv_ext, a_rep, d_rep, m_s, a_old

    def recurrent(n, staged):
        qb, pv, kv_ext, a_rep, d_rep, m_s, a_old = staged
        sl = slice(n * C, (n + 1) * C)
        c_s = c_ref[...]
        inter =jnp.dot(qb, c_s.astype(BF16), preferred_element_type=F32)
        inter_log = a_rep + m_s
        m_row = jnp.maximum(inter_log, d_rep)
        f_intra = jnp.exp(d_rep - m_row)
        f_inter = jnp.exp(inter_log - m_row)
        den = f_intra * pv[:, dv:] + f_inter * inter[:, dv:]
        r = 1.0 / jnp.maximum(jnp.abs(den), jnp.exp(-m_row))
        for lo in range(0, dv, L):
            num = f_intra * pv[:, lo : lo + L] + f_inter * inter[:, lo : lo + L]
            o_ref[0, sl, lo : lo + L] = (num * r * og_ref[0, sl, lo : lo + L]).astype(o_ref.dtype)
        c_ref[...] = a_old * c_s + kv_ext

    gvs = [gate_vectors(n) for n in range(n_chunks)]
    dms = [decay_matrix(gv) for gv in gvs]
    m_s = m_ref[...][:, 0:1]
    scalars = []
    for _, g_tot, m_loc, _, _, _ in dms:
        m_new = jnp.maximum(g_tot + m_s, m_loc)
        scalars.append((m_s, jnp.exp(g_tot + m_s - m_new), jnp.exp(m_loc - m_new)))
        m_s = m_new
    m_ref[...] = jnp.broadcast_to(m_s, m_ref.shape)
    _lookahead(n_chunks, lambda n: local_matmuls(n, dms[n], scalars[n]), recurrent)


def _mlstm(q3, kt, v3, o3, gates_row):
    B, S, _ = v3.shape
    H, d = MLSTM_HEADS, MLSTM_HEAD_DIM
    ts = min(SEQ_TILE, S)
    tiles = S // ts
    col = pl.BlockSpec((1, ts, d), lambda b, h, c: (b, c, h))
    return pl.pallas_call(
        _mlstm_kernel,
        grid=(B, H, tiles),
        in_specs=[col, pl.BlockSpec((d, ts), lambda b, h, c: (h, b * tiles + c)), col, col,
                  pl.BlockSpec((2 * H, ts), lambda b, h, c: (0, b * tiles + c))],
        out_specs=pl.BlockSpec((1, ts, d), lambda b, h, c: (b, c, h)),
        out_shape=jax.ShapeDtypeStruct((B, S, H * d), BF16),
        scratch_shapes=[pltpu.VMEM((d, d + V7X_LANES), F32), pltpu.VMEM((1, V7X_LANES), F32)],
        compiler_params=_params(("parallel", "parallel", "arbitrary")),
        name="mlstm",
    )(q3, kt, v3, o3, gates_row)


def _even_mixer(x2d, B, S, w_in, rel_bias):
    tables = _retention_tables(S)
    q, kt, v, gate, d_nat, *d_dil = _even_in_proj(x2d, S, w_in.astype(BF16), tables[0], tables[1])
    y_r = _retention(q.reshape(B, S, -1), kt, v.reshape(B, S, -1), gate.reshape(B, S, -1), tables)
    y_d = _dilated_attention([d_nat.reshape(B, 1, S, -1)] + d_dil, _dsa_bias(rel_bias))
    return [y_r.reshape(B * S, RET_W), y_d.reshape(B * S, DSA_W)]


def _odd_mixer(x2d, B, S, w_in, gate_b, conv_w):
    H = MLSTM_HEADS
    wide = 4 * MLSTM_W
    wg = jnp.zeros((D_MODEL, V7X_LANES), F32).at[:, : 2 * H].set(w_in[:, wide:]).astype(BF16)
    gb = jnp.zeros((1, V7X_LANES), F32).at[0, : 2 * H].set(gate_b)
    q, kt, v, og, gates = _odd_in_proj(x2d, S, w_in.astype(BF16), wg, gb, conv_w)
    y = _mlstm(q.reshape(B, S, -1), kt, v.reshape(B, S, -1), og.reshape(B, S, -1), gates)
    return [y.reshape(B * S, MLSTM_W)]


def kernel(x, even_w_in, even_w_out, rel_bias, odd_w_in, odd_gate_b, odd_conv_w, odd_w_out, ffn_w_up, ffn_conv_w, ffn_conv_b, ffn_w_down, ln_g, ln_b):
    B, S, D = x.shape
    x2d = x.reshape(B * S, D)
    w_up, w_down = ffn_w_up.astype(BF16), ffn_w_down.astype(BF16)
    for layer in range(DEPTH):
        j = layer // 2
        if layer % 2 == 0:
            ys, w_out = _even_mixer(x2d, B, S, even_w_in[j], rel_bias), even_w_out[j]
        else:
            ys, w_out = _odd_mixer(x2d, B, S, odd_w_in[j], odd_gate_b[j], odd_conv_w[j]), odd_w_out[j]
        x2d = _layer_tail(x2d, ys, S, layer, w_out.astype(BF16), (ln_g[layer, 0], ln_b[layer, 0]), w_up,
                          ffn_conv_w[layer], ffn_conv_b[layer], w_down, (ln_g[layer, 1], ln_b[layer, 1]))
    return x2d.reshape(B, S, D)
```

```python
import functools
import math

import jax
import jax.numpy as jnp
import numpy as np
from jax import lax
from jax.experimental import pallas as pl
from jax.experimental.pallas import tpu as pltpu

F32 = jnp.float32
BF16 = jnp.bfloat16

D_MODEL = 1024
DEPTH = 2
RET_HEADS = 4
RET_HEAD_DIM = 128
RET_CHUNK = 128
DSA_HEADS = 8
DSA_HEAD_DIM = 64
DSA_PATTERNS = ((128, 1), (512, 4), (2048, 16))
DSA_BLOCK = 128
T5_BUCKETS = 32
T5_MAX_DIST = 2048
MLSTM_HEADS = 4
MLSTM_HEAD_DIM = 256
MLSTM_CHUNK = 128
MLSTM_CONV = 4
D_FF = 2816
FFN_CONV = 3
LN_EPS = 1e-5
DEEPNORM_ALPHA = (2.0 * DEPTH) ** 0.25
LOG2_E = math.log2(math.e)

RET_W = RET_HEADS * RET_HEAD_DIM
DSA_W = DSA_HEADS * DSA_HEAD_DIM
MLSTM_W = MLSTM_HEADS * MLSTM_HEAD_DIM

V7X_LANES = 128
V7X_SUBLANES = 8
V7X_VMEM_BYTES = 64 * 1024 * 1024
VMEM_LIMIT_BYTES = V7X_VMEM_BYTES - 8 * 1024 * 1024

ROW_TILE = 512
PROJ_CHUNK = 512
FFN_CHUNK = 256
SEQ_TILE = 4096
DSA_TILE = DSA_BLOCK * DSA_PATTERNS[-1][1]
DSA_GROUP = 2

_NT = (((1,), (1,)), ((), ()))
_TN = (((0,), (0,)), ((), ()))


def _params(semantics):
    return pltpu.CompilerParams(dimension_semantics=semantics, vmem_limit_bytes=VMEM_LIMIT_BYTES)


def _resident(shape):
    nd = len(shape)
    return pl.BlockSpec(shape, lambda *_: (0,) * nd, pipeline_mode=pl.Buffered(1))


def _resident_layer(stacked_shape, layer):
    return pl.BlockSpec((None,) + tuple(stacked_shape[1:]), lambda *_: (layer, 0, 0), pipeline_mode=pl.Buffered(1))


def _layer_norm(z, g, b):
    mu = jnp.mean(z, -1, keepdims=True)
    zc = z - mu
    var = jnp.mean(zc * zc, -1, keepdims=True)
    return zc * lax.rsqrt(var + LN_EPS) * g + b


def _silu(z):
    return z * jax.nn.sigmoid(z)


def _log_sigmoid(x):
    return jnp.minimum(x, 0.0) - jnp.log1p(jnp.exp(-jnp.abs(x)))


def _lookahead(n, produce, consume, depth=1):
    staged = [produce(j) for j in range(min(depth, n))]
    for j in range(n):
        if j + depth < n:
            staged.append(produce(j + depth))
        consume(j, staged[j])
        staged[j] = None


def _shift_rows(x, prev, s):
    if s == 0:
        return x
    rolled = pltpu.roll(x, s, 0)
    head = pltpu.roll(prev, s, 0)
    rid = lax.broadcasted_iota(jnp.int32, (V7X_SUBLANES, x.shape[1]), 0)
    first = jnp.where(rid < s, head, rolled[:V7X_SUBLANES])
    return jnp.concatenate([first, rolled[V7X_SUBLANES:]], axis=0)


def _causal_dwconv(x, prev, w_ref, taps):
    acc = None
    for k in range(taps):
        term = _shift_rows(x, prev, taps - 1 - k) * w_ref[k : k + 1, :]
        acc = term if acc is None else acc + term
    return acc


def _even_in_proj_kernel(x_ref, w_ref, cos_ref, sin_ref, q_ref, kt_ref, v_ref, g_ref, *rest):
    d_refs, stage_ref = rest[:-1], rest[-1]
    xb = x_ref[...].astype(BF16)
    tm = xb.shape[0]
    cos = cos_ref[...]
    sin = sin_ref[...]
    tn, d = RET_W, RET_HEAD_DIM

    def consume(j, h):
        if j < 2:
            for hh in range(RET_HEADS):
                c = h[:, hh * d : (hh + 1) * d]
                rot = c * cos + pltpu.roll(c, d // 2, 1) * sin
                if j == 0:
                    q_ref[:, hh * d : (hh + 1) * d] = rot.astype(BF16)
                else:
                    kt_ref[hh * d : (hh + 1) * d, :] = rot.T.astype(BF16)
        elif j == 2:
            v_ref[...] = h.astype(BF16)
        elif j == 3:
            g_ref[...] = _silu(h).astype(BF16)
        else:
            if j == 4:
                h = h * (DSA_HEAD_DIM ** -0.5 * LOG2_E)
            lo = (j - 4) * tn
            d_refs[0][:, lo : lo + tn] = h.astype(BF16)
            src, dst = stage_ref.at[0], stage_ref.at[1]
            for s in range(tn // V7X_LANES):
                src[s] = h[:, s * V7X_LANES : (s + 1) * V7X_LANES]
            prev_r = 1
            for d_ref, (_, r) in zip(d_refs[1:], DSA_PATTERNS[1:]):
                step = r // prev_r
                n_rows = tm // r
                for rho_prev in range(prev_r):
                    for a in range(step):
                        rho = rho_prev + prev_r * a
                        for s in range(tn // V7X_LANES):
                            rows = src.at[s][pl.ds(rho_prev * (tm // prev_r) + a, n_rows, stride=step), :]
                            if r != DSA_PATTERNS[-1][1]:
                                dst[s, rho * n_rows : (rho + 1) * n_rows, :] = rows
                            d_ref[0, rho, :, lo + s * V7X_LANES : lo + (s + 1) * V7X_LANES] = rows.astype(BF16)
                src, dst, prev_r = dst, src, r

    _lookahead(w_ref.shape[1] // tn, lambda j: jnp.dot(xb, w_ref[:, j * tn : (j + 1) * tn], preferred_element_type=F32), consume)


def _even_in_proj(x2d, seq_len, w, cos_t, sin_t):
    T, K = x2d.shape
    tm = min(ROW_TILE, seq_len)
    tps = seq_len // tm
    B = T // seq_len
    row = lambda i: (i, 0)
    pos = pl.BlockSpec((tm, RET_HEAD_DIM), lambda i: (i % tps, 0))
    shapes = [(T, RET_W), (RET_W, T), (T, RET_W), (T, RET_W), (T, 3 * DSA_W)]
    dtypes = [BF16, BF16, BF16, BF16, BF16]
    specs = [pl.BlockSpec((tm, RET_W), row), pl.BlockSpec((RET_W, tm), lambda i: (0, i)),
             pl.BlockSpec((tm, RET_W), row), pl.BlockSpec((tm, RET_W), row), pl.BlockSpec((tm, 3 * DSA_W), row)]
    for _, r in DSA_PATTERNS[1:]:
        shapes.append((B, r, seq_len // r, 3 * DSA_W))
        dtypes.append(BF16)
        specs.append(pl.BlockSpec((1, r, tm // r, 3 * DSA_W), lambda i: (i // tps, 0, i % tps, 0)))
    return pl.pallas_call(
        _even_in_proj_kernel,
        grid=(T // tm,),
        in_specs=[pl.BlockSpec((tm, K), row), _resident(w.shape), pos, pos],
        out_specs=specs,
        out_shape=[jax.ShapeDtypeStruct(s, dt) for s, dt in zip(shapes, dtypes)],
        scratch_shapes=[pltpu.VMEM((2, RET_W // V7X_LANES, tm, V7X_LANES), F32)],
        compiler_params=_params(("parallel",)),
        name="even_in_proj",
    )(x2d, w, cos_t, sin_t)


def _odd_in_proj_kernel(x_ref, w_ref, wg_ref, gb_ref, cw_ref, q_ref, kt_ref, v_ref, o_ref, gt_ref, carry_ref, *, tiles_per_seq):
    tn = PROJ_CHUNK
    per = MLSTM_W // tn

    @pl.when(pl.program_id(0) % tiles_per_seq == 0)
    def _():
        carry_ref[...] = jnp.zeros_like(carry_ref)

    xb = x_ref[...].astype(BF16)
    tm = xb.shape[0]

    order = [c for pair in zip(range(2 * per), range(2 * per, 4 * per)) for c in pair]

    def produce(i):
        j = order[i]
        return jnp.dot(xb, w_ref[:, j * tn : (j + 1) * tn], preferred_element_type=F32)

    def consume(i, h):
        j = order[i]
        sl = slice(j * tn, (j + 1) * tn)
        if j < 2 * per:
            prev = carry_ref[:, sl]
            carry_ref[:, sl] = h[tm - V7X_SUBLANES :, :]
            act = _silu(_causal_dwconv(h, prev, cw_ref.at[:, sl], MLSTM_CONV))
            if j < per:
                q_ref[:, sl] = act.astype(BF16)
            else:
                kt_ref[(j - per) * tn : (j - per + 1) * tn, :] = (act * (MLSTM_HEAD_DIM ** -0.5)).T.astype(BF16)
        elif j < 3 * per:
            v_ref[:, (j - 2 * per) * tn : (j - 2 * per + 1) * tn] = h.astype(BF16)
        else:
            o_ref[:, (j - 3 * per) * tn : (j - 3 * per + 1) * tn] = jax.nn.sigmoid(h).astype(BF16)

    _lookahead(4 * per, produce, consume)
    gates = (jnp.dot(xb, wg_ref[...], preferred_element_type=F32) + gb_ref[...]).T[: 2 * MLSTM_HEADS, :]
    gate_id = lax.broadcasted_iota(jnp.int32, (2 * MLSTM_HEADS, 1), 0)
    gt_ref[...] = jnp.where(gate_id < MLSTM_HEADS, gates, _log_sigmoid(gates))


def _odd_in_proj(x2d, seq_len, w, wg, gb, conv_w):
    T, K = x2d.shape
    tm = min(ROW_TILE, seq_len)
    W = MLSTM_W
    cw = jnp.zeros((V7X_SUBLANES, 2 * W), F32).at[:MLSTM_CONV].set(conv_w)
    row = lambda i: (i, 0)
    shapes = ((T, W), (W, T), (T, W), (T, W), (2 * MLSTM_HEADS, T))
    dtypes = (BF16, BF16, BF16, BF16, F32)
    return pl.pallas_call(
        functools.partial(_odd_in_proj_kernel, tiles_per_seq=seq_len // tm),
        grid=(T // tm,),
        in_specs=[pl.BlockSpec((tm, K), row), _resident(w.shape), _resident(wg.shape), _resident(gb.shape),
                  _resident(cw.shape)],
        out_specs=[pl.BlockSpec((tm, W), row), pl.BlockSpec((W, tm), lambda i: (0, i)), pl.BlockSpec((tm, W), row),
                   pl.BlockSpec((tm, W), row), pl.BlockSpec((2 * MLSTM_HEADS, tm), lambda i: (0, i))],
        out_shape=[jax.ShapeDtypeStruct(s, dt) for s, dt in zip(shapes, dtypes)],
        scratch_shapes=[pltpu.VMEM((V7X_SUBLANES, 2 * MLSTM_W), F32)],
        compiler_params=_params(("arbitrary",)),
        name="odd_in_proj",
    )(x2d, w, wg, gb, cw)


def _tail_kernel(*refs, n_y, tiles_per_seq):
    x_ref = refs[0]
    y_refs = refs[1 : 1 + n_y]
    (wout_ref, g1_ref, b1_ref, wup_ref, cw_ref, cb_ref, wdn_ref, g_ref, b_ref,
     o_ref, carry_ref, act_ref, x1_ref) = refs[1 + n_y :]

    @pl.when(pl.program_id(0) % tiles_per_seq == 0)
    def _():
        carry_ref[...] = jnp.zeros_like(carry_ref)

    tm = x_ref.shape[0]
    for lo in range(0, tm, tm // 2):
        rows = slice(lo, lo + tm // 2)
        acc = None
        row = 0
        for y_ref in y_refs:
            kw = y_ref.shape[1]
            part = jnp.dot(y_ref[rows, :], wout_ref[row : row + kw, :], preferred_element_type=F32)
            acc = part if acc is None else acc + part
            row += kw
        x1_ref[rows, :] = _layer_norm(DEEPNORM_ALPHA * x_ref[rows, :] + acc, g1_ref[...], b1_ref[...])

    xb = x1_ref[...].astype(BF16)

    def up_dots(j):
        lo = j * FFN_CHUNK
        return (jnp.dot(xb, wup_ref[:, lo : lo + FFN_CHUNK], preferred_element_type=F32),
                jnp.dot(xb, wup_ref[:, D_FF + lo : D_FF + lo + FFN_CHUNK], preferred_element_type=F32))

    def activate(j, gate_up):
        gate, up = gate_up
        sl = slice(j * FFN_CHUNK, (j + 1) * FFN_CHUNK)
        prev = carry_ref[:, sl]
        carry_ref[:, sl] = gate[tm - V7X_SUBLANES :, :]
        conv = _causal_dwconv(gate, prev, cw_ref.at[:, sl], FFN_CONV) + cb_ref[:, sl]
        act_ref[:, sl] = (_silu(conv) * up).astype(BF16)

    _lookahead(D_FF // FFN_CHUNK, up_dots, activate)
    for lo in range(0, tm, tm // 2):
        rows = slice(lo, lo + tm // 2)
        ffn = jnp.dot(act_ref[rows, :], wdn_ref[...], preferred_element_type=F32)
        z = DEEPNORM_ALPHA * x1_ref[rows, :] + ffn
        o_ref[rows, :] = _layer_norm(z, g_ref[...], b_ref[...])


def _layer_tail(x2d, ys, seq_len, layer, w_out, ln1, w_up, conv_w, conv_b, w_down, ln2):
    T, D = x2d.shape
    tm = min(ROW_TILE, seq_len)
    cw = jnp.zeros((V7X_SUBLANES, D_FF), F32).at[:FFN_CONV].set(conv_w)
    row = lambda i: (i, 0)
    vec = lambda v: v.reshape(1, -1)
    return pl.pallas_call(
        functools.partial(_tail_kernel, n_y=len(ys), tiles_per_seq=seq_len // tm),
        grid=(T // tm,),
        in_specs=[pl.BlockSpec((tm, D), row)]
        + [pl.BlockSpec((tm, y.shape[1]), row) for y in ys]
        + [_resident(w_out.shape), _resident((1, D)), _resident((1, D)),
           _resident_layer(w_up.shape, layer), _resident(cw.shape), _resident((1, D_FF)),
           _resident_layer(w_down.shape, layer), _resident((1, D)), _resident((1, D))],
        out_specs=pl.BlockSpec((tm, D), row),
        out_shape=jax.ShapeDtypeStruct((T, D), F32),
        scratch_shapes=[pltpu.VMEM((V7X_SUBLANES, D_FF), F32), pltpu.VMEM((tm, D_FF), BF16), pltpu.VMEM((tm, D), F32)],
        compiler_params=_params(("arbitrary",)),
        name="layer_tail",
    )(x2d, *ys, w_out, vec(ln1[0]), vec(ln1[1]), w_up, cw, vec(conv_b), w_down, vec(ln2[0]), vec(ln2[1]))


def _retention_tables(seq_len):
    H, C, d = RET_HEADS, RET_CHUNK, RET_HEAD_DIM
    scale = d ** -0.5
    log_gamma = jnp.log1p(-jnp.exp2(-5.0 - jnp.arange(H, dtype=F32)))
    idx = jnp.arange(C, dtype=F32)
    rel = idx[:, None] - idx[None, :]
    decay = jnp.where(rel >= 0, jnp.exp(log_gamma[:, None, None] * jnp.maximum(rel, 0.0)), 0.0) * scale
    k_w = jnp.exp(log_gamma[:, None] * (C - 1 - idx)[None, :]) * scale
    q_w = jnp.exp(log_gamma[:, None] * (idx + 1.0)[None, :])
    kw_t = jnp.broadcast_to(k_w[:, :, None], (H, C, d))
    qw_t = jnp.broadcast_to(q_w[:, :, None], (H, C, d))
    cd_t = jnp.broadcast_to(jnp.exp(log_gamma * C)[:, None, None], (H, 1, d))
    inv = np.float32(1.0) / np.float32(10000.0) ** (np.arange(0, d, 2, dtype=np.float32) / np.float32(d))
    ang = (np.arange(seq_len, dtype=np.float32)[:, None] * inv[None, :]).astype(np.float64)
    cos, sin = np.cos(ang).astype(np.float32), np.sin(ang).astype(np.float32)
    cos_t = jnp.asarray(np.concatenate([cos, cos], -1), F32)
    sin_t = jnp.asarray(np.concatenate([-sin, sin], -1), F32)
    return cos_t, sin_t, decay, kw_t, qw_t, cd_t


def _retention_kernel(q_ref, kt_ref, v_ref, g_ref, dec_ref, kw_ref, qw_ref, cd_ref, o_ref, r_ref):
    @pl.when(pl.program_id(2) == 0)
    def _():
        r_ref[...] = jnp.zeros_like(r_ref)

    C = RET_CHUNK
    decay = dec_ref[0]
    kw = kw_ref[0]
    qw = qw_ref[0]
    cd = cd_ref[0]

    def local(n):
        sl = slice(n * C, (n + 1) * C)
        qb = q_ref[0, sl, :]
        kt = kt_ref[:, sl]
        vb = v_ref[0, sl, :]
        scores = jnp.dot(qb, kt, preferred_element_type=F32) * decay
        y_intra = jnp.dot(scores.astype(BF16), vb, preferred_element_type=F32)
        kv = jnp.dot(kt, (vb.astype(F32) * kw).astype(BF16), preferred_element_type=F32)
        q_dec = (qb.astype(F32) * qw).astype(BF16)
        return y_intra, kv, q_dec

    def recurrent(n, staged):
        y_intra, kv, q_dec = staged
        sl = slice(n * C, (n + 1) * C)
        r = r_ref[...]
        y = y_intra + jnp.dot(q_dec, r.astype(BF16), preferred_element_type=F32)
        r_ref[...] = cd * r + kv
        mu = jnp.mean(y, -1, keepdims=True)
        yc = y - mu
        var = jnp.mean(yc * yc, -1, keepdims=True)
        o_ref[0, sl, :] = (yc * lax.rsqrt(var + LN_EPS) * g_ref[0, sl, :]).astype(o_ref.dtype)

    _lookahead(q_ref.shape[1] // C, local, recurrent)


def _retention(q3, kt, v3, g3, tables):
    B, S, _ = v3.shape
    H, d, C = RET_HEADS, RET_HEAD_DIM, RET_CHUNK
    ts = min(SEQ_TILE, S)
    tiles = S // ts
    _, _, decay, kw_t, qw_t, cd_t = tables
    col = pl.BlockSpec((1, ts, d), lambda b, h, c: (b, c, h))
    per_head = lambda shape: pl.BlockSpec((1,) + shape, lambda b, h, c: (h, 0, 0))
    return pl.pallas_call(
        _retention_kernel,
        grid=(B, H, tiles),
        in_specs=[col, pl.BlockSpec((d, ts), lambda b, h, c: (h, b * tiles + c)), col, col,
                  per_head((C, C)), per_head((C, d)), per_head((C, d)), per_head((1, d))],
        out_specs=pl.BlockSpec((1, ts, d), lambda b, h, c: (b, c, h)),
        out_shape=jax.ShapeDtypeStruct((B, S, H * d), BF16),
        scratch_shapes=[pltpu.VMEM((d, d), F32)],
        compiler_params=_params(("parallel", "parallel", "arbitrary")),
        name="retention",
    )(q3, kt, v3, g3, decay, kw_t, qw_t, cd_t)


def _t5_bucket(dist):
    exact = T5_BUCKETS // 2
    n = jnp.maximum(dist, 0)
    large = exact + (jnp.log(jnp.maximum(n, 1).astype(F32) / exact) / math.log(T5_MAX_DIST / exact) * (T5_BUCKETS - exact)).astype(jnp.int32)
    large = jnp.minimum(large, T5_BUCKETS - 1)
    return jnp.where(n < exact, n, large)


def _dsa_bucket_tables():
    blk = DSA_BLOCK
    qi = jnp.arange(blk)[:, None]
    kj = jnp.arange(2 * blk)[None, :]
    tabs = []
    for window, dilation in DSA_PATTERNS:
        assert window // dilation <= blk
        dist = qi + blk - kj
        valid = (dist >= 0) & (dist <= window // dilation)
        tabs.append(jnp.where(valid, _t5_bucket(dist * dilation), -1))
    return jnp.stack(tabs).astype(jnp.int32)


def _dsa_bias_kernel(rb_ref, bk_ref, o_ref):
    blk = DSA_BLOCK
    rows = 4 * V7X_SUBLANES
    masked = jnp.full((rows, blk), -jnp.inf, F32)
    for lo in range(0, blk, rows):
        bk = bk_ref[0, lo : lo + rows, :]
        accs = [jnp.full(bk.shape, -jnp.inf, F32) for _ in range(DSA_HEADS)]
        for b in range(T5_BUCKETS):
            hit = bk == b
            accs = [jnp.where(hit, rb_ref[b, h] * LOG2_E, acc) for h, acc in enumerate(accs)]
        for h, acc in enumerate(accs):
            o_ref[0, 0, h, lo : lo + rows, :] = acc
            o_ref[0, 1, h, lo : lo + rows, :] = jnp.concatenate([acc[:, blk:], masked], axis=1)


def _dsa_bias(rel_bias):
    buckets = _dsa_bucket_tables()
    P, blk = len(DSA_PATTERNS), DSA_BLOCK
    return pl.pallas_call(
        _dsa_bias_kernel,
        grid=(P,),
        in_specs=[pl.BlockSpec(memory_space=pltpu.SMEM), pl.BlockSpec((1, blk, 2 * blk), lambda p: (p, 0, 0))],
        out_specs=pl.BlockSpec((1, 2, DSA_HEADS, blk, 2 * blk), lambda p: (p, 0, 0, 0, 0)),
        out_shape=jax.ShapeDtypeStruct((P, 2, DSA_HEADS, blk, 2 * blk), F32),
        compiler_params=_params(("parallel",)),
        name="dsa_bias",
    )(rel_bias, buckets)


def _dsa_kernel(*refs):
    P = len(DSA_PATTERNS)
    q_refs, k_refs, v_refs = refs[:P], refs[P : 2 * P], refs[2 * P : 3 * P]
    bias_ref, o_ref, acc_o, acc_m, acc_l = refs[3 * P :]
    blk = DSA_BLOCK
    tq = o_ref.shape[1]
    t = pl.program_id(2)
    lane = lax.broadcasted_iota(jnp.int32, (1, V7X_LANES), 1)
    head0 = lane < DSA_HEAD_DIM

    def block_logits(p, r, idx):
        blocks_per_residue = tq // (blk * r)
        rho = idx // blocks_per_residue
        nb = idx % blocks_per_residue
        l0 = pl.multiple_of(t * (tq // r) + nb * blk, blk)
        first = l0 == 0
        k_row = pl.multiple_of(jnp.where(first, l0, l0 - blk), blk)
        qb = q_refs[p][0, rho, pl.ds(l0, blk), :]
        zero = jnp.zeros_like(qb)
        q2 = jnp.concatenate([jnp.where(head0, qb, zero), jnp.where(head0, zero, qb)], axis=0)
        bias = bias_ref[p, pl.ds(first.astype(jnp.int32), 1), 0][0]
        kb = k_refs[p][0, rho, pl.ds(k_row, 2 * blk), :]
        logits = lax.dot_general(q2, kb, _NT, preferred_element_type=F32) + bias
        start = rho + nb * (blk * r)
        rows = pl.ds(start, blk, stride=r) if r > 1 else pl.ds(start, blk)
        return logits, (rho, k_row), rows

    def block_output(p, logits, key_rows, rows):
        rho, k_row = key_rows
        m = jnp.max(logits, -1, keepdims=True)
        e = jnp.exp2(logits - m).astype(BF16)
        vb = v_refs[p][0, rho, pl.ds(k_row, 2 * blk), :]
        o2 = jnp.dot(e, jnp.concatenate([vb, jnp.ones_like(vb)], axis=1), preferred_element_type=F32)
        o_p = jnp.where(head0, o2[:blk, :V7X_LANES], o2[blk:, :V7X_LANES])
        m_p = jnp.where(head0, m[:blk], m[blk:])
        l_p = jnp.where(head0, o2[:blk, V7X_LANES:], o2[blk:, V7X_LANES:])
        if p > 0:
            acc_o.at[p - 1][rows, :] = o_p
            acc_m.at[p - 1][rows, :] = m_p
            acc_l.at[p - 1][rows, :] = l_p
            return
        os_ = [o_p] + [acc_o[b, rows, :] for b in range(P - 1)]
        ms = [m_p] + [acc_m[b, rows, :] for b in range(P - 1)]
        ls = [l_p] + [acc_l[b, rows, :] for b in range(P - 1)]
        m_all = functools.reduce(jnp.maximum, ms)
        ws = [jnp.exp2(m_b - m_all) for m_b in ms]
        num = functools.reduce(jnp.add, [w * o_b for w, o_b in zip(ws, os_)])
        den = functools.reduce(jnp.add, [w * l_b for w, l_b in zip(ws, ls)])
        o_ref[0, rows, :] = (num / den).astype(o_ref.dtype)

    blocks = [(p, DSA_PATTERNS[p][1], idx) for p in reversed(range(P)) for idx in range(tq // blk)]
    staged = [block_logits(*blocks[i]) for i in range(DSA_GROUP)]
    for i, (p, _, _) in enumerate(blocks):
        if i + DSA_GROUP < len(blocks):
            staged.append(block_logits(*blocks[i + DSA_GROUP]))
        block_output(p, *staged[i])


def _dilated_attention(ds, bias):
    B, _, S, _ = ds[0].shape
    pairs = DSA_W // V7X_LANES
    tq = DSA_TILE
    assert S % tq == 0 and S >= 2 * tq
    P, blk = len(DSA_PATTERNS), DSA_BLOCK
    bias5 = bias.reshape(P, 2, pairs, 2 * blk, 2 * blk)

    def seq(r, off):
        return pl.BlockSpec((1, r, S // r, V7X_LANES), lambda b, p, t: (b, 0, 0, off + p))

    return pl.pallas_call(
        _dsa_kernel,
        grid=(B, pairs, S // tq),
        in_specs=[seq(r, off) for off in (0, pairs, 2 * pairs) for _, r in DSA_PATTERNS]
        + [pl.BlockSpec((P, 2, 1, 2 * blk, 2 * blk), lambda b, p, t: (0, 0, p, 0, 0))],
        out_specs=pl.BlockSpec((1, tq, V7X_LANES), lambda b, p, t: (b, t, p)),
        out_shape=jax.ShapeDtypeStruct((B, S, DSA_W), BF16),
        scratch_shapes=[pltpu.VMEM((P - 1, tq, V7X_LANES), F32)] * 3,
        compiler_params=_params(("parallel", "parallel", "arbitrary")),
        name="dilated_attention",
    )(*ds, *ds, *ds, bias5)


def _mlstm_kernel(q_ref, kt_ref, v_ref, og_ref, gr_ref, o_ref, c_ref, m_ref):
    h = pl.program_id(1)

    @pl.when(pl.program_id(2) == 0)
    def _():
        c_ref[...] = jnp.zeros_like(c_ref)
        m_ref[...] = jnp.zeros_like(m_ref)

    C = MLSTM_CHUNK
    dv = MLSTM_HEAD_DIM
    L = V7X_LANES
    n_chunks = q_ref.shape[1] // C
    ri = lax.broadcasted_iota(jnp.int32, (C, C), 0)
    ci = lax.broadcasted_iota(jnp.int32, (C, C), 1)
    causal = ci <= ri
    eye = ci == ri
    rep = lambda col: jnp.broadcast_to(col, (C, L))

    def gate_vectors(n):
        sl = slice(n * C, (n + 1) * C)
        gates = gr_ref[:, sl]
        gate_id = lax.broadcasted_iota(jnp.int32, (2 * MLSTM_HEADS, 1), 0)
        li_row = jnp.sum(jnp.where(gate_id == h, gates, 0.0), 0, keepdims=True)
        lf_row = jnp.sum(jnp.where(gate_id == MLSTM_HEADS + h, gates, 0.0), 0, keepdims=True)
        li_col = jnp.sum(jnp.where(eye, li_row, 0.0), -1, keepdims=True)
        a_col = jnp.sum(jnp.where(causal, lf_row, 0.0), -1, keepdims=True)
        g_tot = jnp.sum(lf_row, -1, keepdims=True)
        return li_row, li_col, a_col, g_tot

    def decay_matrix(gv):
        li_row, li_col, a_col, g_tot = gv
        a_rep = rep(a_col)
        a_row = jnp.sum(jnp.where(eye, a_rep, 0.0), 0, keepdims=True)
        w_state = g_tot - a_col + li_col
        m_loc = jnp.max(w_state, 0, keepdims=True)
        ew_rep = rep(jnp.exp(w_state - m_loc))
        dmat = jnp.where(causal, a_rep - a_row + li_row, -jnp.inf)
        d_max = jnp.max(dmat, -1, keepdims=True)
        return a_rep, g_tot, m_loc, ew_rep, dmat, d_max

    def local_matmuls(n, dm, scalars):
        a_rep, _, _, ew_rep, dmat, d_max = dm
        m_s, a_old, a_new = scalars
        ew_rep = ew_rep * a_new
        sl = slice(n * C, (n + 1) * C)
        qb = q_ref[0, sl, :]
        kt = kt_ref[:, sl]
        vb = v_ref[0, sl, :]
        d_rep = rep(d_max)
        p = (jnp.exp(dmat - d_rep) * jnp.dot(qb, kt, preferred_element_type=F32)).astype(BF16)
        pv = jnp.dot(p, jnp.concatenate([vb, jnp.ones((C, L), BF16)], axis=1), preferred_element_type=F32)
        vw = jnp.concatenate([vb.astype(F32) * jnp.concatenate([ew_rep] * (dv // L), axis=1), ew_rep], axis=1)
        kv_ext = jnp.dot(kt, vw.astype(BF16), preferred_element_type=F32)
        return qb, pv, kv_ext, a_rep, d_rep, m_s, a_old

    def recurrent(n, staged):
        qb, pv, kv_ext, a_rep, d_rep, m_s, a_old = staged
        sl = slice(n * C, (n + 1) * C)
        c_s = c_ref[...]
        inter =jnp.dot(qb, c_s.astype(BF16), preferred_element_type=F32)
        inter_log = a_rep + m_s
        m_row = jnp.maximum(inter_log, d_rep)
        f_intra = jnp.exp(d_rep - m_row)
        f_inter = jnp.exp(inter_log - m_row)
        den = f_intra * pv[:, dv:] + f_inter * inter[:, dv:]
        r = 1.0 / jnp.maximum(jnp.abs(den), jnp.exp(-m_row))
        for lo in range(0, dv, L):
            num = f_intra * pv[:, lo : lo + L] + f_inter * inter[:, lo : lo + L]
            o_ref[0, sl, lo : lo + L] = (num * r * og_ref[0, sl, lo : lo + L]).astype(o_ref.dtype)
        c_ref[...] = a_old * c_s + kv_ext

    gvs = [gate_vectors(n) for n in range(n_chunks)]
    dms = [decay_matrix(gv) for gv in gvs]
    m_s = m_ref[...][:, 0:1]
    scalars = []
    for _, g_tot, m_loc, _, _, _ in dms:
        m_new = jnp.maximum(g_tot + m_s, m_loc)
        scalars.append((m_s, jnp.exp(g_tot + m_s - m_new), jnp.exp(m_loc - m_new)))
        m_s = m_new
    m_ref[...] = jnp.broadcast_to(m_s, m_ref.shape)
    _lookahead(n_chunks, lambda n: local_matmuls(n, dms[n], scalars[n]), recurrent)


def _mlstm(q3, kt, v3, o3, gates_row):
    B, S, _ = v3.shape
    H, d = MLSTM_HEADS, MLSTM_HEAD_DIM
    ts = min(SEQ_TILE, S)
    tiles = S // ts
    col = pl.BlockSpec((1, ts, d), lambda b, h, c: (b, c, h))
    return pl.pallas_call(
        _mlstm_kernel,
        grid=(B, H, tiles),
        in_specs=[col, pl.BlockSpec((d, ts), lambda b, h, c: (h, b * tiles + c)), col, col,
                  pl.BlockSpec((2 * H, ts), lambda b, h, c: (0, b * tiles + c))],
        out_specs=pl.BlockSpec((1, ts, d), lambda b, h, c: (b, c, h)),
        out_shape=jax.ShapeDtypeStruct((B, S, H * d), BF16),
        scratch_shapes=[pltpu.VMEM((d, d + V7X_LANES), F32), pltpu.VMEM((1, V7X_LANES), F32)],
        compiler_params=_params(("parallel", "parallel", "arbitrary")),
        name="mlstm",
    )(q3, kt, v3, o3, gates_row)


def _even_mixer(x2d, B, S, w_in, rel_bias):
    tables = _retention_tables(S)
    q, kt, v, gate, d_nat, *d_dil = _even_in_proj(x2d, S, w_in.astype(BF16), tables[0], tables[1])
    y_r = _retention(q.reshape(B, S, -1), kt, v.reshape(B, S, -1), gate.reshape(B, S, -1), tables)
    y_d = _dilated_attention([d_nat.reshape(B, 1, S, -1)] + d_dil, _dsa_bias(rel_bias))
    return [y_r.reshape(B * S, RET_W), y_d.reshape(B * S, DSA_W)]


def _odd_mixer(x2d, B, S, w_in, gate_b, conv_w):
    H = MLSTM_HEADS
    wide = 4 * MLSTM_W
    wg = jnp.zeros((D_MODEL, V7X_LANES), F32).at[:, : 2 * H].set(w_in[:, wide:]).astype(BF16)
    gb = jnp.zeros((1, V7X_LANES), F32).at[0, : 2 * H].set(gate_b)
    q, kt, v, og, gates = _odd_in_proj(x2d, S, w_in.astype(BF16), wg, gb, conv_w)
    y = _mlstm(q.reshape(B, S, -1), kt, v.reshape(B, S, -1), og.reshape(B, S, -1), gates)
    return [y.reshape(B * S, MLSTM_W)]


def kernel(x, even_w_in, even_w_out, rel_bias, odd_w_in, odd_gate_b, odd_conv_w, odd_w_out, ffn_w_up, ffn_conv_w, ffn_conv_b, ffn_w_down, ln_g, ln_b):
    B, S, D = x.shape
    x2d = x.reshape(B * S, D)
    w_up, w_down = ffn_w_up.astype(BF16), ffn_w_down.astype(BF16)
    for layer in range(DEPTH):
        j = layer // 2
        if layer % 2 == 0:
            ys, w_out = _even_mixer(x2d, B, S, even_w_in[j], rel_bias), even_w_out[j]
        else:
            ys, w_out = _odd_mixer(x2d, B, S, odd_w_in[j], odd_gate_b[j], odd_conv_w[j]), odd_w_out[j]
        x2d = _layer_tail(x2d, ys, S, layer, w_out.astype(BF16), (ln_g[layer, 0], ln_b[layer, 0]), w_up,
                          ffn_conv_w[layer], ffn_conv_b[layer], w_down, (ln_g[layer, 1], ln_b[layer, 1]))
    return x2d.reshape(B, S, D)
```

```python
import functools
import math

import jax
import jax.numpy as jnp
import numpy as np
from jax import lax
from jax.experimental import pallas as pl
from jax.experimental.pallas import tpu as pltpu

F32 = jnp.float32
BF16 = jnp.bfloat16

D_MODEL = 1024
DEPTH = 2
RET_HEADS = 4
RET_HEAD_DIM = 128
RET_CHUNK = 128
DSA_HEADS = 8
DSA_HEAD_DIM = 64
DSA_PATTERNS = ((128, 1), (512, 4), (2048, 16))
DSA_BLOCK = 128
T5_BUCKETS = 32
T5_MAX_DIST = 2048
MLSTM_HEADS = 4
MLSTM_HEAD_DIM = 256
MLSTM_CHUNK = 128
MLSTM_CONV = 4
D_FF = 2816
FFN_CONV = 3
LN_EPS = 1e-5
DEEPNORM_ALPHA = (2.0 * DEPTH) ** 0.25
LOG2_E = math.log2(math.e)

RET_W = RET_HEADS * RET_HEAD_DIM
DSA_W = DSA_HEADS * DSA_HEAD_DIM
MLSTM_W = MLSTM_HEADS * MLSTM_HEAD_DIM

V7X_LANES = 128
V7X_SUBLANES = 8
V7X_VMEM_BYTES = 64 * 1024 * 1024
VMEM_LIMIT_BYTES = V7X_VMEM_BYTES - 8 * 1024 * 1024

ROW_TILE = 512
PROJ_CHUNK = 512
FFN_CHUNK = 256
SEQ_TILE = 4096
DSA_TILE = DSA_BLOCK * DSA_PATTERNS[-1][1]
DSA_GROUP = 2

_NT = (((1,), (1,)), ((), ()))
_TN = (((0,), (0,)), ((), ()))


def _params(semantics):
    return pltpu.CompilerParams(dimension_semantics=semantics, vmem_limit_bytes=VMEM_LIMIT_BYTES)


def _resident(shape):
    nd = len(shape)
    return pl.BlockSpec(shape, lambda *_: (0,) * nd, pipeline_mode=pl.Buffered(1))


def _resident_layer(stacked_shape, layer):
    return pl.BlockSpec((None,) + tuple(stacked_shape[1:]), lambda *_: (layer, 0, 0), pipeline_mode=pl.Buffered(1))


def _layer_norm(z, g, b):
    mu = jnp.mean(z, -1, keepdims=True)
    zc = z - mu
    var = jnp.mean(zc * zc, -1, keepdims=True)
    return zc * lax.rsqrt(var + LN_EPS) * g + b


def _silu(z):
    return z * jax.nn.sigmoid(z)


def _log_sigmoid(x):
    return jnp.minimum(x, 0.0) - jnp.log1p(jnp.exp(-jnp.abs(x)))


def _lookahead(n, produce, consume, depth=1):
    staged = [produce(j) for j in range(min(depth, n))]
    for j in range(n):
        if j + depth < n:
            staged.append(produce(j + depth))
        consume(j, staged[j])
        staged[j] = None


def _shift_rows(x, prev, s):
    if s == 0:
        return x
    rolled = pltpu.roll(x, s, 0)
    head = pltpu.roll(prev, s, 0)
    rid = lax.broadcasted_iota(jnp.int32, (V7X_SUBLANES, x.shape[1]), 0)
    first = jnp.where(rid < s, head, rolled[:V7X_SUBLANES])
    return jnp.concatenate([first, rolled[V7X_SUBLANES:]], axis=0)


def _causal_dwconv(x, prev, w_ref, taps):
    acc = None
    for k in range(taps):
        term = _shift_rows(x, prev, taps - 1 - k) * w_ref[k : k + 1, :]
        acc = term if acc is None else acc + term
    return acc


def _even_in_proj_kernel(x_ref, w_ref, cos_ref, sin_ref, q_ref, kt_ref, v_ref, g_ref, *rest):
    d_refs, stage_ref = rest[:-1], rest[-1]
    xb = x_ref[...].astype(BF16)
    tm = xb.shape[0]
    cos = cos_ref[...]
    sin = sin_ref[...]
    tn, d = RET_W, RET_HEAD_DIM

    def consume(j, h):
        if j < 2:
            for hh in range(RET_HEADS):
                c = h[:, hh * d : (hh + 1) * d]
                rot = c * cos + pltpu.roll(c, d // 2, 1) * sin
                if j == 0:
                    q_ref[:, hh * d : (hh + 1) * d] = rot.astype(BF16)
                else:
                    kt_ref[hh * d : (hh + 1) * d, :] = rot.T.astype(BF16)
        elif j == 2:
            v_ref[...] = h.astype(BF16)
        elif j == 3:
            g_ref[...] = _silu(h).astype(BF16)
        else:
            if j == 4:
                h = h * (DSA_HEAD_DIM ** -0.5 * LOG2_E)
            lo = (j - 4) * tn
            d_refs[0][:, lo : lo + tn] = h.astype(BF16)
            src, dst = stage_ref.at[0], stage_ref.at[1]
            for s in range(tn // V7X_LANES):
                src[s] = h[:, s * V7X_LANES : (s + 1) * V7X_LANES]
            prev_r = 1
            for d_ref, (_, r) in zip(d_refs[1:], DSA_PATTERNS[1:]):
                step = r // prev_r
                n_rows = tm // r
                for rho_prev in range(prev_r):
                    for a in range(step):
                        rho = rho_prev + prev_r * a
                        for s in range(tn // V7X_LANES):
                            rows = src.at[s][pl.ds(rho_prev * (tm // prev_r) + a, n_rows, stride=step), :]
                            if r != DSA_PATTERNS[-1][1]:
                                dst[s, rho * n_rows : (rho + 1) * n_rows, :] = rows
                            d_ref[0, rho, :, lo + s * V7X_LANES : lo + (s + 1) * V7X_LANES] = rows.astype(BF16)
                src, dst, prev_r = dst, src, r

    _lookahead(w_ref.shape[1] // tn, lambda j: jnp.dot(xb, w_ref[:, j * tn : (j + 1) * tn], preferred_element_type=F32), consume)


def _even_in_proj(x2d, seq_len, w, cos_t, sin_t):
    T, K = x2d.shape
    tm = min(2 * ROW_TILE, seq_len)
    tps = seq_len // tm
    B = T // seq_len
    row = lambda i: (i, 0)
    pos = pl.BlockSpec((tm, RET_HEAD_DIM), lambda i: (i % tps, 0))
    shapes = [(T, RET_W), (RET_W, T), (T, RET_W), (T, RET_W), (T, 3 * DSA_W)]
    dtypes = [BF16, BF16, BF16, BF16, BF16]
    specs = [pl.BlockSpec((tm, RET_W), row), pl.BlockSpec((RET_W, tm), lambda i: (0, i)),
             pl.BlockSpec((tm, RET_W), row), pl.BlockSpec((tm, RET_W), row), pl.BlockSpec((tm, 3 * DSA_W), row)]
    for _, r in DSA_PATTERNS[1:]:
        shapes.append((B, r, seq_len // r, 3 * DSA_W))
        dtypes.append(BF16)
        specs.append(pl.BlockSpec((1, r, tm // r, 3 * DSA_W), lambda i: (i // tps, 0, i % tps, 0)))
    return pl.pallas_call(
        _even_in_proj_kernel,
        grid=(T // tm,),
        in_specs=[pl.BlockSpec((tm, K), row), _resident(w.shape), pos, pos],
        out_specs=specs,
        out_shape=[jax.ShapeDtypeStruct(s, dt) for s, dt in zip(shapes, dtypes)],
        scratch_shapes=[pltpu.VMEM((2, RET_W // V7X_LANES, tm, V7X_LANES), F32)],
        compiler_params=_params(("parallel",)),
        name="even_in_proj",
    )(x2d, w, cos_t, sin_t)


def _odd_in_proj_kernel(x_ref, w_ref, wg_ref, gb_ref, cw_ref, q_ref, kt_ref, v_ref, o_ref, gt_ref, carry_ref, *, tiles_per_seq):
    tn = PROJ_CHUNK
    per = MLSTM_W // tn

    @pl.when(pl.program_id(0) % tiles_per_seq == 0)
    def _():
        carry_ref[...] = jnp.zeros_like(carry_ref)

    xb = x_ref[...].astype(BF16)
    tm = xb.shape[0]

    order = [c for pair in zip(range(2 * per), range(2 * per, 4 * per)) for c in pair]

    def produce(i):
        j = order[i]
        return jnp.dot(xb, w_ref[:, j * tn : (j + 1) * tn], preferred_element_type=F32)

    def consume(i, h):
        j = order[i]
        sl = slice(j * tn, (j + 1) * tn)
        if j < 2 * per:
            prev = carry_ref[:, sl]
            carry_ref[:, sl] = h[tm - V7X_SUBLANES :, :]
            act = _silu(_causal_dwconv(h, prev, cw_ref.at[:, sl], MLSTM_CONV))
            if j < per:
                q_ref[:, sl] = act.astype(BF16)
            else:
                kt_ref[(j - per) * tn : (j - per + 1) * tn, :] = (act * (MLSTM_HEAD_DIM ** -0.5)).T.astype(BF16)
        elif j < 3 * per:
            v_ref[:, (j - 2 * per) * tn : (j - 2 * per + 1) * tn] = h.astype(BF16)
        else:
            o_ref[:, (j - 3 * per) * tn : (j - 3 * per + 1) * tn] = jax.nn.sigmoid(h).astype(BF16)

    _lookahead(4 * per, produce, consume)
    gates = (jnp.dot(xb, wg_ref[...], preferred_element_type=F32) + gb_ref[...]).T[: 2 * MLSTM_HEADS, :]
    gate_id = lax.broadcasted_iota(jnp.int32, (2 * MLSTM_HEADS, 1), 0)
    gt_ref[...] = jnp.where(gate_id < MLSTM_HEADS, gates, _log_sigmoid(gates))


def _odd_in_proj(x2d, seq_len, w, wg, gb, conv_w):
    T, K = x2d.shape
    tm = min(2 * ROW_TILE, seq_len)
    W = MLSTM_W
    cw = jnp.zeros((V7X_SUBLANES, 2 * W), F32).at[:MLSTM_CONV].set(conv_w)
    row = lambda i: (i, 0)
    shapes = ((T, W), (W, T), (T, W), (T, W), (2 * MLSTM_HEADS, T))
    dtypes = (BF16, BF16, BF16, BF16, F32)
    return pl.pallas_call(
        functools.partial(_odd_in_proj_kernel, tiles_per_seq=seq_len // tm),
        grid=(T // tm,),
        in_specs=[pl.BlockSpec((tm, K), row), _resident(w.shape), _resident(wg.shape), _resident(gb.shape),
                  _resident(cw.shape)],
        out_specs=[pl.BlockSpec((tm, W), row), pl.BlockSpec((W, tm), lambda i: (0, i)), pl.BlockSpec((tm, W), row),
                   pl.BlockSpec((tm, W), row), pl.BlockSpec((2 * MLSTM_HEADS, tm), lambda i: (0, i))],
        out_shape=[jax.ShapeDtypeStruct(s, dt) for s, dt in zip(shapes, dtypes)],
        scratch_shapes=[pltpu.VMEM((V7X_SUBLANES, 2 * MLSTM_W), F32)],
        compiler_params=_params(("arbitrary",)),
        name="odd_in_proj",
    )(x2d, w, wg, gb, cw)


def _tail_kernel(*refs, n_y, tiles_per_seq):
    x_ref = refs[0]
    y_refs = refs[1 : 1 + n_y]
    (wout_ref, g1_ref, b1_ref, wup_ref, cw_ref, cb_ref, wdn_ref, g_ref, b_ref,
     o_ref, carry_ref, act_ref, x1_ref) = refs[1 + n_y :]

    @pl.when(pl.program_id(0) % tiles_per_seq == 0)
    def _():
        carry_ref[...] = jnp.zeros_like(carry_ref)

    tm = x_ref.shape[0]
    for lo in range(0, tm, tm // 2):
        rows = slice(lo, lo + tm // 2)
        acc = None
        row = 0
        for y_ref in y_refs:
            kw = y_ref.shape[1]
            part = jnp.dot(y_ref[rows, :], wout_ref[row : row + kw, :], preferred_element_type=F32)
            acc = part if acc is None else acc + part
            row += kw
        x1_ref[rows, :] = _layer_norm(DEEPNORM_ALPHA * x_ref[rows, :] + acc, g1_ref[...], b1_ref[...])

    xb = x1_ref[...].astype(BF16)

    def up_dots(j):
        lo = j * FFN_CHUNK
        return (jnp.dot(xb, wup_ref[:, lo : lo + FFN_CHUNK], preferred_element_type=F32),
                jnp.dot(xb, wup_ref[:, D_FF + lo : D_FF + lo + FFN_CHUNK], preferred_element_type=F32))

    def activate(j, gate_up):
        gate, up = gate_up
        sl = slice(j * FFN_CHUNK, (j + 1) * FFN_CHUNK)
        prev = carry_ref[:, sl]
        carry_ref[:, sl] = gate[tm - V7X_SUBLANES :, :]
        conv = _causal_dwconv(gate, prev, cw_ref.at[:, sl], FFN_CONV) + cb_ref[:, sl]
        act_ref[:, sl] = (_silu(conv) * up).astype(BF16)

    _lookahead(D_FF // FFN_CHUNK, up_dots, activate)
    for lo in range(0, tm, tm // 2):
        rows = slice(lo, lo + tm // 2)
        ffn = jnp.dot(act_ref[rows, :], wdn_ref[...], preferred_element_type=F32)
        z = DEEPNORM_ALPHA * x1_ref[rows, :] + ffn
        o_ref[rows, :] = _layer_norm(z, g_ref[...], b_ref[...])


def _layer_tail(x2d, ys, seq_len, layer, w_out, ln1, w_up, conv_w, conv_b, w_down, ln2):
    T, D = x2d.shape
    tm = min(ROW_TILE, seq_len)
    cw = jnp.zeros((V7X_SUBLANES, D_FF), F32).at[:FFN_CONV].set(conv_w)
    row = lambda i: (i, 0)
    vec = lambda v: v.reshape(1, -1)
    return pl.pallas_call(
        functools.partial(_tail_kernel, n_y=len(ys), tiles_per_seq=seq_len // tm),
        grid=(T // tm,),
        in_specs=[pl.BlockSpec((tm, D), row)]
        + [pl.BlockSpec((tm, y.shape[1]), row) for y in ys]
        + [_resident(w_out.shape), _resident((1, D)), _resident((1, D)),
           _resident_layer(w_up.shape, layer), _resident(cw.shape), _resident((1, D_FF)),
           _resident_layer(w_down.shape, layer), _resident((1, D)), _resident((1, D))],
        out_specs=pl.BlockSpec((tm, D), row),
        out_shape=jax.ShapeDtypeStruct((T, D), F32),
        scratch_shapes=[pltpu.VMEM((V7X_SUBLANES, D_FF), F32), pltpu.VMEM((tm, D_FF), BF16), pltpu.VMEM((tm, D), F32)],
        compiler_params=_params(("arbitrary",)),
        name="layer_tail",
    )(x2d, *ys, w_out, vec(ln1[0]), vec(ln1[1]), w_up, cw, vec(conv_b), w_down, vec(ln2[0]), vec(ln2[1]))


def _retention_tables(seq_len):
    H, C, d = RET_HEADS, RET_CHUNK, RET_HEAD_DIM
    scale = d ** -0.5
    log_gamma = jnp.log1p(-jnp.exp2(-5.0 - jnp.arange(H, dtype=F32)))
    idx = jnp.arange(C, dtype=F32)
    rel = idx[:, None] - idx[None, :]
    decay = jnp.where(rel >= 0, jnp.exp(log_gamma[:, None, None] * jnp.maximum(rel, 0.0)), 0.0) * scale
    k_w = jnp.exp(log_gamma[:, None] * (C - 1 - idx)[None, :]) * scale
    q_w = jnp.exp(log_gamma[:, None] * (idx + 1.0)[None, :])
    kw_t = jnp.broadcast_to(k_w[:, :, None], (H, C, d))
    qw_t = jnp.broadcast_to(q_w[:, :, None], (H, C, d))
    cd_t = jnp.broadcast_to(jnp.exp(log_gamma * C)[:, None, None], (H, 1, d))
    inv = np.float32(1.0) / np.float32(10000.0) ** (np.arange(0, d, 2, dtype=np.float32) / np.float32(d))
    ang = (np.arange(seq_len, dtype=np.float32)[:, None] * inv[None, :]).astype(np.float64)
    cos, sin = np.cos(ang).astype(np.float32), np.sin(ang).astype(np.float32)
    cos_t = jnp.asarray(np.concatenate([cos, cos], -1), F32)
    sin_t = jnp.asarray(np.concatenate([-sin, sin], -1), F32)
    return cos_t, sin_t, decay, kw_t, qw_t, cd_t


def _retention_kernel(q_ref, kt_ref, v_ref, g_ref, dec_ref, kw_ref, qw_ref, cd_ref, o_ref, r_ref):
    @pl.when(pl.program_id(2) == 0)
    def _():
        r_ref[...] = jnp.zeros_like(r_ref)

    C = RET_CHUNK
    decay = dec_ref[0]
    kw = kw_ref[0]
    qw = qw_ref[0]
    cd = cd_ref[0]

    def local(n):
        sl = slice(n * C, (n + 1) * C)
        qb = q_ref[0, sl, :]
        kt = kt_ref[:, sl]
        vb = v_ref[0, sl, :]
        scores = jnp.dot(qb, kt, preferred_element_type=F32) * decay
        y_intra = jnp.dot(scores.astype(BF16), vb, preferred_element_type=F32)
        kv = jnp.dot(kt, (vb.astype(F32) * kw).astype(BF16), preferred_element_type=F32)
        q_dec = (qb.astype(F32) * qw).astype(BF16)
        return y_intra, kv, q_dec

    def recurrent(n, staged):
        y_intra, kv, q_dec = staged
        sl = slice(n * C, (n + 1) * C)
        r = r_ref[...]
        y = y_intra + jnp.dot(q_dec, r.astype(BF16), preferred_element_type=F32)
        r_ref[...] = cd * r + kv
        mu = jnp.mean(y, -1, keepdims=True)
        yc = y - mu
        var = jnp.mean(yc * yc, -1, keepdims=True)
        o_ref[0, sl, :] = (yc * lax.rsqrt(var + LN_EPS) * g_ref[0, sl, :]).astype(o_ref.dtype)

    _lookahead(q_ref.shape[1] // C, local, recurrent)


def _retention(q3, kt, v3, g3, tables):
    B, S, _ = v3.shape
    H, d, C = RET_HEADS, RET_HEAD_DIM, RET_CHUNK
    ts = min(SEQ_TILE, S)
    tiles = S // ts
    _, _, decay, kw_t, qw_t, cd_t = tables
    col = pl.BlockSpec((1, ts, d), lambda b, h, c: (b, c, h))
    per_head = lambda shape: pl.BlockSpec((1,) + shape, lambda b, h, c: (h, 0, 0))
    return pl.pallas_call(
        _retention_kernel,
        grid=(B, H, tiles),
        in_specs=[col, pl.BlockSpec((d, ts), lambda b, h, c: (h, b * tiles + c)), col, col,
                  per_head((C, C)), per_head((C, d)), per_head((C, d)), per_head((1, d))],
        out_specs=pl.BlockSpec((1, ts, d), lambda b, h, c: (b, c, h)),
        out_shape=jax.ShapeDtypeStruct((B, S, H * d), BF16),
        scratch_shapes=[pltpu.VMEM((d, d), F32)],
        compiler_params=_params(("parallel", "parallel", "arbitrary")),
        name="retention",
    )(q3, kt, v3, g3, decay, kw_t, qw_t, cd_t)


def _t5_bucket(dist):
    exact = T5_BUCKETS // 2
    n = jnp.maximum(dist, 0)
    large = exact + (jnp.log(jnp.maximum(n, 1).astype(F32) / exact) / math.log(T5_MAX_DIST / exact) * (T5_BUCKETS - exact)).astype(jnp.int32)
    large = jnp.minimum(large, T5_BUCKETS - 1)
    return jnp.where(n < exact, n, large)


def _dsa_bucket_tables():
    blk = DSA_BLOCK
    qi = jnp.arange(blk)[:, None]
    kj = jnp.arange(2 * blk)[None, :]
    tabs = []
    for window, dilation in DSA_PATTERNS:
        assert window // dilation <= blk
        dist = qi + blk - kj
        valid = (dist >= 0) & (dist <= window // dilation)
        tabs.append(jnp.where(valid, _t5_bucket(dist * dilation), -1))
    return jnp.stack(tabs).astype(jnp.int32)


def _dsa_bias_kernel(rb_ref, bk_ref, o_ref):
    blk = DSA_BLOCK
    rows = 4 * V7X_SUBLANES
    masked = jnp.full((rows, blk), -jnp.inf, F32)
    for lo in range(0, blk, rows):
        bk = bk_ref[0, lo : lo + rows, :]
        accs = [jnp.full(bk.shape, -jnp.inf, F32) for _ in range(DSA_HEADS)]
        for b in range(T5_BUCKETS):
            hit = bk == b
            accs = [jnp.where(hit, rb_ref[b, h] * LOG2_E, acc) for h, acc in enumerate(accs)]
        for h, acc in enumerate(accs):
            o_ref[0, 0, h, lo : lo + rows, :] = acc
            o_ref[0, 1, h, lo : lo + rows, :] = jnp.concatenate([acc[:, blk:], masked], axis=1)


def _dsa_bias(rel_bias):
    buckets = _dsa_bucket_tables()
    P, blk = len(DSA_PATTERNS), DSA_BLOCK
    return pl.pallas_call(
        _dsa_bias_kernel,
        grid=(P,),
        in_specs=[pl.BlockSpec(memory_space=pltpu.SMEM), pl.BlockSpec((1, blk, 2 * blk), lambda p: (p, 0, 0))],
        out_specs=pl.BlockSpec((1, 2, DSA_HEADS, blk, 2 * blk), lambda p: (p, 0, 0, 0, 0)),
        out_shape=jax.ShapeDtypeStruct((P, 2, DSA_HEADS, blk, 2 * blk), F32),
        compiler_params=_params(("parallel",)),
        name="dsa_bias",
    )(rel_bias, buckets)


def _dsa_kernel(*refs):
    P = len(DSA_PATTERNS)
    q_refs, k_refs, v_refs = refs[:P], refs[P : 2 * P], refs[2 * P : 3 * P]
    bias_ref, o_ref, acc_o, acc_m, acc_l = refs[3 * P :]
    blk = DSA_BLOCK
    tq = o_ref.shape[1]
    t = pl.program_id(2)
    lane = lax.broadcasted_iota(jnp.int32, (1, V7X_LANES), 1)
    head0 = lane < DSA_HEAD_DIM

    def block_logits(p, r, idx):
        blocks_per_residue = tq // (blk * r)
        rho = idx // blocks_per_residue
        nb = idx % blocks_per_residue
        l0 = pl.multiple_of(t * (tq // r) + nb * blk, blk)
        first = l0 == 0
        k_row = pl.multiple_of(jnp.where(first, l0, l0 - blk), blk)
        qb = q_refs[p][0, rho, pl.ds(l0, blk), :]
        zero = jnp.zeros_like(qb)
        q2 = jnp.concatenate([jnp.where(head0, qb, zero), jnp.where(head0, zero, qb)], axis=0)
        bias = bias_ref[p, pl.ds(first.astype(jnp.int32), 1), 0][0]
        kb = k_refs[p][0, rho, pl.ds(k_row, 2 * blk), :]
        logits = lax.dot_general(q2, kb, _NT, preferred_element_type=F32) + bias
        start = rho + nb * (blk * r)
        rows = pl.ds(start, blk, stride=r) if r > 1 else pl.ds(start, blk)
        return logits, (rho, k_row), rows

    def block_output(p, logits, key_rows, rows):
        rho, k_row = key_rows
        m = jnp.max(logits, -1, keepdims=True)
        e = jnp.exp2(logits - m).astype(BF16)
        vb = v_refs[p][0, rho, pl.ds(k_row, 2 * blk), :]
        o2 = jnp.dot(e, jnp.concatenate([vb, jnp.ones_like(vb)], axis=1), preferred_element_type=F32)
        o_p = jnp.where(head0, o2[:blk, :V7X_LANES], o2[blk:, :V7X_LANES])
        m_p = jnp.where(head0, m[:blk], m[blk:])
        l_p = jnp.where(head0, o2[:blk, V7X_LANES:], o2[blk:, V7X_LANES:])
        if p > 0:
            acc_o.at[p - 1][rows, :] = o_p
            acc_m.at[p - 1][rows, :] = m_p
            acc_l.at[p - 1][rows, :] = l_p
            return
        os_ = [o_p] + [acc_o[b, rows, :] for b in range(P - 1)]
        ms = [m_p] + [acc_m[b, rows, :] for b in range(P - 1)]
        ls = [l_p] + [acc_l[b, rows, :] for b in range(P - 1)]
        m_all = functools.reduce(jnp.maximum, ms)
        ws = [jnp.exp2(m_b - m_all) for m_b in ms]
        num = functools.reduce(jnp.add, [w * o_b for w, o_b in zip(ws, os_)])
        den = functools.reduce(jnp.add, [w * l_b for w, l_b in zip(ws, ls)])
        o_ref[0, rows, :] = (num / den).astype(o_ref.dtype)

    blocks = [(p, DSA_PATTERNS[p][1], idx) for p in reversed(range(P)) for idx in range(tq // blk)]
    staged = [block_logits(*blocks[i]) for i in range(DSA_GROUP)]
    for i, (p, _, _) in enumerate(blocks):
        if i + DSA_GROUP < len(blocks):
            staged.append(block_logits(*blocks[i + DSA_GROUP]))
        block_output(p, *staged[i])


def _dilated_attention(ds, bias):
    B, _, S, _ = ds[0].shape
    pairs = DSA_W // V7X_LANES
    tq = DSA_TILE
    assert S % tq == 0 and S >= 2 * tq
    P, blk = len(DSA_PATTERNS), DSA_BLOCK
    bias5 = bias.reshape(P, 2, pairs, 2 * blk, 2 * blk)

    def seq(r, off):
        return pl.BlockSpec((1, r, S // r, V7X_LANES), lambda b, p, t: (b, 0, 0, off + p))

    return pl.pallas_call(
        _dsa_kernel,
        grid=(B, pairs, S // tq),
        in_specs=[seq(r, off) for off in (0, pairs, 2 * pairs) for _, r in DSA_PATTERNS]
        + [pl.BlockSpec((P, 2, 1, 2 * blk, 2 * blk), lambda b, p, t: (0, 0, p, 0, 0))],
        out_specs=pl.BlockSpec((1, tq, V7X_LANES), lambda b, p, t: (b, t, p)),
        out_shape=jax.ShapeDtypeStruct((B, S, DSA_W), BF16),
        scratch_shapes=[pltpu.VMEM((P - 1, tq, V7X_LANES), F32)] * 3,
        compiler_params=_params(("parallel", "parallel", "arbitrary")),
        name="dilated_attention",
    )(*ds, *ds, *ds, bias5)


def _mlstm_kernel(q_ref, kt_ref, v_ref, og_ref, gr_ref, o_ref, c_ref, m_ref):
    h = pl.program_id(1)

    @pl.when(pl.program_id(2) == 0)
    def _():
        c_ref[...] = jnp.zeros_like(c_ref)
        m_ref[...] = jnp.zeros_like(m_ref)

    C = MLSTM_CHUNK
    dv = MLSTM_HEAD_DIM
    L = V7X_LANES
    n_chunks = q_ref.shape[1] // C
    ri = lax.broadcasted_iota(jnp.int32, (C, C), 0)
    ci = lax.broadcasted_iota(jnp.int32, (C, C), 1)
    causal = ci <= ri
    eye = ci == ri
    rep = lambda col: jnp.broadcast_to(col, (C, L))

    def gate_vectors(n):
        sl = slice(n * C, (n + 1) * C)
        gates = gr_ref[:, sl]
        gate_id = lax.broadcasted_iota(jnp.int32, (2 * MLSTM_HEADS, 1), 0)
        li_row = jnp.sum(jnp.where(gate_id == h, gates, 0.0), 0, keepdims=True)
        lf_row = jnp.sum(jnp.where(gate_id == MLSTM_HEADS + h, gates, 0.0), 0, keepdims=True)
        li_col = jnp.sum(jnp.where(eye, li_row, 0.0), -1, keepdims=True)
        a_col = jnp.sum(jnp.where(causal, lf_row, 0.0), -1, keepdims=True)
        g_tot = jnp.sum(lf_row, -1, keepdims=True)
        return li_row, li_col, a_col, g_tot

    def decay_matrix(gv):
        li_row, li_col, a_col, g_tot = gv
        a_rep = rep(a_col)
        a_row = jnp.sum(jnp.where(eye, a_rep, 0.0), 0, keepdims=True)
        w_state = g_tot - a_col + li_col
        m_loc = jnp.max(w_state, 0, keepdims=True)
        ew_rep = rep(jnp.exp(w_state - m_loc))
        dmat = jnp.where(causal, a_rep - a_row + li_row, -jnp.inf)
        d_max = jnp.max(dmat, -1, keepdims=True)
        return a_rep, g_tot, m_loc, ew_rep, dmat, d_max

    def local_matmuls(n, dm, scalars):
        a_rep, _, _, ew_rep, dmat, d_max = dm
        m_s, a_old, a_new = scalars
        ew_rep = ew_rep * a_new
        sl = slice(n * C, (n + 1) * C)
        qb = q_ref[0, sl, :]
        kt = kt_ref[:, sl]
        vb = v_ref[0, sl, :]
        d_rep = rep(d_max)
        p = (jnp.exp(dmat - d_rep) * jnp.dot(qb, kt, preferred_element_type=F32)).astype(BF16)
        pv = jnp.dot(p, jnp.concatenate([vb, jnp.ones((C, L), BF16)], axis=1), preferred_element_type=F32)
        vw = jnp.concatenate([vb.astype(F32) * jnp.concatenate([ew_rep] * (dv // L), axis=1), ew_rep], axis=1)
        kv_ext = jnp.dot(kt, vw.astype(BF16), preferred_element_type=F32)
        return qb, pv, kv_ext, a_rep, d_rep, m_s, a_old

    def recurrent(n, staged):
        qb, pv, kv_ext, a_rep, d_rep, m_s, a_old = staged
        sl = slice(n * C, (n + 1) * C)
        c_s = c_ref[...]
        inter =jnp.dot(qb, c_s.astype(BF16), preferred_element_type=F32)
        inter_log = a_rep + m_s
        m_row = jnp.maximum(inter_log, d_rep)
        f_intra = jnp.exp(d_rep - m_row)
        f_inter = jnp.exp(inter_log - m_row)
        den = f_intra * pv[:, dv:] + f_inter * inter[:, dv:]
        r = 1.0 / jnp.maximum(jnp.abs(den), jnp.exp(-m_row))
        for lo in range(0, dv, L):
            num = f_intra * pv[:, lo : lo + L] + f_inter * inter[:, lo : lo + L]
            o_ref[0, sl, lo : lo + L] = (num * r * og_ref[0, sl, lo : lo + L]).astype(o_ref.dtype)
        c_ref[...] = a_old * c_s + kv_ext

    gvs = [gate_vectors(n) for n in range(n_chunks)]
    dms = [decay_matrix(gv) for gv in gvs]
    m_s = m_ref[...][:, 0:1]
    scalars = []
    for _, g_tot, m_loc, _, _, _ in dms:
        m_new = jnp.maximum(g_tot + m_s, m_loc)
        scalars.append((m_s, jnp.exp(g_tot + m_s - m_new), jnp.exp(m_loc - m_new)))
        m_s = m_new
    m_ref[...] = jnp.broadcast_to(m_s, m_ref.shape)
    _lookahead(n_chunks, lambda n: local_matmuls(n, dms[n], scalars[n]), recurrent)


def _mlstm(q3, kt, v3, o3, gates_row):
    B, S, _ = v3.shape
    H, d = MLSTM_HEADS, MLSTM_HEAD_DIM
    ts = min(SEQ_TILE, S)
    tiles = S // ts
    col = pl.BlockSpec((1, ts, d), lambda b, h, c: (b, c, h))
    return pl.pallas_call(
        _mlstm_kernel,
        grid=(B, H, tiles),
        in_specs=[col, pl.BlockSpec((d, ts), lambda b, h, c: (h, b * tiles + c)), col, col,
                  pl.BlockSpec((2 * H, ts), lambda b, h, c: (0, b * tiles + c))],
        out_specs=pl.BlockSpec((1, ts, d), lambda b, h, c: (b, c, h)),
        out_shape=jax.ShapeDtypeStruct((B, S, H * d), BF16),
        scratch_shapes=[pltpu.VMEM((d, d + V7X_LANES), F32), pltpu.VMEM((1, V7X_LANES), F32)],
        compiler_params=_params(("parallel", "parallel", "arbitrary")),
        name="mlstm",
    )(q3, kt, v3, o3, gates_row)


def _even_mixer(x2d, B, S, w_in, rel_bias):
    tables = _retention_tables(S)
    q, kt, v, gate, d_nat, *d_dil = _even_in_proj(x2d, S, w_in.astype(BF16), tables[0], tables[1])
    y_r = _retention(q.reshape(B, S, -1), kt, v.reshape(B, S, -1), gate.reshape(B, S, -1), tables)
    y_d = _dilated_attention([d_nat.reshape(B, 1, S, -1)] + d_dil, _dsa_bias(rel_bias))
    return [y_r.reshape(B * S, RET_W), y_d.reshape(B * S, DSA_W)]


def _odd_mixer(x2d, B, S, w_in, gate_b, conv_w):
    H = MLSTM_HEADS
    wide = 4 * MLSTM_W
    wg = jnp.zeros((D_MODEL, V7X_LANES), F32).at[:, : 2 * H].set(w_in[:, wide:]).astype(BF16)
    gb = jnp.zeros((1, V7X_LANES), F32).at[0, : 2 * H].set(gate_b)
    q, kt, v, og, gates = _odd_in_proj(x2d, S, w_in.astype(BF16), wg, gb, conv_w)
    y = _mlstm(q.reshape(B, S, -1), kt, v.reshape(B, S, -1), og.reshape(B, S, -1), gates)
    return [y.reshape(B * S, MLSTM_W)]


def kernel(x, even_w_in, even_w_out, rel_bias, odd_w_in, odd_gate_b, odd_conv_w, odd_w_out, ffn_w_up, ffn_conv_w, ffn_conv_b, ffn_w_down, ln_g, ln_b):
    B, S, D = x.shape
    x2d = x.reshape(B * S, D)
    w_up, w_down = ffn_w_up.astype(BF16), ffn_w_down.astype(BF16)
    for layer in range(DEPTH):
        j = layer // 2
        if layer % 2 == 0:
            ys, w_out = _even_mixer(x2d, B, S, even_w_in[j], rel_bias), even_w_out[j]
        else:
            ys, w_out = _odd_mixer(x2d, B, S, odd_w_in[j], odd_gate_b[j], odd_conv_w[j]), odd_w_out[j]
        x2d = _layer_tail(x2d, ys, S, layer, w_out.astype(BF16), (ln_g[layer, 0], ln_b[layer, 0]), w_up,
                          ffn_conv_w[layer], ffn_conv_b[layer], w_down, (ln_g[layer, 1], ln_b[layer, 1]))
    return x2d.reshape(B, S, D)
```

```python
import functools
import math

import jax
import jax.numpy as jnp
import numpy as np
from jax import lax
from jax.experimental import pallas as pl
from jax.experimental.pallas import tpu as pltpu

F32 = jnp.float32
BF16 = jnp.bfloat16

D_MODEL = 1024
DEPTH = 2
RET_HEADS = 4
RET_HEAD_DIM = 128
RET_CHUNK = 128
DSA_HEADS = 8
DSA_HEAD_DIM = 64
DSA_PATTERNS = ((128, 1), (512, 4), (2048, 16))
DSA_BLOCK = 128
T5_BUCKETS = 32
T5_MAX_DIST = 2048
MLSTM_HEADS = 4
MLSTM_HEAD_DIM = 256
MLSTM_CHUNK = 128
MLSTM_CONV = 4
D_FF = 2816
FFN_CONV = 3
LN_EPS = 1e-5
DEEPNORM_ALPHA = (2.0 * DEPTH) ** 0.25
LOG2_E = math.log2(math.e)

RET_W = RET_HEADS * RET_HEAD_DIM
DSA_W = DSA_HEADS * DSA_HEAD_DIM
MLSTM_W = MLSTM_HEADS * MLSTM_HEAD_DIM

V7X_LANES = 128
V7X_SUBLANES = 8
V7X_VMEM_BYTES = 64 * 1024 * 1024
VMEM_LIMIT_BYTES = V7X_VMEM_BYTES - 8 * 1024 * 1024

ROW_TILE = 512
PROJ_CHUNK = 512
FFN_CHUNK = 256
SEQ_TILE = 4096
DSA_TILE = DSA_BLOCK * DSA_PATTERNS[-1][1]
DSA_GROUP = 2

_NT = (((1,), (1,)), ((), ()))
_TN = (((0,), (0,)), ((), ()))


def _params(semantics):
    return pltpu.CompilerParams(dimension_semantics=semantics, vmem_limit_bytes=VMEM_LIMIT_BYTES)


def _resident(shape):
    nd = len(shape)
    return pl.BlockSpec(shape, lambda *_: (0,) * nd, pipeline_mode=pl.Buffered(1))


def _resident_layer(stacked_shape, layer):
    return pl.BlockSpec((None,) + tuple(stacked_shape[1:]), lambda *_: (layer, 0, 0), pipeline_mode=pl.Buffered(1))


def _layer_norm(z, g, b):
    mu = jnp.mean(z, -1, keepdims=True)
    zc = z - mu
    var = jnp.mean(zc * zc, -1, keepdims=True)
    return zc * lax.rsqrt(var + LN_EPS) * g + b


def _silu(z):
    return z * jax.nn.sigmoid(z)


def _log_sigmoid(x):
    return jnp.minimum(x, 0.0) - jnp.log1p(jnp.exp(-jnp.abs(x)))


def _lookahead(n, produce, consume, depth=1):
    staged = [produce(j) for j in range(min(depth, n))]
    for j in range(n):
        if j + depth < n:
            staged.append(produce(j + depth))
        consume(j, staged[j])
        staged[j] = None


def _shift_rows(x, prev, s):
    if s == 0:
        return x
    rolled = pltpu.roll(x, s, 0)
    head = pltpu.roll(prev, s, 0)
    rid = lax.broadcasted_iota(jnp.int32, (V7X_SUBLANES, x.shape[1]), 0)
    first = jnp.where(rid < s, head, rolled[:V7X_SUBLANES])
    return jnp.concatenate([first, rolled[V7X_SUBLANES:]], axis=0)


def _causal_dwconv(x, prev, w_ref, taps):
    acc = None
    for k in range(taps):
        term = _shift_rows(x, prev, taps - 1 - k) * w_ref[k : k + 1, :]
        acc = term if acc is None else acc + term
    return acc


def _even_in_proj_kernel(x_ref, w_ref, cos_ref, sin_ref, q_ref, kt_ref, v_ref, g_ref, *rest):
    d_refs, stage_ref = rest[:-1], rest[-1]
    xb = x_ref[...].astype(BF16)
    tm = xb.shape[0]
    cos = cos_ref[...]
    sin = sin_ref[...]
    tn, d = RET_W, RET_HEAD_DIM

    def consume(j, h):
        if j < 2:
            for hh in range(RET_HEADS):
                c = h[:, hh * d : (hh + 1) * d]
                rot = c * cos + pltpu.roll(c, d // 2, 1) * sin
                if j == 0:
                    q_ref[:, hh * d : (hh + 1) * d] = rot.astype(BF16)
                else:
                    kt_ref[hh * d : (hh + 1) * d, :] = rot.T.astype(BF16)
        elif j == 2:
            v_ref[...] = h.astype(BF16)
        elif j == 3:
            g_ref[...] = _silu(h).astype(BF16)
        else:
            if j == 4:
                h = h * (DSA_HEAD_DIM ** -0.5 * LOG2_E)
            lo = (j - 4) * tn
            d_refs[0][:, lo : lo + tn] = h.astype(BF16)
            src, dst = stage_ref.at[0], stage_ref.at[1]
            for s in range(tn // V7X_LANES):
                src[s] = h[:, s * V7X_LANES : (s + 1) * V7X_LANES]
            prev_r = 1
            for d_ref, (_, r) in zip(d_refs[1:], DSA_PATTERNS[1:]):
                step = r // prev_r
                n_rows = tm // r
                for rho_prev in range(prev_r):
                    for a in range(step):
                        rho = rho_prev + prev_r * a
                        for s in range(tn // V7X_LANES):
                            rows = src.at[s][pl.ds(rho_prev * (tm // prev_r) + a, n_rows, stride=step), :]
                            if r != DSA_PATTERNS[-1][1]:
                                dst[s, rho * n_rows : (rho + 1) * n_rows, :] = rows
                            d_ref[0, rho, :, lo + s * V7X_LANES : lo + (s + 1) * V7X_LANES] = rows.astype(BF16)
                src, dst, prev_r = dst, src, r

    _lookahead(w_ref.shape[1] // tn, lambda j: jnp.dot(xb, w_ref[:, j * tn : (j + 1) * tn], preferred_element_type=F32), consume)


def _even_in_proj(x2d, seq_len, w, cos_t, sin_t):
    T, K = x2d.shape
    tm = min(2 * ROW_TILE, seq_len)
    tps = seq_len // tm
    B = T // seq_len
    row = lambda i: (i, 0)
    pos = pl.BlockSpec((tm, RET_HEAD_DIM), lambda i: (i % tps, 0))
    shapes = [(T, RET_W), (RET_W, T), (T, RET_W), (T, RET_W), (T, 3 * DSA_W)]
    dtypes = [BF16, BF16, BF16, BF16, BF16]
    specs = [pl.BlockSpec((tm, RET_W), row), pl.BlockSpec((RET_W, tm), lambda i: (0, i)),
             pl.BlockSpec((tm, RET_W), row), pl.BlockSpec((tm, RET_W), row), pl.BlockSpec((tm, 3 * DSA_W), row)]
    for _, r in DSA_PATTERNS[1:]:
        shapes.append((B, r, seq_len // r, 3 * DSA_W))
        dtypes.append(BF16)
        specs.append(pl.BlockSpec((1, r, tm // r, 3 * DSA_W), lambda i: (i // tps, 0, i % tps, 0)))
    return pl.pallas_call(
        _even_in_proj_kernel,
        grid=(T // tm,),
        in_specs=[pl.BlockSpec((tm, K), row), _resident(w.shape), pos, pos],
        out_specs=specs,
        out_shape=[jax.ShapeDtypeStruct(s, dt) for s, dt in zip(shapes, dtypes)],
        scratch_shapes=[pltpu.VMEM((2, RET_W // V7X_LANES, tm, V7X_LANES), F32)],
        compiler_params=_params(("parallel",)),
        name="even_in_proj",
    )(x2d, w, cos_t, sin_t)


def _odd_in_proj_kernel(x_ref, w_ref, wg_ref, gb_ref, cw_ref, q_ref, kt_ref, v_ref, o_ref, gt_ref, carry_ref, *, tiles_per_seq):
    tn = PROJ_CHUNK
    per = MLSTM_W // tn

    @pl.when(pl.program_id(0) % tiles_per_seq == 0)
    def _():
        carry_ref[...] = jnp.zeros_like(carry_ref)

    xb = x_ref[...].astype(BF16)
    tm = xb.shape[0]

    order = [c for pair in zip(range(2 * per), range(2 * per, 4 * per)) for c in pair]

    def produce(i):
        j = order[i]
        return jnp.dot(xb, w_ref[:, j * tn : (j + 1) * tn], preferred_element_type=F32)

    def consume(i, h):
        j = order[i]
        sl = slice(j * tn, (j + 1) * tn)
        if j < 2 * per:
            prev = carry_ref[:, sl]
            carry_ref[:, sl] = h[tm - V7X_SUBLANES :, :]
            act = _silu(_causal_dwconv(h, prev, cw_ref.at[:, sl], MLSTM_CONV))
            if j < per:
                q_ref[:, sl] = act.astype(BF16)
            else:
                kt_ref[(j - per) * tn : (j - per + 1) * tn, :] = (act * (MLSTM_HEAD_DIM ** -0.5)).T.astype(BF16)
        elif j < 3 * per:
            v_ref[:, (j - 2 * per) * tn : (j - 2 * per + 1) * tn] = h.astype(BF16)
        else:
            o_ref[:, (j - 3 * per) * tn : (j - 3 * per + 1) * tn] = jax.nn.sigmoid(h).astype(BF16)

    _lookahead(4 * per, produce, consume)
    gates = (jnp.dot(xb, wg_ref[...], preferred_element_type=F32) + gb_ref[...]).T[: 2 * MLSTM_HEADS, :]
    gate_id = lax.broadcasted_iota(jnp.int32, (2 * MLSTM_HEADS, 1), 0)
    gt_ref[...] = jnp.where(gate_id < MLSTM_HEADS, gates, _log_sigmoid(gates))


def _odd_in_proj(x2d, seq_len, w, wg, gb, conv_w):
    T, K = x2d.shape
    tm = min(2 * ROW_TILE, seq_len)
    W = MLSTM_W
    cw = jnp.zeros((V7X_SUBLANES, 2 * W), F32).at[:MLSTM_CONV].set(conv_w)
    row = lambda i: (i, 0)
    shapes = ((T, W), (W, T), (T, W), (T, W), (2 * MLSTM_HEADS, T))
    dtypes = (BF16, BF16, BF16, BF16, F32)
    return pl.pallas_call(
        functools.partial(_odd_in_proj_kernel, tiles_per_seq=seq_len // tm),
        grid=(T // tm,),
        in_specs=[pl.BlockSpec((tm, K), row), _resident(w.shape), _resident(wg.shape), _resident(gb.shape),
                  _resident(cw.shape)],
        out_specs=[pl.BlockSpec((tm, W), row), pl.BlockSpec((W, tm), lambda i: (0, i)), pl.BlockSpec((tm, W), row),
                   pl.BlockSpec((tm, W), row), pl.BlockSpec((2 * MLSTM_HEADS, tm), lambda i: (0, i))],
        out_shape=[jax.ShapeDtypeStruct(s, dt) for s, dt in zip(shapes, dtypes)],
        scratch_shapes=[pltpu.VMEM((V7X_SUBLANES, 2 * MLSTM_W), F32)],
        compiler_params=_params(("arbitrary",)),
        name="odd_in_proj",
    )(x2d, w, wg, gb, cw)


def _tail_kernel(*refs, n_y, tiles_per_seq):
    x_ref = refs[0]
    y_refs = refs[1 : 1 + n_y]
    (wout_ref, g1_ref, b1_ref, wup_ref, cw_ref, cb_ref, wdn_ref, g_ref, b_ref,
     o_ref, carry_ref, act_ref, x1_ref) = refs[1 + n_y :]

    @pl.when(pl.program_id(0) % tiles_per_seq == 0)
    def _():
        carry_ref[...] = jnp.zeros_like(carry_ref)

    tm = x_ref.shape[0]
    for lo in range(0, tm, tm // 2):
        rows = slice(lo, lo + tm // 2)
        acc = None
        row = 0
        for y_ref in y_refs:
            kw = y_ref.shape[1]
            part = jnp.dot(y_ref[rows, :], wout_ref[row : row + kw, :], preferred_element_type=F32)
            acc = part if acc is None else acc + part
            row += kw
        x1_ref[rows, :] = _layer_norm(DEEPNORM_ALPHA * x_ref[rows, :] + acc, g1_ref[...], b1_ref[...])

    xb = x1_ref[...].astype(BF16)

    def up_dots(j):
        lo = j * FFN_CHUNK
        return (jnp.dot(xb, wup_ref[:, lo : lo + FFN_CHUNK], preferred_element_type=F32),
                jnp.dot(xb, wup_ref[:, D_FF + lo : D_FF + lo + FFN_CHUNK], preferred_element_type=F32))

    def activate(j, gate_up):
        gate, up = gate_up
        sl = slice(j * FFN_CHUNK, (j + 1) * FFN_CHUNK)
        prev = carry_ref[:, sl]
        carry_ref[:, sl] = gate[tm - V7X_SUBLANES :, :]
        conv = _causal_dwconv(gate, prev, cw_ref.at[:, sl], FFN_CONV) + cb_ref[:, sl]
        act_ref[:, sl] = (_silu(conv) * up).astype(BF16)

    _lookahead(D_FF // FFN_CHUNK, up_dots, activate)
    for lo in range(0, tm, tm // 2):
        rows = slice(lo, lo + tm // 2)
        ffn = jnp.dot(act_ref[rows, :], wdn_ref[...], preferred_element_type=F32)
        z = DEEPNORM_ALPHA * x1_ref[rows, :] + ffn
        o_ref[rows, :] = _layer_norm(z, g_ref[...], b_ref[...])


def _layer_tail(x2d, ys, seq_len, layer, w_out, ln1, w_up, conv_w, conv_b, w_down, ln2):
    T, D = x2d.shape
    tm = min(2 * ROW_TILE, seq_len)
    cw = jnp.zeros((V7X_SUBLANES, D_FF), F32).at[:FFN_CONV].set(conv_w)
    row = lambda i: (i, 0)
    vec = lambda v: v.reshape(1, -1)
    return pl.pallas_call(
        functools.partial(_tail_kernel, n_y=len(ys), tiles_per_seq=seq_len // tm),
        grid=(T // tm,),
        in_specs=[pl.BlockSpec((tm, D), row)]
        + [pl.BlockSpec((tm, y.shape[1]), row) for y in ys]
        + [_resident(w_out.shape), _resident((1, D)), _resident((1, D)),
           _resident_layer(w_up.shape, layer), _resident(cw.shape), _resident((1, D_FF)),
           _resident_layer(w_down.shape, layer), _resident((1, D)), _resident((1, D))],
        out_specs=pl.BlockSpec((tm, D), row),
        out_shape=jax.ShapeDtypeStruct((T, D), F32),
        scratch_shapes=[pltpu.VMEM((V7X_SUBLANES, D_FF), F32), pltpu.VMEM((tm, D_FF), BF16), pltpu.VMEM((tm, D), F32)],
        compiler_params=_params(("arbitrary",)),
        name="layer_tail",
    )(x2d, *ys, w_out, vec(ln1[0]), vec(ln1[1]), w_up, cw, vec(conv_b), w_down, vec(ln2[0]), vec(ln2[1]))


def _retention_tables(seq_len):
    H, C, d = RET_HEADS, RET_CHUNK, RET_HEAD_DIM
    scale = d ** -0.5
    log_gamma = jnp.log1p(-jnp.exp2(-5.0 - jnp.arange(H, dtype=F32)))
    idx = jnp.arange(C, dtype=F32)
    rel = idx[:, None] - idx[None, :]
    decay = jnp.where(rel >= 0, jnp.exp(log_gamma[:, None, None] * jnp.maximum(rel, 0.0)), 0.0) * scale
    k_w = jnp.exp(log_gamma[:, None] * (C - 1 - idx)[None, :]) * scale
    q_w = jnp.exp(log_gamma[:, None] * (idx + 1.0)[None, :])
    kw_t = jnp.broadcast_to(k_w[:, :, None], (H, C, d))
    qw_t = jnp.broadcast_to(q_w[:, :, None], (H, C, d))
    cd_t = jnp.broadcast_to(jnp.exp(log_gamma * C)[:, None, None], (H, 1, d))
    inv = np.float32(1.0) / np.float32(10000.0) ** (np.arange(0, d, 2, dtype=np.float32) / np.float32(d))
    ang = (np.arange(seq_len, dtype=np.float32)[:, None] * inv[None, :]).astype(np.float64)
    cos, sin = np.cos(ang).astype(np.float32), np.sin(ang).astype(np.float32)
    cos_t = jnp.asarray(np.concatenate([cos, cos], -1), F32)
    sin_t = jnp.asarray(np.concatenate([-sin, sin], -1), F32)
    return cos_t, sin_t, decay, kw_t, qw_t, cd_t


def _retention_kernel(q_ref, kt_ref, v_ref, g_ref, dec_ref, kw_ref, qw_ref, cd_ref, o_ref, r_ref):
    @pl.when(pl.program_id(2) == 0)
    def _():
        r_ref[...] = jnp.zeros_like(r_ref)

    C = RET_CHUNK
    decay = dec_ref[0]
    kw = kw_ref[0]
    qw = qw_ref[0]
    cd = cd_ref[0]

    def local(n):
        sl = slice(n * C, (n + 1) * C)
        qb = q_ref[0, sl, :]
        kt = kt_ref[:, sl]
        vb = v_ref[0, sl, :]
        scores = jnp.dot(qb, kt, preferred_element_type=F32) * decay
        y_intra = jnp.dot(scores.astype(BF16), vb, preferred_element_type=F32)
        kv = jnp.dot(kt, (vb.astype(F32) * kw).astype(BF16), preferred_element_type=F32)
        q_dec = (qb.astype(F32) * qw).astype(BF16)
        return y_intra, kv, q_dec

    def recurrent(n, staged):
        y_intra, kv, q_dec = staged
        sl = slice(n * C, (n + 1) * C)
        r = r_ref[...]
        y = y_intra + jnp.dot(q_dec, r.astype(BF16), preferred_element_type=F32)
        r_ref[...] = cd * r + kv
        mu = jnp.mean(y, -1, keepdims=True)
        yc = y - mu
        var = jnp.mean(yc * yc, -1, keepdims=True)
        o_ref[0, sl, :] = (yc * lax.rsqrt(var + LN_EPS) * g_ref[0, sl, :]).astype(o_ref.dtype)

    _lookahead(q_ref.shape[1] // C, local, recurrent)


def _retention(q3, kt, v3, g3, tables):
    B, S, _ = v3.shape
    H, d, C = RET_HEADS, RET_HEAD_DIM, RET_CHUNK
    ts = min(SEQ_TILE, S)
    tiles = S // ts
    _, _, decay, kw_t, qw_t, cd_t = tables
    col = pl.BlockSpec((1, ts, d), lambda b, h, c: (b, c, h))
    per_head = lambda shape: pl.BlockSpec((1,) + shape, lambda b, h, c: (h, 0, 0))
    return pl.pallas_call(
        _retention_kernel,
        grid=(B, H, tiles),
        in_specs=[col, pl.BlockSpec((d, ts), lambda b, h, c: (h, b * tiles + c)), col, col,
                  per_head((C, C)), per_head((C, d)), per_head((C, d)), per_head((1, d))],
        out_specs=pl.BlockSpec((1, ts, d), lambda b, h, c: (b, c, h)),
        out_shape=jax.ShapeDtypeStruct((B, S, H * d), BF16),
        scratch_shapes=[pltpu.VMEM((d, d), F32)],
        compiler_params=_params(("parallel", "parallel", "arbitrary")),
        name="retention",
    )(q3, kt, v3, g3, decay, kw_t, qw_t, cd_t)


def _t5_bucket(dist):
    exact = T5_BUCKETS // 2
    n = jnp.maximum(dist, 0)
    large = exact + (jnp.log(jnp.maximum(n, 1).astype(F32) / exact) / math.log(T5_MAX_DIST / exact) * (T5_BUCKETS - exact)).astype(jnp.int32)
    large = jnp.minimum(large, T5_BUCKETS - 1)
    return jnp.where(n < exact, n, large)


def _dsa_bucket_tables():
    blk = DSA_BLOCK
    qi = jnp.arange(blk)[:, None]
    kj = jnp.arange(2 * blk)[None, :]
    tabs = []
    for window, dilation in DSA_PATTERNS:
        assert window // dilation <= blk
        dist = qi + blk - kj
        valid = (dist >= 0) & (dist <= window // dilation)
        tabs.append(jnp.where(valid, _t5_bucket(dist * dilation), -1))
    return jnp.stack(tabs).astype(jnp.int32)


def _dsa_bias_kernel(rb_ref, bk_ref, o_ref):
    blk = DSA_BLOCK
    rows = 4 * V7X_SUBLANES
    masked = jnp.full((rows, blk), -jnp.inf, F32)
    for lo in range(0, blk, rows):
        bk = bk_ref[0, lo : lo + rows, :]
        accs = [jnp.full(bk.shape, -jnp.inf, F32) for _ in range(DSA_HEADS)]
        for b in range(T5_BUCKETS):
            hit = bk == b
            accs = [jnp.where(hit, rb_ref[b, h] * LOG2_E, acc) for h, acc in enumerate(accs)]
        for h, acc in enumerate(accs):
            o_ref[0, 0, h, lo : lo + rows, :] = acc
            o_ref[0, 1, h, lo : lo + rows, :] = jnp.concatenate([acc[:, blk:], masked], axis=1)


def _dsa_bias(rel_bias):
    buckets = _dsa_bucket_tables()
    P, blk = len(DSA_PATTERNS), DSA_BLOCK
    return pl.pallas_call(
        _dsa_bias_kernel,
        grid=(P,),
        in_specs=[pl.BlockSpec(memory_space=pltpu.SMEM), pl.BlockSpec((1, blk, 2 * blk), lambda p: (p, 0, 0))],
        out_specs=pl.BlockSpec((1, 2, DSA_HEADS, blk, 2 * blk), lambda p: (p, 0, 0, 0, 0)),
        out_shape=jax.ShapeDtypeStruct((P, 2, DSA_HEADS, blk, 2 * blk), F32),
        compiler_params=_params(("parallel",)),
        name="dsa_bias",
    )(rel_bias, buckets)


def _dsa_kernel(*refs):
    P = len(DSA_PATTERNS)
    q_refs, k_refs, v_refs = refs[:P], refs[P : 2 * P], refs[2 * P : 3 * P]
    bias_ref, o_ref, acc_o, acc_m, acc_l = refs[3 * P :]
    blk = DSA_BLOCK
    tq = o_ref.shape[1]
    t = pl.program_id(2)
    lane = lax.broadcasted_iota(jnp.int32, (1, V7X_LANES), 1)
    head0 = lane < DSA_HEAD_DIM

    def block_logits(p, r, idx):
        blocks_per_residue = tq // (blk * r)
        rho = idx // blocks_per_residue
        nb = idx % blocks_per_residue
        l0 = pl.multiple_of(t * (tq // r) + nb * blk, blk)
        first = l0 == 0
        k_row = pl.multiple_of(jnp.where(first, l0, l0 - blk), blk)
        qb = q_refs[p][0, rho, pl.ds(l0, blk), :]
        zero = jnp.zeros_like(qb)
        q2 = jnp.concatenate([jnp.where(head0, qb, zero), jnp.where(head0, zero, qb)], axis=0)
        bias = bias_ref[p, pl.ds(first.astype(jnp.int32), 1), 0][0]
        kb = k_refs[p][0, rho, pl.ds(k_row, 2 * blk), :]
        logits = lax.dot_general(q2, kb, _NT, preferred_element_type=F32) + bias
        start = rho + nb * (blk * r)
        rows = pl.ds(start, blk, stride=r) if r > 1 else pl.ds(start, blk)
        return logits, (rho, k_row), rows

    def block_output(p, logits, key_rows, rows):
        rho, k_row = key_rows
        m = jnp.max(logits, -1, keepdims=True)
        e = jnp.exp2(logits - m).astype(BF16)
        vb = v_refs[p][0, rho, pl.ds(k_row, 2 * blk), :]
        o2 = jnp.dot(e, jnp.concatenate([vb, jnp.ones_like(vb)], axis=1), preferred_element_type=F32)
        o_p = jnp.where(head0, o2[:blk, :V7X_LANES], o2[blk:, :V7X_LANES])
        m_p = jnp.where(head0, m[:blk], m[blk:])
        l_p = jnp.where(head0, o2[:blk, V7X_LANES:], o2[blk:, V7X_LANES:])
        if p > 0:
            acc_o.at[p - 1][rows, :] = o_p
            acc_m.at[p - 1][rows, :] = m_p
            acc_l.at[p - 1][rows, :] = l_p
            return
        os_ = [o_p] + [acc_o[b, rows, :] for b in range(P - 1)]
        ms = [m_p] + [acc_m[b, rows, :] for b in range(P - 1)]
        ls = [l_p] + [acc_l[b, rows, :] for b in range(P - 1)]
        m_all = functools.reduce(jnp.maximum, ms)
        ws = [jnp.exp2(m_b - m_all) for m_b in ms]
        num = functools.reduce(jnp.add, [w * o_b for w, o_b in zip(ws, os_)])
        den = functools.reduce(jnp.add, [w * l_b for w, l_b in zip(ws, ls)])
        o_ref[0, rows, :] = (num / den).astype(o_ref.dtype)

    blocks = [(p, DSA_PATTERNS[p][1], idx) for p in reversed(range(P)) for idx in range(tq // blk)]
    staged = [block_logits(*blocks[i]) for i in range(DSA_GROUP)]
    for i, (p, _, _) in enumerate(blocks):
        if i + DSA_GROUP < len(blocks):
            staged.append(block_logits(*blocks[i + DSA_GROUP]))
        block_output(p, *staged[i])


def _dilated_attention(ds, bias):
    B, _, S, _ = ds[0].shape
    pairs = DSA_W // V7X_LANES
    tq = DSA_TILE
    assert S % tq == 0 and S >= 2 * tq
    P, blk = len(DSA_PATTERNS), DSA_BLOCK
    bias5 = bias.reshape(P, 2, pairs, 2 * blk, 2 * blk)

    def seq(r, off):
        return pl.BlockSpec((1, r, S // r, V7X_LANES), lambda b, p, t: (b, 0, 0, off + p))

    return pl.pallas_call(
        _dsa_kernel,
        grid=(B, pairs, S // tq),
        in_specs=[seq(r, off) for off in (0, pairs, 2 * pairs) for _, r in DSA_PATTERNS]
        + [pl.BlockSpec((P, 2, 1, 2 * blk, 2 * blk), lambda b, p, t: (0, 0, p, 0, 0))],
        out_specs=pl.BlockSpec((1, tq, V7X_LANES), lambda b, p, t: (b, t, p)),
        out_shape=jax.ShapeDtypeStruct((B, S, DSA_W), BF16),
        scratch_shapes=[pltpu.VMEM((P - 1, tq, V7X_LANES), F32)] * 3,
        compiler_params=_params(("parallel", "parallel", "arbitrary")),
        name="dilated_attention",
    )(*ds, *ds, *ds, bias5)


def _mlstm_kernel(q_ref, kt_ref, v_ref, og_ref, gr_ref, o_ref, c_ref, m_ref):
    h = pl.program_id(1)

    @pl.when(pl.program_id(2) == 0)
    def _():
        c_ref[...] = jnp.zeros_like(c_ref)
        m_ref[...] = jnp.zeros_like(m_ref)

    C = MLSTM_CHUNK
    dv = MLSTM_HEAD_DIM
    L = V7X_LANES
    n_chunks = q_ref.shape[1] // C
    ri = lax.broadcasted_iota(jnp.int32, (C, C), 0)
    ci = lax.broadcasted_iota(jnp.int32, (C, C), 1)
    causal = ci <= ri
    eye = ci == ri
    rep = lambda col: jnp.broadcast_to(col, (C, L))

    def gate_vectors(n):
        sl = slice(n * C, (n + 1) * C)
        gates = gr_ref[:, sl]
        gate_id = lax.broadcasted_iota(jnp.int32, (2 * MLSTM_HEADS, 1), 0)
        li_row = jnp.sum(jnp.where(gate_id == h, gates, 0.0), 0, keepdims=True)
        lf_row = jnp.sum(jnp.where(gate_id == MLSTM_HEADS + h, gates, 0.0), 0, keepdims=True)
        li_col = jnp.sum(jnp.where(eye, li_row, 0.0), -1, keepdims=True)
        a_col = jnp.sum(jnp.where(causal, lf_row, 0.0), -1, keepdims=True)
        g_tot = jnp.sum(lf_row, -1, keepdims=True)
        return li_row, li_col, a_col, g_tot

    def decay_matrix(gv):
        li_row, li_col, a_col, g_tot = gv
        a_rep = rep(a_col)
        a_row = jnp.sum(jnp.where(eye, a_rep, 0.0), 0, keepdims=True)
        w_state = g_tot - a_col + li_col
        m_loc = jnp.max(w_state, 0, keepdims=True)
        ew_rep = rep(jnp.exp(w_state - m_loc))
        dmat = jnp.where(causal, a_rep - a_row + li_row, -jnp.inf)
        d_max = jnp.max(dmat, -1, keepdims=True)
        return a_rep, g_tot, m_loc, ew_rep, dmat, d_max

    def local_matmuls(n, dm, scalars):
        a_rep, _, _, ew_rep, dmat, d_max = dm
        m_s, a_old, a_new = scalars
        ew_rep = ew_rep * a_new
        sl = slice(n * C, (n + 1) * C)
        qb = q_ref[0, sl, :]
        kt = kt_ref[:, sl]
        vb = v_ref[0, sl, :]
        d_rep = rep(d_max)
        p = (jnp.exp(dmat - d_rep) * jnp.dot(qb, kt, preferred_element_type=F32)).astype(BF16)
        pv = jnp.dot(p, jnp.concatenate([vb, jnp.ones((C, L), BF16)], axis=1), preferred_element_type=F32)
        vw = jnp.concatenate([vb.astype(F32) * jnp.concatenate([ew_rep] * (dv // L), axis=1), ew_rep], axis=1)
        kv_ext = jnp.dot(kt, vw.astype(BF16), preferred_element_type=F32)
        return qb, pv, kv_ext, a_rep, d_rep, m_s, a_old

    def recurrent(n, staged):
        qb, pv, kv_ext, a_rep, d_rep, m_s, a_old = staged
        sl = slice(n * C, (n + 1) * C)
        c_s = c_ref[...]
        inter =jnp.dot(qb, c_s.astype(BF16), preferred_element_type=F32)
        inter_log = a_rep + m_s
        m_row = jnp.maximum(inter_log, d_rep)
        f_intra = jnp.exp(d_rep - m_row)
        f_inter = jnp.exp(inter_log - m_row)
        den = f_intra * pv[:, dv:] + f_inter * inter[:, dv:]
        r = 1.0 / jnp.maximum(jnp.abs(den), jnp.exp(-m_row))
        for lo in range(0, dv, L):
            num = f_intra * pv[:, lo : lo + L] + f_inter * inter[:, lo : lo + L]
            o_ref[0, sl, lo : lo + L] = (num * r * og_ref[0, sl, lo : lo + L]).astype(o_ref.dtype)
        c_ref[...] = a_old * c_s + kv_ext

    gvs = [gate_vectors(n) for n in range(n_chunks)]
    dms = [decay_matrix(gv) for gv in gvs]
    m_s = m_ref[...][:, 0:1]
    scalars = []
    for _, g_tot, m_loc, _, _, _ in dms:
        m_new = jnp.maximum(g_tot + m_s, m_loc)
        scalars.append((m_s, jnp.exp(g_tot + m_s - m_new), jnp.exp(m_loc - m_new)))
        m_s = m_new
    m_ref[...] = jnp.broadcast_to(m_s, m_ref.shape)
    _lookahead(n_chunks, lambda n: local_matmuls(n, dms[n], scalars[n]), recurrent)


def _mlstm(q3, kt, v3, o3, gates_row):
    B, S, _ = v3.shape
    H, d = MLSTM_HEADS, MLSTM_HEAD_DIM
    ts = min(SEQ_TILE, S)
    tiles = S // ts
    col = pl.BlockSpec((1, ts, d), lambda b, h, c: (b, c, h))
    return pl.pallas_call(
        _mlstm_kernel,
        grid=(B, H, tiles),
        in_specs=[col, pl.BlockSpec((d, ts), lambda b, h, c: (h, b * tiles + c)), col, col,
                  pl.BlockSpec((2 * H, ts), lambda b, h, c: (0, b * tiles + c))],
        out_specs=pl.BlockSpec((1, ts, d), lambda b, h, c: (b, c, h)),
        out_shape=jax.ShapeDtypeStruct((B, S, H * d), BF16),
        scratch_shapes=[pltpu.VMEM((d, d + V7X_LANES), F32), pltpu.VMEM((1, V7X_LANES), F32)],
        compiler_params=_params(("parallel", "parallel", "arbitrary")),
        name="mlstm",
    )(q3, kt, v3, o3, gates_row)


def _even_mixer(x2d, B, S, w_in, rel_bias):
    tables = _retention_tables(S)
    q, kt, v, gate, d_nat, *d_dil = _even_in_proj(x2d, S, w_in.astype(BF16), tables[0], tables[1])
    y_r = _retention(q.reshape(B, S, -1), kt, v.reshape(B, S, -1), gate.reshape(B, S, -1), tables)
    y_d = _dilated_attention([d_nat.reshape(B, 1, S, -1)] + d_dil, _dsa_bias(rel_bias))
    return [y_r.reshape(B * S, RET_W), y_d.reshape(B * S, DSA_W)]


def _odd_mixer(x2d, B, S, w_in, gate_b, conv_w):
    H = MLSTM_HEADS
    wide = 4 * MLSTM_W
    wg = jnp.zeros((D_MODEL, V7X_LANES), F32).at[:, : 2 * H].set(w_in[:, wide:]).astype(BF16)
    gb = jnp.zeros((1, V7X_LANES), F32).at[0, : 2 * H].set(gate_b)
    q, kt, v, og, gates = _odd_in_proj(x2d, S, w_in.astype(BF16), wg, gb, conv_w)
    y = _mlstm(q.reshape(B, S, -1), kt, v.reshape(B, S, -1), og.reshape(B, S, -1), gates)
    return [y.reshape(B * S, MLSTM_W)]


def kernel(x, even_w_in, even_w_out, rel_bias, odd_w_in, odd_gate_b, odd_conv_w, odd_w_out, ffn_w_up, ffn_conv_w, ffn_conv_b, ffn_w_down, ln_g, ln_b):
    B, S, D = x.shape
    x2d = x.reshape(B * S, D)
    w_up, w_down = ffn_w_up.astype(BF16), ffn_w_down.astype(BF16)
    for layer in range(DEPTH):
        j = layer // 2
        if layer % 2 == 0:
            ys, w_out = _even_mixer(x2d, B, S, even_w_in[j], rel_bias), even_w_out[j]
        else:
            ys, w_out = _odd_mixer(x2d, B, S, odd_w_in[j], odd_gate_b[j], odd_conv_w[j]), odd_w_out[j]
        x2d = _layer_tail(x2d, ys, S, layer, w_out.astype(BF16), (ln_g[layer, 0], ln_b[layer, 0]), w_up,
                          ffn_conv_w[layer], ffn_conv_b[layer], w_down, (ln_g[layer, 1], ln_b[layer, 1]))
    return x2d.reshape(B, S, D)
```

```python
import functools
import math

import jax
import jax.numpy as jnp
import numpy as np
from jax import lax
from jax.experimental import pallas as pl
from jax.experimental.pallas import tpu as pltpu

F32 = jnp.float32
BF16 = jnp.bfloat16

D_MODEL = 1024
DEPTH = 2
RET_HEADS = 4
RET_HEAD_DIM = 128
RET_CHUNK = 128
DSA_HEADS = 8
DSA_HEAD_DIM = 64
DSA_PATTERNS = ((128, 1), (512, 4), (2048, 16))
DSA_BLOCK = 128
T5_BUCKETS = 32
T5_MAX_DIST = 2048
MLSTM_HEADS = 4
MLSTM_HEAD_DIM = 256
MLSTM_CHUNK = 128
MLSTM_CONV = 4
D_FF = 2816
FFN_CONV = 3
LN_EPS = 1e-5
DEEPNORM_ALPHA = (2.0 * DEPTH) ** 0.25
LOG2_E = math.log2(math.e)

RET_W = RET_HEADS * RET_HEAD_DIM
DSA_W = DSA_HEADS * DSA_HEAD_DIM
MLSTM_W = MLSTM_HEADS * MLSTM_HEAD_DIM

V7X_LANES = 128
V7X_SUBLANES = 8
V7X_VMEM_BYTES = 64 * 1024 * 1024
VMEM_LIMIT_BYTES = V7X_VMEM_BYTES - 8 * 1024 * 1024

ROW_TILE = 512
PROJ_ROW_TILE = 1024
PROJ_CHUNK = 512
FFN_CHUNK = 256
SEQ_TILE = 4096
DSA_TILE = DSA_BLOCK * DSA_PATTERNS[-1][1]

_NT = (((1,), (1,)), ((), ()))


def _params(semantics):
    return pltpu.CompilerParams(dimension_semantics=semantics, vmem_limit_bytes=VMEM_LIMIT_BYTES)


def _resident(shape):
    nd = len(shape)
    return pl.BlockSpec(shape, lambda *_: (0,) * nd, pipeline_mode=pl.Buffered(1))


def _resident_layer(stacked_shape, layer):
    return pl.BlockSpec((None,) + tuple(stacked_shape[1:]), lambda *_: (layer, 0, 0), pipeline_mode=pl.Buffered(1))


def _layer_norm(z, g, b):
    mu = jnp.mean(z, -1, keepdims=True)
    zc = z - mu
    var = jnp.mean(zc * zc, -1, keepdims=True)
    return zc * lax.rsqrt(var + LN_EPS) * g + b


def _silu(z):
    return z * jax.nn.sigmoid(z)


def _log_sigmoid(x):
    return jnp.minimum(x, 0.0) - jnp.log1p(jnp.exp(-jnp.abs(x)))


def _lookahead(n, produce, consume):
    nxt = produce(0)
    for j in range(n):
        cur = nxt
        if j + 1 < n:
            nxt = produce(j + 1)
        consume(j, cur)


def _shift_rows(x, prev, s):
    if s == 0:
        return x
    rolled = pltpu.roll(x, s, 0)
    head = pltpu.roll(prev, s, 0)
    rid = lax.broadcasted_iota(jnp.int32, (V7X_SUBLANES, x.shape[1]), 0)
    first = jnp.where(rid < s, head, rolled[:V7X_SUBLANES])
    return jnp.concatenate([first, rolled[V7X_SUBLANES:]], axis=0)


def _causal_dwconv(x, prev, w_ref, taps):
    acc = None
    for k in range(taps):
        term = _shift_rows(x, prev, taps - 1 - k) * w_ref[k : k + 1, :]
        acc = term if acc is None else acc + term
    return acc


def _even_in_proj_kernel(x_ref, w_ref, cos_ref, sin_ref, q_ref, kt_ref, v_ref, g_ref, *rest):
    d_refs, stage_ref = rest[:-1], rest[-1]
    xb = x_ref[...].astype(BF16)
    tm = xb.shape[0]
    cos = cos_ref[...]
    sin = sin_ref[...]
    tn, d = RET_W, RET_HEAD_DIM

    def consume(j, h):
        if j < 2:
            for hh in range(RET_HEADS):
                c = h[:, hh * d : (hh + 1) * d]
                rot = c * cos + pltpu.roll(c, d // 2, 1) * sin
                if j == 0:
                    q_ref[:, hh * d : (hh + 1) * d] = rot.astype(BF16)
                else:
                    kt_ref[hh * d : (hh + 1) * d, :] = rot.T.astype(BF16)
        elif j == 2:
            v_ref[...] = h.astype(BF16)
        elif j == 3:
            g_ref[...] = _silu(h).astype(BF16)
        else:
            if j == 4:
                h = h * (DSA_HEAD_DIM ** -0.5 * LOG2_E)
            lo = (j - 4) * tn
            d_refs[0][:, lo : lo + tn] = h.astype(BF16)
            src, dst = stage_ref.at[0], stage_ref.at[1]
            for s in range(tn // V7X_LANES):
                src[s] = h[:, s * V7X_LANES : (s + 1) * V7X_LANES]
            prev_r = 1
            for d_ref, (_, r) in zip(d_refs[1:], DSA_PATTERNS[1:]):
                step = r // prev_r
                n_rows = tm // r
                for rho_prev in range(prev_r):
                    for a in range(step):
                        rho = rho_prev + prev_r * a
                        for s in range(tn // V7X_LANES):
                            rows = src.at[s][pl.ds(rho_prev * (tm // prev_r) + a, n_rows, stride=step), :]
                            if r != DSA_PATTERNS[-1][1]:
                                dst[s, rho * n_rows : (rho + 1) * n_rows, :] = rows
                            d_ref[0, rho, :, lo + s * V7X_LANES : lo + (s + 1) * V7X_LANES] = rows.astype(BF16)
                src, dst, prev_r = dst, src, r

    _lookahead(w_ref.shape[1] // tn, lambda j: jnp.dot(xb, w_ref[:, j * tn : (j + 1) * tn], preferred_element_type=F32), consume)


def _even_in_proj(x2d, seq_len, w, cos_t, sin_t):
    T, K = x2d.shape
    tm = min(PROJ_ROW_TILE, seq_len)
    tps = seq_len // tm
    B = T // seq_len
    row = lambda i: (i, 0)
    pos = pl.BlockSpec((tm, RET_HEAD_DIM), lambda i: (i % tps, 0))
    shapes = [(T, RET_W), (RET_W, T), (T, RET_W), (T, RET_W), (T, 3 * DSA_W)]
    dtypes = [BF16, BF16, BF16, BF16, BF16]
    specs = [pl.BlockSpec((tm, RET_W), row), pl.BlockSpec((RET_W, tm), lambda i: (0, i)),
             pl.BlockSpec((tm, RET_W), row), pl.BlockSpec((tm, RET_W), row), pl.BlockSpec((tm, 3 * DSA_W), row)]
    for _, r in DSA_PATTERNS[1:]:
        shapes.append((B, r, seq_len // r, 3 * DSA_W))
        dtypes.append(BF16)
        specs.append(pl.BlockSpec((1, r, tm // r, 3 * DSA_W), lambda i: (i // tps, 0, i % tps, 0)))
    return pl.pallas_call(
        _even_in_proj_kernel,
        grid=(T // tm,),
        in_specs=[pl.BlockSpec((tm, K), row), _resident(w.shape), pos, pos],
        out_specs=specs,
        out_shape=[jax.ShapeDtypeStruct(s, dt) for s, dt in zip(shapes, dtypes)],
        scratch_shapes=[pltpu.VMEM((2, RET_W // V7X_LANES, tm, V7X_LANES), F32)],
        compiler_params=_params(("parallel",)),
        name="even_in_proj",
    )(x2d, w, cos_t, sin_t)


def _odd_in_proj_kernel(x_ref, w_ref, wg_ref, gb_ref, cw_ref, q_ref, kt_ref, v_ref, o_ref, gt_ref, carry_ref, *, tiles_per_seq):
    tn = PROJ_CHUNK
    per = MLSTM_W // tn

    @pl.when(pl.program_id(0) % tiles_per_seq == 0)
    def _():
        carry_ref[...] = jnp.zeros_like(carry_ref)

    xb = x_ref[...].astype(BF16)
    tm = xb.shape[0]

    order = [c for pair in zip(range(2 * per), range(2 * per, 4 * per)) for c in pair]

    def produce(i):
        j = order[i]
        return jnp.dot(xb, w_ref[:, j * tn : (j + 1) * tn], preferred_element_type=F32)

    def consume(i, h):
        j = order[i]
        sl = slice(j * tn, (j + 1) * tn)
        if j < 2 * per:
            prev = carry_ref[:, sl]
            carry_ref[:, sl] = h[tm - V7X_SUBLANES :, :]
            act = _silu(_causal_dwconv(h, prev, cw_ref.at[:, sl], MLSTM_CONV))
            if j < per:
                q_ref[:, sl] = act.astype(BF16)
            else:
                kt_ref[(j - per) * tn : (j - per + 1) * tn, :] = (act * (MLSTM_HEAD_DIM ** -0.5)).T.astype(BF16)
        elif j < 3 * per:
            v_ref[:, (j - 2 * per) * tn : (j - 2 * per + 1) * tn] = h.astype(BF16)
        else:
            o_ref[:, (j - 3 * per) * tn : (j - 3 * per + 1) * tn] = jax.nn.sigmoid(h).astype(BF16)

    _lookahead(4 * per, produce, consume)
    gates = (jnp.dot(xb, wg_ref[...], preferred_element_type=F32) + gb_ref[...]).T[: 2 * MLSTM_HEADS, :]
    gate_id = lax.broadcasted_iota(jnp.int32, (2 * MLSTM_HEADS, 1), 0)
    gt_ref[...] = jnp.where(gate_id < MLSTM_HEADS, gates, _log_sigmoid(gates))


def _odd_in_proj(x2d, seq_len, w, wg, gb, conv_w):
    T, K = x2d.shape
    tm = min(PROJ_ROW_TILE, seq_len)
    W = MLSTM_W
    cw = jnp.zeros((V7X_SUBLANES, 2 * W), F32).at[:MLSTM_CONV].set(conv_w)
    row = lambda i: (i, 0)
    shapes = ((T, W), (W, T), (T, W), (T, W), (2 * MLSTM_HEADS, T))
    dtypes = (BF16, BF16, BF16, BF16, F32)
    return pl.pallas_call(
        functools.partial(_odd_in_proj_kernel, tiles_per_seq=seq_len // tm),
        grid=(T // tm,),
        in_specs=[pl.BlockSpec((tm, K), row), _resident(w.shape), _resident(wg.shape), _resident(gb.shape),
                  _resident(cw.shape)],
        out_specs=[pl.BlockSpec((tm, W), row), pl.BlockSpec((W, tm), lambda i: (0, i)), pl.BlockSpec((tm, W), row),
                   pl.BlockSpec((tm, W), row), pl.BlockSpec((2 * MLSTM_HEADS, tm), lambda i: (0, i))],
        out_shape=[jax.ShapeDtypeStruct(s, dt) for s, dt in zip(shapes, dtypes)],
        scratch_shapes=[pltpu.VMEM((V7X_SUBLANES, 2 * MLSTM_W), F32)],
        compiler_params=_params(("arbitrary",)),
        name="odd_in_proj",
    )(x2d, w, wg, gb, cw)


def _tail_kernel(*refs, n_y, tiles_per_seq):
    x_ref = refs[0]
    y_refs = refs[1 : 1 + n_y]
    (wout_ref, g1_ref, b1_ref, wup_ref, cw_ref, cb_ref, wdn_ref, g_ref, b_ref,
     o_ref, carry_ref, act_ref, x1_ref) = refs[1 + n_y :]

    @pl.when(pl.program_id(0) % tiles_per_seq == 0)
    def _():
        carry_ref[...] = jnp.zeros_like(carry_ref)

    tm = x_ref.shape[0]
    for lo in range(0, tm, tm // 2):
        rows = slice(lo, lo + tm // 2)
        acc = None
        row = 0
        for y_ref in y_refs:
            kw = y_ref.shape[1]
            part = jnp.dot(y_ref[rows, :], wout_ref[row : row + kw, :], preferred_element_type=F32)
            acc = part if acc is None else acc + part
            row += kw
        x1_ref[rows, :] = _layer_norm(DEEPNORM_ALPHA * x_ref[rows, :] + acc, g1_ref[...], b1_ref[...])

    xb = x1_ref[...].astype(BF16)

    def up_dots(j):
        lo = j * FFN_CHUNK
        return (jnp.dot(xb, wup_ref[:, lo : lo + FFN_CHUNK], preferred_element_type=F32),
                jnp.dot(xb, wup_ref[:, D_FF + lo : D_FF + lo + FFN_CHUNK], preferred_element_type=F32))

    def activate(j, gate_up):
        gate, up = gate_up
        sl = slice(j * FFN_CHUNK, (j + 1) * FFN_CHUNK)
        prev = carry_ref[:, sl]
        carry_ref[:, sl] = gate[tm - V7X_SUBLANES :, :]
        conv = _causal_dwconv(gate, prev, cw_ref.at[:, sl], FFN_CONV) + cb_ref[:, sl]
        act_ref[:, sl] = (_silu(conv) * up).astype(BF16)

    _lookahead(D_FF // FFN_CHUNK, up_dots, activate)
    for lo in range(0, tm, tm // 2):
        rows = slice(lo, lo + tm // 2)
        ffn = jnp.dot(act_ref[rows, :], wdn_ref[...], preferred_element_type=F32)
        z = DEEPNORM_ALPHA * x1_ref[rows, :] + ffn
        o_ref[rows, :] = _layer_norm(z, g_ref[...], b_ref[...])


def _layer_tail(x2d, ys, seq_len, layer, w_out, ln1, w_up, conv_w, conv_b, w_down, ln2):
    T, D = x2d.shape
    tm = min(ROW_TILE, seq_len)
    cw = jnp.zeros((V7X_SUBLANES, D_FF), F32).at[:FFN_CONV].set(conv_w)
    row = lambda i: (i, 0)
    vec = lambda v: v.reshape(1, -1)
    return pl.pallas_call(
        functools.partial(_tail_kernel, n_y=len(ys), tiles_per_seq=seq_len // tm),
        grid=(T // tm,),
        in_specs=[pl.BlockSpec((tm, D), row)]
        + [pl.BlockSpec((tm, y.shape[1]), row) for y in ys]
        + [_resident(w_out.shape), _resident((1, D)), _resident((1, D)),
           _resident_layer(w_up.shape, layer), _resident(cw.shape), _resident((1, D_FF)),
           _resident_layer(w_down.shape, layer), _resident((1, D)), _resident((1, D))],
        out_specs=pl.BlockSpec((tm, D), row),
        out_shape=jax.ShapeDtypeStruct((T, D), F32),
        scratch_shapes=[pltpu.VMEM((V7X_SUBLANES, D_FF), F32), pltpu.VMEM((tm, D_FF), BF16), pltpu.VMEM((tm, D), F32)],
        compiler_params=_params(("arbitrary",)),
        name="layer_tail",
    )(x2d, *ys, w_out, vec(ln1[0]), vec(ln1[1]), w_up, cw, vec(conv_b), w_down, vec(ln2[0]), vec(ln2[1]))


def _retention_tables(seq_len):
    H, C, d = RET_HEADS, RET_CHUNK, RET_HEAD_DIM
    scale = d ** -0.5
    log_gamma = jnp.log1p(-jnp.exp2(-5.0 - jnp.arange(H, dtype=F32)))
    idx = jnp.arange(C, dtype=F32)
    rel = idx[:, None] - idx[None, :]
    decay = jnp.where(rel >= 0, jnp.exp(log_gamma[:, None, None] * jnp.maximum(rel, 0.0)), 0.0) * scale
    k_w = jnp.exp(log_gamma[:, None] * (C - 1 - idx)[None, :]) * scale
    q_w = jnp.exp(log_gamma[:, None] * (idx + 1.0)[None, :])
    kw_t = jnp.broadcast_to(k_w[:, :, None], (H, C, d))
    qw_t = jnp.broadcast_to(q_w[:, :, None], (H, C, d))
    cd_t = jnp.broadcast_to(jnp.exp(log_gamma * C)[:, None, None], (H, 1, d))
    inv = np.float32(1.0) / np.float32(10000.0) ** (np.arange(0, d, 2, dtype=np.float32) / np.float32(d))
    ang = (np.arange(seq_len, dtype=np.float32)[:, None] * inv[None, :]).astype(np.float64)
    cos, sin = np.cos(ang).astype(np.float32), np.sin(ang).astype(np.float32)
    cos_t = jnp.asarray(np.concatenate([cos, cos], -1), F32)
    sin_t = jnp.asarray(np.concatenate([-sin, sin], -1), F32)
    return cos_t, sin_t, decay, kw_t, qw_t, cd_t


def _retention_kernel(q_ref, kt_ref, v_ref, g_ref, dec_ref, kw_ref, qw_ref, cd_ref, o_ref, r_ref):
    @pl.when(pl.program_id(2) == 0)
    def _():
        r_ref[...] = jnp.zeros_like(r_ref)

    C = RET_CHUNK
    decay = dec_ref[0]
    kw = kw_ref[0]
    qw = qw_ref[0]
    cd = cd_ref[0]

    def local(n):
        sl = slice(n * C, (n + 1) * C)
        qb = q_ref[0, sl, :]
        kt = kt_ref[:, sl]
        vb = v_ref[0, sl, :]
        scores = jnp.dot(qb, kt, preferred_element_type=F32) * decay
        y_intra = jnp.dot(scores.astype(BF16), vb, preferred_element_type=F32)
        kv = jnp.dot(kt, (vb.astype(F32) * kw).astype(BF16), preferred_element_type=F32)
        q_dec = (qb.astype(F32) * qw).astype(BF16)
        return y_intra, kv, q_dec

    def recurrent(n, staged):
        y_intra, kv, q_dec = staged
        sl = slice(n * C, (n + 1) * C)
        r = r_ref[...]
        y = y_intra + jnp.dot(q_dec, r.astype(BF16), preferred_element_type=F32)
        r_ref[...] = cd * r + kv
        mu = jnp.mean(y, -1, keepdims=True)
        yc = y - mu
        var = jnp.mean(yc * yc, -1, keepdims=True)
        o_ref[0, sl, :] = (yc * lax.rsqrt(var + LN_EPS) * g_ref[0, sl, :]).astype(o_ref.dtype)

    _lookahead(q_ref.shape[1] // C, local, recurrent)


def _retention(q3, kt, v3, g3, tables):
    B, S, _ = v3.shape
    H, d, C = RET_HEADS, RET_HEAD_DIM, RET_CHUNK
    ts = min(SEQ_TILE, S)
    tiles = S // ts
    _, _, decay, kw_t, qw_t, cd_t = tables
    col = pl.BlockSpec((1, ts, d), lambda b, h, c: (b, c, h))
    per_head = lambda shape: pl.BlockSpec((1,) + shape, lambda b, h, c: (h, 0, 0))
    return pl.pallas_call(
        _retention_kernel,
        grid=(B, H, tiles),
        in_specs=[col, pl.BlockSpec((d, ts), lambda b, h, c: (h, b * tiles + c)), col, col,
                  per_head((C, C)), per_head((C, d)), per_head((C, d)), per_head((1, d))],
        out_specs=pl.BlockSpec((1, ts, d), lambda b, h, c: (b, c, h)),
        out_shape=jax.ShapeDtypeStruct((B, S, H * d), BF16),
        scratch_shapes=[pltpu.VMEM((d, d), F32)],
        compiler_params=_params(("parallel", "parallel", "arbitrary")),
        name="retention",
    )(q3, kt, v3, g3, decay, kw_t, qw_t, cd_t)


def _t5_bucket(dist):
    exact = T5_BUCKETS // 2
    n = jnp.maximum(dist, 0)
    large = exact + (jnp.log(jnp.maximum(n, 1).astype(F32) / exact) / math.log(T5_MAX_DIST / exact) * (T5_BUCKETS - exact)).astype(jnp.int32)
    large = jnp.minimum(large, T5_BUCKETS - 1)
    return jnp.where(n < exact, n, large)


def _dsa_bucket_tables():
    blk = DSA_BLOCK
    qi = jnp.arange(blk)[:, None]
    kj = jnp.arange(2 * blk)[None, :]
    tabs = []
    for window, dilation in DSA_PATTERNS:
        assert window // dilation <= blk
        dist = qi + blk - kj
        valid = (dist >= 0) & (dist <= window // dilation)
        tabs.append(jnp.where(valid, _t5_bucket(dist * dilation), -1))
    return jnp.stack(tabs).astype(jnp.int32)


def _dsa_bias_kernel(rb_ref, bk_ref, o_ref):
    blk = DSA_BLOCK
    rows = 4 * V7X_SUBLANES
    masked = jnp.full((rows, blk), -jnp.inf, F32)
    for lo in range(0, blk, rows):
        bk = bk_ref[0, lo : lo + rows, :]
        accs = [jnp.full(bk.shape, -jnp.inf, F32) for _ in range(DSA_HEADS)]
        for b in range(T5_BUCKETS):
            hit = bk == b
            accs = [jnp.where(hit, rb_ref[b, h] * LOG2_E, acc) for h, acc in enumerate(accs)]
        for h, acc in enumerate(accs):
            o_ref[0, 0, h, lo : lo + rows, :] = acc
            o_ref[0, 1, h, lo : lo + rows, :] = jnp.concatenate([acc[:, blk:], masked], axis=1)


def _dsa_bias(rel_bias):
    buckets = _dsa_bucket_tables()
    P, blk = len(DSA_PATTERNS), DSA_BLOCK
    return pl.pallas_call(
        _dsa_bias_kernel,
        grid=(P,),
        in_specs=[pl.BlockSpec(memory_space=pltpu.SMEM), pl.BlockSpec((1, blk, 2 * blk), lambda p: (p, 0, 0))],
        out_specs=pl.BlockSpec((1, 2, DSA_HEADS, blk, 2 * blk), lambda p: (p, 0, 0, 0, 0)),
        out_shape=jax.ShapeDtypeStruct((P, 2, DSA_HEADS, blk, 2 * blk), F32),
        compiler_params=_params(("parallel",)),
        name="dsa_bias",
    )(rel_bias, buckets)


def _dsa_kernel(*refs):
    P = len(DSA_PATTERNS)
    q_refs, k_refs, v_refs = refs[:P], refs[P : 2 * P], refs[2 * P : 3 * P]
    bias_ref, o_ref, acc_o, acc_m, acc_l = refs[3 * P :]
    blk = DSA_BLOCK
    tq = o_ref.shape[1]
    t = pl.program_id(2)
    lane = lax.broadcasted_iota(jnp.int32, (1, V7X_LANES), 1)
    head0 = lane < DSA_HEAD_DIM

    def block_logits(p, r, idx):
        blocks_per_residue = tq // (blk * r)
        rho = idx // blocks_per_residue
        nb = idx % blocks_per_residue
        l0 = pl.multiple_of(t * (tq // r) + nb * blk, blk)
        first = l0 == 0
        k_row = pl.multiple_of(jnp.where(first, l0, l0 - blk), blk)
        qb = q_refs[p][0, rho, pl.ds(l0, blk), :]
        zero = jnp.zeros_like(qb)
        q2 = jnp.concatenate([jnp.where(head0, qb, zero), jnp.where(head0, zero, qb)], axis=0)
        bias = bias_ref[p, pl.ds(first.astype(jnp.int32), 1), 0][0]
        kb = k_refs[p][0, rho, pl.ds(k_row, 2 * blk), :]
        logits = lax.dot_general(q2, kb, _NT, preferred_element_type=F32) + bias
        start = rho + nb * (blk * r)
        rows = pl.ds(start, blk, stride=r) if r > 1 else pl.ds(start, blk)
        return logits, (rho, k_row), rows

    def block_output(p, logits, key_rows, rows):
        rho, k_row = key_rows
        m = jnp.max(logits, -1, keepdims=True)
        e = jnp.exp2(logits - m).astype(BF16)
        vb = v_refs[p][0, rho, pl.ds(k_row, 2 * blk), :]
        o2 = jnp.dot(e, jnp.concatenate([vb, jnp.ones_like(vb)], axis=1), preferred_element_type=F32)
        o_p = jnp.where(head0, o2[:blk, :V7X_LANES], o2[blk:, :V7X_LANES])
        m_p = jnp.where(head0, m[:blk], m[blk:])
        l_p = jnp.where(head0, o2[:blk, V7X_LANES:], o2[blk:, V7X_LANES:])
        if p > 0:
            acc_o.at[p - 1][rows, :] = o_p
            acc_m.at[p - 1][rows, :] = m_p
            acc_l.at[p - 1][rows, :] = l_p
            return
        os_ = [o_p] + [acc_o[b, rows, :] for b in range(P - 1)]
        ms = [m_p] + [acc_m[b, rows, :] for b in range(P - 1)]
        ls = [l_p] + [acc_l[b, rows, :] for b in range(P - 1)]
        m_all = functools.reduce(jnp.maximum, ms)
        ws = [jnp.exp2(m_b - m_all) for m_b in ms]
        num = functools.reduce(jnp.add, [w * o_b for w, o_b in zip(ws, os_)])
        den = functools.reduce(jnp.add, [w * l_b for w, l_b in zip(ws, ls)])
        o_ref[0, rows, :] = (num / den).astype(o_ref.dtype)

    blocks = [(p, DSA_PATTERNS[p][1], idx) for p in reversed(range(P)) for idx in range(tq // blk)]
    _lookahead(len(blocks), lambda i: block_logits(*blocks[i]), lambda i, staged: block_output(blocks[i][0], *staged))


def _dilated_attention(ds, bias):
    B, _, S, _ = ds[0].shape
    pairs = DSA_W // V7X_LANES
    tq = DSA_TILE
    assert S % tq == 0 and S >= 2 * tq
    P, blk = len(DSA_PATTERNS), DSA_BLOCK
    bias5 = bias.reshape(P, 2, pairs, 2 * blk, 2 * blk)

    def seq(r, off):
        return pl.BlockSpec((1, r, S // r, V7X_LANES), lambda b, p, t: (b, 0, 0, off + p))

    return pl.pallas_call(
        _dsa_kernel,
        grid=(B, pairs, S // tq),
        in_specs=[seq(r, off) for off in (0, pairs, 2 * pairs) for _, r in DSA_PATTERNS]
        + [pl.BlockSpec((P, 2, 1, 2 * blk, 2 * blk), lambda b, p, t: (0, 0, p, 0, 0))],
        out_specs=pl.BlockSpec((1, tq, V7X_LANES), lambda b, p, t: (b, t, p)),
        out_shape=jax.ShapeDtypeStruct((B, S, DSA_W), BF16),
        scratch_shapes=[pltpu.VMEM((P - 1, tq, V7X_LANES), F32)] * 3,
        compiler_params=_params(("parallel", "parallel", "arbitrary")),
        name="dilated_attention",
    )(*ds, *ds, *ds, bias5)


def _mlstm_kernel(q_ref, kt_ref, v_ref, og_ref, gr_ref, o_ref, c_ref, m_ref):
    h = pl.program_id(1)

    @pl.when(pl.program_id(2) == 0)
    def _():
        c_ref[...] = jnp.zeros_like(c_ref)
        m_ref[...] = jnp.zeros_like(m_ref)

    C = MLSTM_CHUNK
    dv = MLSTM_HEAD_DIM
    L = V7X_LANES
    n_chunks = q_ref.shape[1] // C
    ri = lax.broadcasted_iota(jnp.int32, (C, C), 0)
    ci = lax.broadcasted_iota(jnp.int32, (C, C), 1)
    causal = ci <= ri
    eye = ci == ri
    rep = lambda col: jnp.broadcast_to(col, (C, L))

    def gate_vectors(n):
        sl = slice(n * C, (n + 1) * C)
        gates = gr_ref[:, sl]
        gate_id = lax.broadcasted_iota(jnp.int32, (2 * MLSTM_HEADS, 1), 0)
        li_row = jnp.sum(jnp.where(gate_id == h, gates, 0.0), 0, keepdims=True)
        lf_row = jnp.sum(jnp.where(gate_id == MLSTM_HEADS + h, gates, 0.0), 0, keepdims=True)
        li_col = jnp.sum(jnp.where(eye, li_row, 0.0), -1, keepdims=True)
        a_col = jnp.sum(jnp.where(causal, lf_row, 0.0), -1, keepdims=True)
        g_tot = jnp.sum(lf_row, -1, keepdims=True)
        return li_row, li_col, a_col, g_tot

    def decay_matrix(gv):
        li_row, li_col, a_col, g_tot = gv
        a_rep = rep(a_col)
        a_row = jnp.sum(jnp.where(eye, a_rep, 0.0), 0, keepdims=True)
        w_state = g_tot - a_col + li_col
        m_loc = jnp.max(w_state, 0, keepdims=True)
        ew_rep = rep(jnp.exp(w_state - m_loc))
        dmat = jnp.where(causal, a_rep - a_row + li_row, -jnp.inf)
        d_max = jnp.max(dmat, -1, keepdims=True)
        return a_rep, g_tot, m_loc, ew_rep, dmat, d_max

    def local_matmuls(n, dm, scalars):
        a_rep, _, _, ew_rep, dmat, d_max = dm
        m_s, a_old, a_new = scalars
        ew_rep = ew_rep * a_new
        sl = slice(n * C, (n + 1) * C)
        qb = q_ref[0, sl, :]
        kt = kt_ref[:, sl]
        vb = v_ref[0, sl, :]
        d_rep = rep(d_max)
        p = (jnp.exp(dmat - d_rep) * jnp.dot(qb, kt, preferred_element_type=F32)).astype(BF16)
        pv = jnp.dot(p, jnp.concatenate([vb, jnp.ones((C, L), BF16)], axis=1), preferred_element_type=F32)
        vw = jnp.concatenate([vb.astype(F32) * jnp.concatenate([ew_rep] * (dv // L), axis=1), ew_rep], axis=1)
        kv_ext = jnp.dot(kt, vw.astype(BF16), preferred_element_type=F32)
        return qb, pv, kv_ext, a_rep, d_rep, m_s, a_old

    def recurrent(n, staged):
        qb, pv, kv_ext, a_rep, d_rep, m_s, a_old = staged
        sl = slice(n * C, (n + 1) * C)
        c_s = c_ref[...]
        inter = jnp.dot(qb, c_s.astype(BF16), preferred_element_type=F32)
        inter_log = a_rep + m_s
        m_row = jnp.maximum(inter_log, d_rep)
        f_intra = jnp.exp(d_rep - m_row)
        f_inter = jnp.exp(inter_log - m_row)
        den = f_intra * pv[:, dv:] + f_inter * inter[:, dv:]
        r = 1.0 / jnp.maximum(jnp.abs(den), jnp.exp(-m_row))
        for lo in range(0, dv, L):
            num = f_intra * pv[:, lo : lo + L] + f_inter * inter[:, lo : lo + L]
            o_ref[0, sl, lo : lo + L] = (num * r * og_ref[0, sl, lo : lo + L]).astype(o_ref.dtype)
        c_ref[...] = a_old * c_s + kv_ext

    gvs = [gate_vectors(n) for n in range(n_chunks)]
    dms = [decay_matrix(gv) for gv in gvs]
    m_s = m_ref[...][:, 0:1]
    scalars = []
    for _, g_tot, m_loc, _, _, _ in dms:
        m_new = jnp.maximum(g_tot + m_s, m_loc)
        scalars.append((m_s, jnp.exp(g_tot + m_s - m_new), jnp.exp(m_loc - m_new)))
        m_s = m_new
    m_ref[...] = jnp.broadcast_to(m_s, m_ref.shape)
    _lookahead(n_chunks, lambda n: local_matmuls(n, dms[n], scalars[n]), recurrent)


def _mlstm(q3, kt, v3, o3, gates_row):
    B, S, _ = v3.shape
    H, d = MLSTM_HEADS, MLSTM_HEAD_DIM
    ts = min(SEQ_TILE, S)
    tiles = S // ts
    col = pl.BlockSpec((1, ts, d), lambda b, h, c: (b, c, h))
    return pl.pallas_call(
        _mlstm_kernel,
        grid=(B, H, tiles),
        in_specs=[col, pl.BlockSpec((d, ts), lambda b, h, c: (h, b * tiles + c)), col, col,
                  pl.BlockSpec((2 * H, ts), lambda b, h, c: (0, b * tiles + c))],
        out_specs=pl.BlockSpec((1, ts, d), lambda b, h, c: (b, c, h)),
        out_shape=jax.ShapeDtypeStruct((B, S, H * d), BF16),
        scratch_shapes=[pltpu.VMEM((d, d + V7X_LANES), F32), pltpu.VMEM((1, V7X_LANES), F32)],
        compiler_params=_params(("parallel", "parallel", "arbitrary")),
        name="mlstm",
    )(q3, kt, v3, o3, gates_row)


def _even_mixer(x2d, B, S, w_in, rel_bias):
    tables = _retention_tables(S)
    q, kt, v, gate, d_nat, *d_dil = _even_in_proj(x2d, S, w_in.astype(BF16), tables[0], tables[1])
    y_r = _retention(q.reshape(B, S, -1), kt, v.reshape(B, S, -1), gate.reshape(B, S, -1), tables)
    y_d = _dilated_attention([d_nat.reshape(B, 1, S, -1)] + d_dil, _dsa_bias(rel_bias))
    return [y_r.reshape(B * S, RET_W), y_d.reshape(B * S, DSA_W)]


def _odd_mixer(x2d, B, S, w_in, gate_b, conv_w):
    H = MLSTM_HEADS
    wide = 4 * MLSTM_W
    wg = jnp.zeros((D_MODEL, V7X_LANES), F32).at[:, : 2 * H].set(w_in[:, wide:]).astype(BF16)
    gb = jnp.zeros((1, V7X_LANES), F32).at[0, : 2 * H].set(gate_b)
    q, kt, v, og, gates = _odd_in_proj(x2d, S, w_in.astype(BF16), wg, gb, conv_w)
    y = _mlstm(q.reshape(B, S, -1), kt, v.reshape(B, S, -1), og.reshape(B, S, -1), gates)
    return [y.reshape(B * S, MLSTM_W)]


def kernel(x, even_w_in, even_w_out, rel_bias, odd_w_in, odd_gate_b, odd_conv_w, odd_w_out, ffn_w_up, ffn_conv_w, ffn_conv_b, ffn_w_down, ln_g, ln_b):
    B, S, D = x.shape
    x2d = x.reshape(B * S, D)
    w_up, w_down = ffn_w_up.astype(BF16), ffn_w_down.astype(BF16)
    for layer in range(DEPTH):
        j = layer // 2
        if layer % 2 == 0:
            ys, w_out = _even_mixer(x2d, B, S, even_w_in[j], rel_bias), even_w_out[j]
        else:
            ys, w_out = _odd_mixer(x2d, B, S, odd_w_in[j], odd_gate_b[j], odd_conv_w[j]), odd_w_out[j]
        x2d = _layer_tail(x2d, ys, S, layer, w_out.astype(BF16), (ln_g[layer, 0], ln_b[layer, 0]), w_up,
                          ffn_conv_w[layer], ffn_conv_b[layer], w_down, (ln_g[layer, 1], ln_b[layer, 1]))
    return x2d.reshape(B, S, D)
```
